```python
import math
import jax, jax.numpy as jnp
from jax import lax
import numpy as np

D_MODEL = 2048
BATCH = 1
SEQ = 16384
DEPTH = 1
DEC_BATCH = 32
DEC_SEQ = 32
PAST_LEN = 2048

CHUNK = 64
MIX_WIDTH = D_MODEL
ATT_WIDTH = MIX_WIDTH // 2
CONV_CH = MIX_WIDTH - ATT_WIDTH
HEAD_DIM = 64
V_DIM = 2 * HEAD_DIM
N_HEADS = ATT_WIDTH // V_DIM
QK_WIDTH = N_HEADS * 2 * HEAD_DIM
CONV_K = 3
ROPE_THETA = 10000.0
Q_BLOCK = 128
N_GROUPS = 8
EXPERTS_PER_GROUP = 8
N_EXPERTS = N_GROUPS * EXPERTS_PER_GROUP
TOP_K = 2
D_EXPERT = D_MODEL // 4
E_BLOCK = 128
NORM_EPS = 1e-6
IN_WIDTH = 2 * QK_WIDTH + ATT_WIDTH + 3 * CONV_CH
SPLITS = (QK_WIDTH, 2 * QK_WIDTH, 2 * QK_WIDTH + ATT_WIDTH,
          2 * QK_WIDTH + ATT_WIDTH + CONV_CH, 2 * QK_WIDTH + ATT_WIDTH + 2 * CONV_CH)

kernel_name = "hymba_diffattn_shortconv_hiermoe_stream_step"


def rms_norm(x, g):
    xf = x.astype(jnp.float32)
    y = xf * lax.rsqrt(jnp.mean(xf * xf, axis=-1, keepdims=True) + NORM_EPS)
    return (y * g.astype(jnp.float32)).astype(x.dtype)


def rope(x, pos):
    half = HEAD_DIM // 2
    inv = ROPE_THETA ** (-jnp.arange(half, dtype=jnp.float32) / half)
    ang = pos.astype(jnp.float32)[:, None] * inv[None, :]
    cos = jnp.cos(ang)[None, :, None, None, :]
    sin = jnp.sin(ang)[None, :, None, None, :]
    x1 = x[..., :half].astype(jnp.float32)
    x2 = x[..., half:].astype(jnp.float32)
    return jnp.concatenate([x1 * cos - x2 * sin, x2 * cos + x1 * sin], axis=-1).astype(x.dtype)


def _diff_attend_block(q, k, v, q_pos, k_pos, lam):
    s = jnp.einsum('bqhmd,bkhmd->bhmqk', q, k).astype(jnp.float32) * (HEAD_DIM ** -0.5)
    visible = (k_pos[None, :] // CHUNK) <= (q_pos[:, None] // CHUNK)
    s = jnp.where(visible[None, None, None], s, jnp.finfo(jnp.float32).min)
    p = jax.nn.softmax(s, axis=-1)
    a = p[:, :, 0] - lam * p[:, :, 1]
    return jnp.einsum('bhqk,bkhd->bqhd', a.astype(v.dtype), v)


def diff_attention(q, k, v, q_pos, k_pos, lam):
    b, nq = q.shape[0], q.shape[1]
    if nq <= Q_BLOCK or nq % Q_BLOCK != 0:
        return _diff_attend_block(q, k, v, q_pos, k_pos, lam)
    nb = nq // Q_BLOCK
    qb = jnp.moveaxis(q.reshape(b, nb, Q_BLOCK, N_HEADS, 2, HEAD_DIM), 1, 0)
    pb = q_pos.reshape(nb, Q_BLOCK)
    out = lax.map(lambda a: _diff_attend_block(a[0], k, v, a[1], k_pos, lam), (qb, pb))
    return jnp.moveaxis(out, 0, 1).reshape(b, nq, N_HEADS, V_DIM)


def mixer_sublayer(x, pos, k_past, v_past, conv_prev, lam, lambda_init,
                   norm_g, w_in, q_norm_g, k_norm_g, subln_g, conv_w, conv_norm_g, w_out):
    b, s, _ = x.shape
    xn = rms_norm(x, norm_g)
    proj = jnp.einsum('bsd,de->bse', xn, w_in)
    q, k, v, gate_b, gate_c, h = jnp.split(proj, SPLITS, axis=-1)
    q = rope(rms_norm(q.reshape(b, s, N_HEADS, 2, HEAD_DIM), q_norm_g), pos)
    k = rope(rms_norm(k.reshape(b, s, N_HEADS, 2, HEAD_DIM), k_norm_g), pos)
    v = v.reshape(b, s, N_HEADS, V_DIM)
    k_all = jnp.concatenate([k_past.astype(k.dtype), k], axis=1)
    v_all = jnp.concatenate([v_past.astype(v.dtype), v], axis=1)
    k_pos = jnp.concatenate([jnp.arange(k_past.shape[1], dtype=jnp.int32), pos])
    att = diff_attention(q, k_all, v_all, pos, k_pos, lam)
    att = (rms_norm(att, subln_g) * (1.0 - lambda_init)).reshape(b, s, ATT_WIDTH)
    u = gate_c * h
    u_pad = jnp.concatenate([conv_prev.astype(u.dtype), u], axis=1)
    conv = sum(u_pad[:, j:j + s] * conv_w[j] for j in range(CONV_K))
    y_conv = rms_norm(gate_b * conv, conv_norm_g)
    new_conv = u_pad[:, -(CONV_K - 1):]
    out = jnp.einsum('bse,ed->bsd', jnp.concatenate([att, y_conv], axis=-1), w_out)
    return x + out, k, v, new_conv


def routed_experts(xf, eidx, w, w_gate, w_up, w_down):
    T, d = xf.shape
    n = T * TOP_K
    flat_e = eidx.reshape(n)
    order = jnp.argsort(flat_e)
    se = flat_e[order]
    tok = order // TOP_K
    counts = jnp.bincount(flat_e, length=N_EXPERTS)
    padded = (counts + E_BLOCK - 1) // E_BLOCK * E_BLOCK
    pad_end = jnp.cumsum(padded)
    pad_start = pad_end - padded
    start = jnp.cumsum(counts) - counts
    dest = pad_start[se] + jnp.arange(n, dtype=jnp.int32) - start[se]
    n_blocks = -(-n // E_BLOCK) + N_EXPERTS
    cap = n_blocks * E_BLOCK
    row_tok = jnp.full((cap,), T, jnp.int32).at[dest].set(tok.astype(jnp.int32))
    row_w = jnp.zeros((cap,), jnp.float32).at[dest].set(w.reshape(n)[order])
    blk_e = jnp.minimum(jnp.searchsorted(pad_end, jnp.arange(n_blocks, dtype=jnp.int32) * E_BLOCK,
                                         side='right'), N_EXPERTS - 1)
    x_pad = jnp.concatenate([xf, jnp.zeros((1, d), xf.dtype)], axis=0)

    def run_block(args):
        toks, e = args
        xb = x_pad[toks]
        hb = jax.nn.silu(xb @ w_gate[e]) * (xb @ w_up[e])
        return hb @ w_down[e]

    out = lax.map(run_block, (row_tok.reshape(n_blocks, E_BLOCK), blk_e)).reshape(cap, d)
    out = out * row_w[:, None].astype(out.dtype)
    return jnp.zeros((T + 1, d), out.dtype).at[row_tok].add(out)[:T]


def moe_sublayer(x, norm_g, rg_w, rg_b, re_w, re_b, w_gate, w_up, w_down):
    b, s, d = x.shape
    T = b * s
    xf = rms_norm(x, norm_g).reshape(T, d)
    g_logits = (xf @ rg_w).astype(jnp.float32) + rg_b.astype(jnp.float32)
    grp = jnp.argmax(g_logits, axis=-1)
    g_w = jnp.take_along_axis(jax.nn.softmax(g_logits, axis=-1), grp[:, None], axis=-1)
    e_logits = ((xf @ re_w).astype(jnp.float32) + re_b.astype(jnp.float32)).reshape(
        T, N_GROUPS, EXPERTS_PER_GROUP)
    e_in = jnp.take_along_axis(e_logits, grp[:, None, None], axis=1)[:, 0]
    top_l, top_i = lax.top_k(e_in, TOP_K)
    w = jax.nn.softmax(top_l, axis=-1) * g_w
    eidx = grp[:, None].astype(jnp.int32) * EXPERTS_PER_GROUP + top_i.astype(jnp.int32)
    y = routed_experts(xf, eidx, w, w_gate, w_up, w_down)
    return x + y.reshape(b, s, d)


def setup_inputs(seed: int = 0) -> dict:
    key = jax.random.key(seed)
    ks = jax.random.split(key, 32)

    def nrm(k, shape, scale):
        return jax.random.normal(k, shape, jnp.float32) * scale

    def gain(k, shape):
        return 1.0 + 0.01 * jax.random.normal(k, shape, jnp.float32)

    return {
        "x_prompt": nrm(ks[0], (BATCH, SEQ, D_MODEL), 1.0),
        "x_sample": nrm(ks[1], (DEC_BATCH, DEC_SEQ, D_MODEL), 1.0),
        "cache_k": nrm(ks[2], (DEPTH, DEC_BATCH, PAST_LEN, N_HEADS, 2, HEAD_DIM), 1.0),
        "cache_v": nrm(ks[3], (DEPTH, DEC_BATCH, PAST_LEN, N_HEADS, V_DIM), 1.0),
        "state_conv": nrm(ks[4], (DEPTH, DEC_BATCH, CONV_K - 1, CONV_CH), 1.0),
        "norm_mix_g": gain(ks[5], (DEPTH, D_MODEL)),
        "w_in": nrm(ks[6], (DEPTH, D_MODEL, IN_WIDTH), D_MODEL ** -0.5),
        "q_norm_g": gain(ks[7], (DEPTH, HEAD_DIM)),
        "k_norm_g": gain(ks[8], (DEPTH, HEAD_DIM)),
        "lambda_q1": nrm(ks[9], (DEPTH, HEAD_DIM), 0.1),
        "lambda_k1": nrm(ks[10], (DEPTH, HEAD_DIM), 0.1),
        "lambda_q2": nrm(ks[11], (DEPTH, HEAD_DIM), 0.1),
        "lambda_k2": nrm(ks[12], (DEPTH, HEAD_DIM), 0.1),
        "subln_g": gain(ks[13], (DEPTH, V_DIM)),
        "conv_w": nrm(ks[14], (DEPTH, CONV_K, CONV_CH), CONV_K ** -0.5),
        "conv_norm_g": gain(ks[15], (DEPTH, CONV_CH)),
        "w_out": nrm(ks[16], (DEPTH, MIX_WIDTH, D_MODEL), MIX_WIDTH ** -0.5),
        "norm_ffn_g": gain(ks[17], (DEPTH, D_MODEL)),
        "router_group_w": nrm(ks[18], (DEPTH, D_MODEL, N_GROUPS), D_MODEL ** -0.5),
        "router_group_b": nrm(ks[19], (DEPTH, N_GROUPS), 0.01),
        "router_expert_w": nrm(ks[20], (DEPTH, D_MODEL, N_EXPERTS), D_MODEL ** -0.5),
        "router_expert_b": nrm(ks[21], (DEPTH, N_EXPERTS), 0.01),
        "expert_w_gate": nrm(ks[22], (DEPTH, N_EXPERTS, D_MODEL, D_EXPERT), D_MODEL ** -0.5),
        "expert_w_up": nrm(ks[23], (DEPTH, N_EXPERTS, D_MODEL, D_EXPERT), D_MODEL ** -0.5),
        "expert_w_down": nrm(ks[24], (DEPTH, N_EXPERTS, D_EXPERT, D_MODEL), D_EXPERT ** -0.5),
    }


def reference(x_prompt, x_sample, cache_k, cache_v, state_conv,
              norm_mix_g, w_in, q_norm_g, k_norm_g, lambda_q1, lambda_k1, lambda_q2, lambda_k2,
              subln_g, conv_w, conv_norm_g, w_out, norm_ffn_g,
              router_group_w, router_group_b, router_expert_w, router_expert_b,
              expert_w_gate, expert_w_up, expert_w_down):
    bp, sp, _ = x_prompt.shape
    ss = x_sample.shape[1]
    past = cache_k.shape[2]
    pos_p = jnp.arange(sp, dtype=jnp.int32)
    pos_s = past + jnp.arange(ss, dtype=jnp.int32)
    empty_k = jnp.zeros((bp, 0, N_HEADS, 2, HEAD_DIM), x_prompt.dtype)
    empty_v = jnp.zeros((bp, 0, N_HEADS, V_DIM), x_prompt.dtype)
    zero_conv = jnp.zeros((bp, CONV_K - 1, CONV_CH), x_prompt.dtype)

    h_p, h_s = x_prompt, x_sample
    kp_l, vp_l, cp_l, ks_l, vs_l, cs_l = [], [], [], [], [], []
    for l in range(DEPTH):
        lambda_init = 0.8 - 0.6 * math.exp(-0.3 * l)
        lam = (jnp.exp(jnp.sum(lambda_q1[l].astype(jnp.float32) * lambda_k1[l].astype(jnp.float32)))
               - jnp.exp(jnp.sum(lambda_q2[l].astype(jnp.float32) * lambda_k2[l].astype(jnp.float32)))
               + lambda_init)
        mix_w = (norm_mix_g[l], w_in[l], q_norm_g[l], k_norm_g[l], subln_g[l],
                 conv_w[l], conv_norm_g[l], w_out[l])
        h_p, k_p, v_p, c_p = mixer_sublayer(h_p, pos_p, empty_k, empty_v, zero_conv,
                                            lam, lambda_init, *mix_w)
        h_s, k_s, v_s, c_s = mixer_sublayer(h_s, pos_s, cache_k[l], cache_v[l], state_conv[l],
                                            lam, lambda_init, *mix_w)
        ffn_w = (norm_ffn_g[l], router_group_w[l], router_group_b[l], router_expert_w[l],
                 router_expert_b[l], expert_w_gate[l], expert_w_up[l], expert_w_down[l])
        h_p = moe_sublayer(h_p, *ffn_w)
        h_s = moe_sublayer(h_s, *ffn_w)
        kp_l.append(k_p); vp_l.append(v_p); cp_l.append(c_p)
        ks_l.append(k_s); vs_l.append(v_s); cs_l.append(c_s)

    k_prompt_new = jnp.stack(kp_l)
    v_prompt_new = jnp.stack(vp_l)
    conv_prompt_new = jnp.stack(cp_l)
    k_sample_new = jnp.stack(ks_l)
    v_sample_new = jnp.stack(vs_l)
    conv_sample_new = jnp.stack(cs_l)
    return (h_p, h_s, k_prompt_new, v_prompt_new, conv_prompt_new,
            k_sample_new, v_sample_new, conv_sample_new)
```

```python
import functools
import math

import jax
import jax.numpy as jnp
from jax import lax
from jax.experimental import pallas as pl
from jax.experimental.pallas import tpu as pltpu

F32 = jnp.float32
BF16 = jnp.bfloat16

D_MODEL = 2048
CHUNK = 64
HEAD_DIM = 64
V_DIM = 2 * HEAD_DIM
N_HEADS = 8
QK_WIDTH = N_HEADS * 2 * HEAD_DIM
ATT_WIDTH = N_HEADS * V_DIM
CONV_CH = 1024
CONV_K = 3
ROPE_THETA = 10000.0
N_GROUPS = 8
EXPERTS_PER_GROUP = 8
N_EXPERTS = N_GROUPS * EXPERTS_PER_GROUP
TOP_K = 2
D_EXPERT = D_MODEL // 4
NORM_EPS = 1e-6
SECTION = 1024
N_SECTIONS = 6

LANES = 128
IN_TN = 512
IN_NJ = N_SECTIONS * SECTION // IN_TN
ATT_T = 512
OUT_TM = 256
MOE_BM = 256
DISPATCH_TM = 512
ROUTE_LANES = LANES
MASKED = -1e30
MIB = 1024 * 1024


def _params(sem, vmem_mib):
    return pltpu.CompilerParams(dimension_semantics=sem, vmem_limit_bytes=vmem_mib * MIB)


def _inproj_kernel(x_ref, ng_ref, w_ref, qg_ref, kg_ref, gmat_ref, cos_ref, sin_ref,
                   prev_ref, cw_ref, cng_ref,
                   q_ref, kf_ref, kb_ref, vf_ref, vb_ref, yc_ref, tail_ref,
                   xn_s, gb_s, c_s, z_s, u_s, *, nseq, carry, k_transposed):
    i = pl.program_id(0)
    j = pl.program_id(1)
    tm = x_ref.shape[0]
    seq = tm // nseq

    @pl.when(j == 0)
    def _():
        x = x_ref[...]
        inv = lax.rsqrt(jnp.mean(x * x, axis=-1, keepdims=True) + NORM_EPS)
        xn_s[...] = (x * inv * ng_ref[...]).astype(BF16)

    acc = jnp.dot(xn_s[...], w_ref[...], preferred_element_type=F32)

    def head_norm_rope(a, g):
        ms = jnp.dot((a * a).astype(BF16), gmat_ref[...], preferred_element_type=F32)
        y = a * lax.rsqrt(ms + NORM_EPS) * g
        reps = IN_TN // LANES
        cos = jnp.concatenate([cos_ref[...]] * reps, axis=1)
        sin = jnp.concatenate([sin_ref[...]] * reps, axis=1)
        lane = lax.broadcasted_iota(jnp.int32, y.shape, 1)
        first = (lane & (HEAD_DIM - 1)) < HEAD_DIM // 2
        partner = jnp.where(first, pltpu.roll(y, IN_TN - HEAD_DIM // 2, 1),
                            pltpu.roll(y, HEAD_DIM // 2, 1))
        return y * cos + partner * sin

    for jj in range(IN_NJ):
        section, half = divmod(jj, SECTION // IN_TN)
        cols = slice(half * IN_TN, (half + 1) * IN_TN)

        @pl.when(j == jj)
        def _(section=section, half=half, cols=cols):
            if section == 0:
                q = head_norm_rope(acc, qg_ref[:, cols]) * (HEAD_DIM ** -0.5)
                q_ref[:, cols] = q.astype(BF16)
            elif section == 1:
                k = head_norm_rope(acc, kg_ref[:, cols])
                kf_ref[:, cols] = k
                if k_transposed:
                    kb_ref[0, cols, :] = k.T.astype(BF16)
                else:
                    kb_ref[:, cols] = k.astype(BF16)
            elif section == 2:
                vf_ref[:, cols] = acc
                vb_ref[:, cols] = acc.astype(BF16)
            elif section == 3:
                gb_s[:, cols] = acc
            elif section == 4:
                c_s[:, cols] = acc
            else:
                u = c_s[:, cols] * acc
                w0 = cw_ref[0:1, cols]
                w1 = cw_ref[1:2, cols]
                w2 = cw_ref[2:3, cols]
                for s in range(nseq):
                    rows = slice(s * seq, (s + 1) * seq)
                    if carry:
                        @pl.when(i == 0)
                        def _():
                            u_s[6:8, cols] = prev_ref[0, :, cols]
                    else:
                        u_s[6:8, cols] = prev_ref[s, :, cols]
                    u_s[8:8 + seq, cols] = u[rows]
                    conv = (w0 * u_s[6:6 + seq, cols] + w1 * u_s[7:7 + seq, cols]
                            + w2 * u_s[8:8 + seq, cols])
                    z_s[rows, cols] = gb_s[rows, cols] * conv
                    tail = u_s[seq + 6:seq + 8, cols]
                    tail_ref[s, :, cols] = tail
                    if carry:
                        u_s[6:8, cols] = tail
                if half == SECTION // IN_TN - 1:
                    z = z_s[...]
                    inv = lax.rsqrt(jnp.mean(z * z, axis=-1, keepdims=True) + NORM_EPS)
                    yc_ref[...] = (z * inv * cng_ref[...]).astype(BF16)


def _inproj(x2d, norm_g, w_in_bf, qg, kg, gmat, cos, sin, conv_prev, conv_w, conv_norm_g,
            *, tm, nseq, carry, k_transposed):
    t = x2d.shape[0]
    ni = t // tm
    seq = tm // nseq
    row = lambda i, j: (i, 0)
    const = lambda i, j: (0, 0)
    if k_transposed:
        kb_shape = jax.ShapeDtypeStruct((ni, QK_WIDTH, tm), BF16)
        kb_spec = pl.BlockSpec((1, QK_WIDTH, tm), lambda i, j: (i, 0, 0))
    else:
        kb_shape = jax.ShapeDtypeStruct((t, QK_WIDTH), BF16)
        kb_spec = pl.BlockSpec((tm, QK_WIDTH), row)
    if carry:
        prev_spec = pl.BlockSpec((1, CONV_K - 1, CONV_CH), lambda i, j: (0, 0, 0))
        tail_shape = jax.ShapeDtypeStruct((1, CONV_K - 1, CONV_CH), F32)
        tail_spec = pl.BlockSpec((1, CONV_K - 1, CONV_CH), lambda i, j: (0, 0, 0))
    else:
        prev_spec = pl.BlockSpec((nseq, CONV_K - 1, CONV_CH), lambda i, j: (i, 0, 0))
        tail_shape = jax.ShapeDtypeStruct((ni * nseq, CONV_K - 1, CONV_CH), F32)
        tail_spec = pl.BlockSpec((nseq, CONV_K - 1, CONV_CH), lambda i, j: (i, 0, 0))
    kern = functools.partial(_inproj_kernel, nseq=nseq, carry=carry, k_transposed=k_transposed)
    return pl.pallas_call(
        kern,
        grid=(ni, IN_NJ),
        in_specs=[
            pl.BlockSpec((tm, D_MODEL), row),
            pl.BlockSpec((1, D_MODEL), const),
            pl.BlockSpec((D_MODEL, IN_TN), lambda i, j: (0, j)),
            pl.BlockSpec((1, SECTION), const),
            pl.BlockSpec((1, SECTION), const),
            pl.BlockSpec((IN_TN, IN_TN), const),
            pl.BlockSpec((tm, LANES), row),
            pl.BlockSpec((tm, LANES), row),
            prev_spec,
            pl.BlockSpec((CONV_K, CONV_CH), const),
            pl.BlockSpec((1, CONV_CH), const),
        ],
        out_specs=[
            pl.BlockSpec((tm, QK_WIDTH), row),
            pl.BlockSpec((tm, QK_WIDTH), row),
            kb_spec,
            pl.BlockSpec((tm, ATT_WIDTH), row),
            pl.BlockSpec((tm, ATT_WIDTH), row),
            pl.BlockSpec((tm, CONV_CH), row),
            tail_spec,
        ],
        out_shape=[
            jax.ShapeDtypeStruct((t, QK_WIDTH), BF16),
            jax.ShapeDtypeStruct((t, QK_WIDTH), F32),
            kb_shape,
            jax.ShapeDtypeStruct((t, ATT_WIDTH), F32),
            jax.ShapeDtypeStruct((t, ATT_WIDTH), BF16),
            jax.ShapeDtypeStruct((t, CONV_CH), BF16),
            tail_shape,
        ],
        scratch_shapes=[
            pltpu.VMEM((tm, D_MODEL), BF16),
            pltpu.VMEM((tm, CONV_CH), F32),
            pltpu.VMEM((tm, CONV_CH), F32),
            pltpu.VMEM((tm, CONV_CH), F32),
            pltpu.VMEM((seq + 8, CONV_CH), F32),
        ],
        compiler_params=_params(("arbitrary", "arbitrary"), 56),
        name="inproj_carry" if carry else "inproj_seqs",
    )(x2d, norm_g, w_in_bf, qg, kg, gmat, cos, sin, conv_prev, conv_w, conv_norm_g)


def _lambda_full(lamv_ref, lambda_init):
    lv = lamv_ref[...]
    a = jnp.sum(lv[0:1] * lv[1:2], axis=-1, keepdims=True)
    b = jnp.sum(lv[2:3] * lv[3:4], axis=-1, keepdims=True)
    return jnp.exp(a) - jnp.exp(b) + lambda_init


def _split_heads(q):
    lane = lax.broadcasted_iota(jnp.int32, q.shape, 1)
    zero = jnp.zeros_like(q)
    return jnp.where(lane < HEAD_DIM, q, zero), jnp.where(lane >= HEAD_DIM, q, zero)


def _diff_finish(acc, l, lam, g, lambda_init, rows):
    a = acc / l
    d = a[0:rows] - lam * a[rows:2 * rows]
    inv = lax.rsqrt(jnp.mean(d * d, axis=-1, keepdims=True) + NORM_EPS)
    return d * inv * g * (1.0 - lambda_init)


def _attn_prompt_kernel(q_ref, kt_ref, v_ref, lamv_ref, g_ref, o_ref, qq_s, m_s, l_s, acc_s,
                        *, lambda_init):
    i = pl.program_id(1)
    t = ATT_T
    q1, q2 = _split_heads(q_ref[...])
    qq_s[0:t, :] = q1
    qq_s[t:2 * t, :] = q2
    m_s[...] = jnp.full(m_s.shape, MASKED, F32)
    l_s[...] = jnp.zeros(l_s.shape, F32)
    acc_s[...] = jnp.zeros(acc_s.shape, F32)

    def step(kt, diagonal):
        s = jnp.dot(qq_s[...], kt_ref[kt], preferred_element_type=F32)
        if diagonal:
            r = lax.broadcasted_iota(jnp.int32, s.shape, 0)
            c = lax.broadcasted_iota(jnp.int32, s.shape, 1)
            s = jnp.where((c // CHUNK) <= ((r % t) // CHUNK), s, MASKED)
        m_prev = m_s[...]
        m_new = jnp.maximum(m_prev, jnp.max(s, axis=-1, keepdims=True))
        alpha = jnp.exp(m_prev - m_new)
        p = jnp.exp(s - m_new)
        l_s[...] = alpha * l_s[...] + jnp.sum(p, axis=-1, keepdims=True)
        v = v_ref[pl.ds(pl.multiple_of(kt * t, t), t), :]
        acc_s[...] = alpha * acc_s[...] + jnp.dot(p.astype(BF16), v, preferred_element_type=F32)
        m_s[...] = m_new

    def body(kt, carry):
        step(kt, False)
        return carry

    lax.fori_loop(0, i, body, 0)
    step(i, True)
    lam = _lambda_full(lamv_ref, lambda_init)
    o_ref[...] = _diff_finish(acc_s[...], l_s[...], lam, g_ref[...], lambda_init, t).astype(BF16)


def _attn_prompt(q_bf, kt_bf, v_bf, lamv, subln_g, lambda_init):
    t = q_bf.shape[0]
    nq = t // ATT_T
    kern = functools.partial(_attn_prompt_kernel, lambda_init=lambda_init)
    return pl.pallas_call(
        kern,
        grid=(N_HEADS, nq),
        in_specs=[
            pl.BlockSpec((ATT_T, V_DIM), lambda h, i: (i, h)),
            pl.BlockSpec((nq, V_DIM, ATT_T), lambda h, i: (0, h, 0)),
            pl.BlockSpec((t, V_DIM), lambda h, i: (0, h)),
            pl.BlockSpec((4, HEAD_DIM), lambda h, i: (0, 0)),
            pl.BlockSpec((1, V_DIM), lambda h, i: (0, 0)),
        ],
        out_specs=pl.BlockSpec((ATT_T, V_DIM), lambda h, i: (i, h)),
        out_shape=jax.ShapeDtypeStruct((t, ATT_WIDTH), BF16),
        scratch_shapes=[
            pltpu.VMEM((2 * ATT_T, V_DIM), BF16),
            pltpu.VMEM((2 * ATT_T, 1), F32),
            pltpu.VMEM((2 * ATT_T, 1), F32),
            pltpu.VMEM((2 * ATT_T, V_DIM), F32),
        ],
        compiler_params=_params(("arbitrary", "arbitrary"), 48),
        name="attn_prompt",
    )(q_bf, kt_bf, v_bf, lamv, subln_g)


def _attn_sample_kernel(q_ref, kc_ref, vc_ref, kn_ref, vn_ref, lamv_ref, g_ref, o_ref,
                        *, lambda_init, past):
    rows = q_ref.shape[0]
    q1, q2 = _split_heads(q_ref[...])
    qq = jnp.concatenate([q1, q2], axis=0)
    contract_last = (((1,), (1,)), ((), ()))
    s_c = lax.dot_general(qq, kc_ref[0].astype(BF16), contract_last, preferred_element_type=F32)
    s_n = lax.dot_general(qq, kn_ref[...], contract_last, preferred_element_type=F32)
    r = lax.broadcasted_iota(jnp.int32, s_n.shape, 0)
    c = lax.broadcasted_iota(jnp.int32, s_n.shape, 1)
    s_n = jnp.where(((past + c) // CHUNK) <= ((past + r % rows) // CHUNK), s_n, MASKED)
    m = jnp.maximum(jnp.max(s_c, axis=-1, keepdims=True), jnp.max(s_n, axis=-1, keepdims=True))
    p_c = jnp.exp(s_c - m)
    p_n = jnp.exp(s_n - m)
    l = jnp.sum(p_c, axis=-1, keepdims=True) + jnp.sum(p_n, axis=-1, keepdims=True)
    acc = (jnp.dot(p_c.astype(BF16), vc_ref[0].astype(BF16), preferred_element_type=F32)
           + jnp.dot(p_n.astype(BF16), vn_ref[...], preferred_element_type=F32))
    lam = _lambda_full(lamv_ref, lambda_init)
    o_ref[...] = _diff_finish(acc, l, lam, g_ref[...], lambda_init, rows).astype(BF16)


def _attn_sample(q_bf, k_cache, v_cache, k_bf, v_bf, lamv, subln_g, lambda_init):
    nb, past, _ = k_cache.shape
    rows = q_bf.shape[0] // nb
    kern = functools.partial(_attn_sample_kernel, lambda_init=lambda_init, past=past)
    new_spec = pl.BlockSpec((rows, V_DIM), lambda b, h: (b, h))
    cache_spec = pl.BlockSpec((1, past, V_DIM), lambda b, h: (b, 0, h))
    return pl.pallas_call(
        kern,
        grid=(nb, N_HEADS),
        in_specs=[
            new_spec, cache_spec, cache_spec, new_spec, new_spec,
            pl.BlockSpec((4, HEAD_DIM), lambda b, h: (0, 0)),
            pl.BlockSpec((1, V_DIM), lambda b, h: (0, 0)),
        ],
        out_specs=new_spec,
        out_shape=jax.ShapeDtypeStruct(q_bf.shape, BF16),
        compiler_params=_params(("arbitrary", "arbitrary"), 32),
        name="attn_sample",
    )(q_bf, k_cache, v_cache, k_bf, v_bf, lamv, subln_g)


def _outproj_kernel(att_p_ref, yc_p_ref, x_p_ref, att_s_ref, yc_s_ref, x_s_ref,
                    wt_ref, wb_ref, ng_ref, rw_ref, rb_ref, x1_ref, xf_ref, route_ref, *, n_prompt_tiles):
    i = pl.program_id(0)
    shared = (wt_ref, wb_ref, ng_ref, rw_ref, rb_ref, x1_ref, xf_ref, route_ref)

    @pl.when(i < n_prompt_tiles)
    def _():
        _outproj_tile(att_p_ref, yc_p_ref, x_p_ref, *shared)

    @pl.when(i >= n_prompt_tiles)
    def _():
        _outproj_tile(att_s_ref, yc_s_ref, x_s_ref, *shared)


def _outproj_tile(att_ref, yc_ref, x_ref, wt_ref, wb_ref, ng_ref, rw_ref, rb_ref,
                  x1_ref, xf_ref, route_ref):
    o = (jnp.dot(att_ref[...], wt_ref[...], preferred_element_type=F32)
         + jnp.dot(yc_ref[...], wb_ref[...], preferred_element_type=F32))
    x1 = x_ref[...] + o
    x1_ref[...] = x1
    xf = x1 * lax.rsqrt(jnp.mean(x1 * x1, axis=-1, keepdims=True) + NORM_EPS) * ng_ref[...]
    xf_ref[...] = xf
    logits = jnp.dot(xf.astype(BF16), rw_ref[...], preferred_element_type=F32) + rb_ref[...]

    lane = lax.broadcasted_iota(jnp.int32, logits.shape, 1)
    neg = -jnp.inf
    is_group = lane < N_GROUPS
    gl = jnp.where(is_group, logits, neg)
    gmax = jnp.max(gl, axis=-1, keepdims=True)
    grp = jnp.min(jnp.where(gl == gmax, lane, ROUTE_LANES), axis=-1, keepdims=True)
    gsum = jnp.sum(jnp.where(is_group, jnp.exp(gl - gmax), 0.0), axis=-1, keepdims=True)
    g_w = 1.0 / gsum
    e_lane = lane - N_GROUPS
    in_grp = (e_lane >= 0) & (e_lane < N_EXPERTS) & ((e_lane // EXPERTS_PER_GROUP) == grp)
    el = jnp.where(in_grp, logits, neg)
    t1 = jnp.max(el, axis=-1, keepdims=True)
    i1 = jnp.min(jnp.where(el == t1, lane, ROUTE_LANES), axis=-1, keepdims=True)
    el2 = jnp.where(lane == i1, neg, el)
    t2 = jnp.max(el2, axis=-1, keepdims=True)
    i2 = jnp.min(jnp.where(el2 == t2, lane, ROUTE_LANES), axis=-1, keepdims=True)
    r21 = jnp.exp(t2 - t1)
    w0 = g_w / (1.0 + r21)
    w1 = g_w * r21 / (1.0 + r21)
    e0 = (i1 - N_GROUPS).astype(F32)
    e1 = (i2 - N_GROUPS).astype(F32)
    route_ref[...] = jnp.where(lane == 0, e0, jnp.where(lane == 1, e1,
                               jnp.where(lane == 2, w0, jnp.where(lane == 3, w1, 0.0))))


def _outproj(att_p, yc_p, x_p, att_s, yc_s, x_s, w_top, w_bot, norm_g, rw, rb):
    tm = OUT_TM
    n_p = x_p.shape[0] // tm
    n_s = x_s.shape[0] // tm
    t_all = x_p.shape[0] + x_s.shape[0]
    prow = lambda i: (jnp.minimum(i, n_p - 1), 0)
    srow = lambda i: (jnp.maximum(i - n_p, 0), 0)
    row = lambda i: (i, 0)
    const = lambda i: (0, 0)
    kern = functools.partial(_outproj_kernel, n_prompt_tiles=n_p)
    return pl.pallas_call(
        kern,
        grid=(n_p + n_s,),
        in_specs=[
            pl.BlockSpec((tm, ATT_WIDTH), prow),
            pl.BlockSpec((tm, CONV_CH), prow),
            pl.BlockSpec((tm, D_MODEL), prow),
            pl.BlockSpec((tm, ATT_WIDTH), srow),
            pl.BlockSpec((tm, CONV_CH), srow),
            pl.BlockSpec((tm, D_MODEL), srow),
            pl.BlockSpec((ATT_WIDTH, D_MODEL), const),
            pl.BlockSpec((CONV_CH, D_MODEL), const),
            pl.BlockSpec((1, D_MODEL), const),
            pl.BlockSpec((D_MODEL, ROUTE_LANES), const),
            pl.BlockSpec((1, ROUTE_LANES), const),
        ],
        out_specs=[
            pl.BlockSpec((tm, D_MODEL), row),
            pl.BlockSpec((tm, D_MODEL), row),
            pl.BlockSpec((tm, ROUTE_LANES), row),
        ],
        out_shape=[
            jax.ShapeDtypeStruct((t_all, D_MODEL), F32),
            jax.ShapeDtypeStruct((t_all, D_MODEL), F32),
            jax.ShapeDtypeStruct((t_all, ROUTE_LANES), F32),
        ],
        compiler_params=_params(("arbitrary",), 52),
        name="outproj_router",
    )(att_p, yc_p, x_p, att_s, yc_s, x_s, w_top, w_bot, norm_g, rw, rb)


def _dispatch_copy(xf_hbm, xs_hbm, sem, src_row, dst_row):
    return pltpu.make_async_copy(xf_hbm.at[pl.ds(src_row, 1)], xs_hbm.at[pl.ds(dst_row, 1)], sem)


def _dispatch_kernel(dest_ref, xf_hbm, xs_init_hbm, xs_hbm, sem):
    del xs_init_hbm
    i = pl.program_id(0)
    tm = DISPATCH_TM

    def issue(t, carry):
        for k in range(TOP_K):
            _dispatch_copy(xf_hbm, xs_hbm, sem, i * tm + t, dest_ref[0, 0, TOP_K * t + k]).start()
        return carry

    def drain(t, carry):
        for k in range(TOP_K):
            _dispatch_copy(xf_hbm, xs_hbm, sem, i * tm + t, dest_ref[0, 0, TOP_K * t + k]).wait()
        return carry

    lax.fori_loop(0, tm, issue, 0)
    lax.fori_loop(0, tm, drain, 0)


def _dispatch(dest, xf, cap):
    t_all = xf.shape[0]
    tm = DISPATCH_TM
    dest3 = dest.reshape(t_all // tm, 1, TOP_K * tm)
    xs_init = jnp.zeros((cap, D_MODEL), F32)
    return pl.pallas_call(
        _dispatch_kernel,
        grid=(t_all // tm,),
        in_specs=[
            pl.BlockSpec((1, 1, TOP_K * tm), lambda i: (i, 0, 0), memory_space=pltpu.SMEM),
            pl.BlockSpec(memory_space=pl.ANY),
            pl.BlockSpec(memory_space=pl.ANY),
        ],
        out_specs=pl.BlockSpec(memory_space=pl.ANY),
        out_shape=jax.ShapeDtypeStruct((cap, D_MODEL), F32),
        scratch_shapes=[pltpu.SemaphoreType.DMA(())],
        input_output_aliases={2: 0},
        compiler_params=_params(("arbitrary",), 16),
        name="moe_dispatch",
    )(dest3, xf, xs_init)


def _experts_kernel(blk_e_ref, first_ref, nused_ref, x_ref, wg_ref, wu_ref, wd_ref, o_ref,
                    wgu_s, wd_s):
    del blk_e_ref
    i = pl.program_id(0)

    @pl.when(i < nused_ref[0])
    def _():
        @pl.when(first_ref[i] == 1)
        def _():
            wgu_s[:, 0:D_EXPERT] = wg_ref[0].astype(BF16)
            wgu_s[:, D_EXPERT:2 * D_EXPERT] = wu_ref[0].astype(BF16)
            wd_s[...] = wd_ref[0].astype(BF16)

        gu = jnp.dot(x_ref[...].astype(BF16), wgu_s[...], preferred_element_type=F32)
        g = gu[:, 0:D_EXPERT]
        u = gu[:, D_EXPERT:2 * D_EXPERT]
        h = g / (1.0 + jnp.exp(-g)) * u
        o_ref[...] = jnp.dot(h.astype(BF16), wd_s[...], preferred_element_type=F32)

    @pl.when(i >= nused_ref[0])
    def _():
        o_ref[...] = jnp.zeros(o_ref.shape, F32)


def _experts(blk_e, first, nused, xs, w_gate, w_up, w_down):
    cap = xs.shape[0]
    bm = MOE_BM
    rows = lambda i, be, fi, nu: (jnp.minimum(i, nu[0] - 1), 0)
    wsel = lambda i, be, fi, nu: (be[i], 0, 0)
    grid_spec = pltpu.PrefetchScalarGridSpec(
        num_scalar_prefetch=3,
        grid=(cap // bm,),
        in_specs=[
            pl.BlockSpec((bm, D_MODEL), rows),
            pl.BlockSpec((1, D_MODEL, D_EXPERT), wsel),
            pl.BlockSpec((1, D_MODEL, D_EXPERT), wsel),
            pl.BlockSpec((1, D_EXPERT, D_MODEL), wsel),
        ],
        out_specs=pl.BlockSpec((bm, D_MODEL), lambda i, be, fi, nu: (i, 0)),
        scratch_shapes=[
            pltpu.VMEM((D_MODEL, 2 * D_EXPERT), BF16),
            pltpu.VMEM((D_EXPERT, D_MODEL), BF16),
        ],
    )
    return pl.pallas_call(
        _experts_kernel,
        grid_spec=grid_spec,
        out_shape=jax.ShapeDtypeStruct((cap, D_MODEL), F32),
        compiler_params=_params(("arbitrary",), 52),
        name="moe_experts",
    )(blk_e, first, nused, xs, w_gate, w_up, w_down)


def _combine_copy(outs_hbm, g_s, sem, src_row, k, t):
    return pltpu.make_async_copy(outs_hbm.at[pl.ds(src_row, 1)], g_s.at[k, pl.ds(t, 1)], sem)


def _combine_kernel(dest_ref, x1_ref, route_ref, outs_hbm, y_ref, g_s, sem):
    tm = x1_ref.shape[0]

    def issue(t, carry):
        for k in range(TOP_K):
            _combine_copy(outs_hbm, g_s, sem, dest_ref[0, 0, TOP_K * t + k], k, t).start()
        return carry

    def drain(t, carry):
        for k in range(TOP_K):
            _combine_copy(outs_hbm, g_s, sem, dest_ref[0, 0, TOP_K * t + k], k, t).wait()
        return carry

    lax.fori_loop(0, tm, issue, 0)
    lax.fori_loop(0, tm, drain, 0)
    r = route_ref[...]
    y_ref[...] = x1_ref[...] + r[:, 2:3] * g_s[0] + r[:, 3:4] * g_s[1]


def _combine(dest, x1, route, outs, row_off, t):
    tm = OUT_TM
    t_all = x1.shape[0]
    off = row_off // tm
    dest3 = dest.reshape(t_all // tm, 1, TOP_K * tm)
    return pl.pallas_call(
        _combine_kernel,
        grid=(t // tm,),
        in_specs=[
            pl.BlockSpec((1, 1, TOP_K * tm), lambda i: (i + off, 0, 0), memory_space=pltpu.SMEM),
            pl.BlockSpec((tm, D_MODEL), lambda i: (i + off, 0)),
            pl.BlockSpec((tm, ROUTE_LANES), lambda i: (i + off, 0)),
            pl.BlockSpec(memory_space=pl.ANY),
        ],
        out_specs=pl.BlockSpec((tm, D_MODEL), lambda i: (i, 0)),
        out_shape=jax.ShapeDtypeStruct((t, D_MODEL), F32),
        scratch_shapes=[pltpu.VMEM((TOP_K, tm, D_MODEL), F32), pltpu.SemaphoreType.DMA(())],
        compiler_params=_params(("arbitrary",), 32),
        name="moe_combine",
    )(dest3, x1, route, outs)


def _routing_tables(route, n_blocks):
    bm = MOE_BM
    flat_e = route[:, 0:TOP_K].astype(jnp.int32).reshape(-1)
    onehot = (flat_e[:, None] == jnp.arange(N_EXPERTS, dtype=jnp.int32)[None, :]).astype(jnp.int32)
    csum = jnp.cumsum(onehot, axis=0)
    rank = jnp.sum(csum * onehot, axis=1) - 1
    counts = csum[-1]
    padded = (counts + bm - 1) // bm * bm
    pad_end = jnp.cumsum(padded)
    pad_start = pad_end - padded
    dest = (pad_start[flat_e] + rank).astype(jnp.int32)
    nused = (pad_end[-1] // bm).astype(jnp.int32)
    blk = jnp.arange(n_blocks, dtype=jnp.int32)
    blk_e = jnp.searchsorted(pad_end, jnp.minimum(blk, nused - 1) * bm, side='right').astype(jnp.int32)
    blk_e = jnp.minimum(blk_e, N_EXPERTS - 1)
    first = jnp.concatenate([jnp.ones((1,), jnp.int32),
                             (blk_e[1:] != blk_e[:-1]).astype(jnp.int32)])
    return dest, blk_e, first, nused.reshape(1)


def _rope_tables(pos):
    half = HEAD_DIM // 2
    inv = ROPE_THETA ** (-jnp.arange(half, dtype=F32) / half)
    ang = pos.astype(F32)[:, None] * inv[None, :]
    cos = jnp.cos(ang)
    sin = jnp.sin(ang)
    cos_h = jnp.concatenate([cos, cos], axis=-1)
    sin_h = jnp.concatenate([-sin, sin], axis=-1)
    reps = LANES // HEAD_DIM
    return jnp.tile(cos_h, (1, reps)), jnp.tile(sin_h, (1, reps))


def kernel(x_prompt, x_sample, cache_k, cache_v, state_conv, norm_mix_g, w_in, q_norm_g, k_norm_g, lambda_q1, lambda_k1, lambda_q2, lambda_k2, subln_g, conv_w, conv_norm_g, w_out, norm_ffn_g, router_group_w, router_group_b, router_expert_w, router_expert_b, expert_w_gate, expert_w_up, expert_w_down):
    assert w_in.shape[0] == 1, "single-layer step"
    bp, sp, _ = x_prompt.shape
    bs, ss, _ = x_sample.shape
    past = cache_k.shape[2]
    assert bp == 1 and sp % ATT_T == 0
    tp = bp * sp
    ts = bs * ss
    t_all = tp + ts
    lambda_init = 0.8 - 0.6 * math.exp(-0.3 * 0)

    w_in_bf = w_in[0].astype(BF16)
    w_out_bf = w_out[0].astype(BF16)
    w_top, w_bot = w_out_bf[:ATT_WIDTH], w_out_bf[ATT_WIDTH:]
    ng = norm_mix_g[0].reshape(1, D_MODEL)
    qg = jnp.tile(q_norm_g[0], QK_WIDTH // HEAD_DIM).reshape(1, QK_WIDTH)
    kg = jnp.tile(k_norm_g[0], QK_WIDTH // HEAD_DIM).reshape(1, QK_WIDTH)
    head_of = jnp.arange(IN_TN, dtype=jnp.int32) // HEAD_DIM
    gmat = jnp.where(head_of[:, None] == head_of[None, :], 1.0 / HEAD_DIM, 0.0).astype(BF16)
    lamv = jnp.stack([lambda_q1[0], lambda_k1[0], lambda_q2[0], lambda_k2[0]]).astype(F32)
    sg = subln_g[0].reshape(1, V_DIM)
    cw = conv_w[0]
    cng = conv_norm_g[0].reshape(1, CONV_CH)
    cos_p, sin_p = _rope_tables(jnp.arange(sp, dtype=jnp.int32))
    cos_s, sin_s = _rope_tables(jnp.tile(past + jnp.arange(ss, dtype=jnp.int32), bs))

    zero_conv = jnp.zeros((1, CONV_K - 1, CONV_CH), F32)
    q_p, kf_p, kt_p, vf_p, vb_p, yc_p, tail_p = _inproj(
        x_prompt.reshape(tp, D_MODEL), ng, w_in_bf, qg, kg, gmat, cos_p, sin_p, zero_conv, cw, cng,
        tm=ATT_T, nseq=1, carry=True, k_transposed=True)
    att_p = _attn_prompt(q_p, kt_p, vb_p, lamv, sg, lambda_init)

    seqs_per_tile = OUT_TM // ss
    q_s, kf_s, kb_s, vf_s, vb_s, yc_s, tail_s = _inproj(
        x_sample.reshape(ts, D_MODEL), ng, w_in_bf, qg, kg, gmat, cos_s, sin_s, state_conv[0], cw, cng,
        tm=OUT_TM, nseq=seqs_per_tile, carry=False, k_transposed=False)
    att_s = _attn_sample(q_s, cache_k[0].reshape(bs, past, QK_WIDTH), cache_v[0].reshape(bs, past, ATT_WIDTH),
                         kb_s, vb_s, lamv, sg, lambda_init)

    rw = jnp.zeros((D_MODEL, ROUTE_LANES), F32)
    rw = rw.at[:, 0:N_GROUPS].set(router_group_w[0]).at[:, N_GROUPS:N_GROUPS + N_EXPERTS].set(router_expert_w[0])
    rb = jnp.zeros((1, ROUTE_LANES), F32)
    rb = rb.at[0, 0:N_GROUPS].set(router_group_b[0]).at[0, N_GROUPS:N_GROUPS + N_EXPERTS].set(router_expert_b[0])
    nf = norm_ffn_g[0].reshape(1, D_MODEL)
    rw_bf = rw.astype(BF16)
    x1, xf, route = _outproj(att_p, yc_p, x_prompt.reshape(tp, D_MODEL),
                             att_s, yc_s, x_sample.reshape(ts, D_MODEL), w_top, w_bot, nf, rw_bf, rb)

    n = t_all * TOP_K
    n_blocks = n // MOE_BM + N_EXPERTS
    dest, blk_e, first, nused = _routing_tables(route, n_blocks)
    xs = _dispatch(dest, xf, n_blocks * MOE_BM)
    outs = _experts(blk_e, first, nused, xs, expert_w_gate[0], expert_w_up[0], expert_w_down[0])
    y_p = _combine(dest, x1, route, outs, 0, tp)
    y_s = _combine(dest, x1, route, outs, tp, ts)

    return (y_p.reshape(bp, sp, D_MODEL),
            y_s.reshape(bs, ss, D_MODEL),
            kf_p.reshape(1, bp, sp, N_HEADS, 2, HEAD_DIM),
            vf_p.reshape(1, bp, sp, N_HEADS, V_DIM),
            tail_p.reshape(1, bp, CONV_K - 1, CONV_CH),
            kf_s.reshape(1, bs, ss, N_HEADS, 2, HEAD_DIM),
            vf_s.reshape(1, bs, ss, N_HEADS, V_DIM),
            tail_s.reshape(1, bs, CONV_K - 1, CONV_CH))
```

```python
import functools
import math

import jax
import jax.numpy as jnp
from jax import lax
from jax.experimental import pallas as pl
from jax.experimental.pallas import tpu as pltpu

F32 = jnp.float32
BF16 = jnp.bfloat16

D_MODEL = 2048
CHUNK = 64
HEAD_DIM = 64
V_DIM = 2 * HEAD_DIM
N_HEADS = 8
QK_WIDTH = N_HEADS * 2 * HEAD_DIM
ATT_WIDTH = N_HEADS * V_DIM
CONV_CH = 1024
CONV_K = 3
ROPE_THETA = 10000.0
N_GROUPS = 8
EXPERTS_PER_GROUP = 8
N_EXPERTS = N_GROUPS * EXPERTS_PER_GROUP
TOP_K = 2
D_EXPERT = D_MODEL // 4
NORM_EPS = 1e-6
SECTION = 1024
N_SECTIONS = 6

LANES = 128
IN_TN = 512
IN_NJ = N_SECTIONS * SECTION // IN_TN
ATT_CB = 256
ATT_T = 512
OUT_TM = 256
MOE_BM = 256
DISPATCH_TM = 512
ROUTE_LANES = LANES
MASKED = -1e30
Q_SCALE = HEAD_DIM ** -0.5 * math.log2(math.e)
MIB = 1024 * 1024


def _params(sem, vmem_mib):
    return pltpu.CompilerParams(dimension_semantics=sem, vmem_limit_bytes=vmem_mib * MIB)


def _inproj_kernel(x_ref, ng_ref, w_ref, qg_ref, kg_ref, gmat_ref, cos_ref, sin_ref,
                   prev_ref, cw_ref, cng_ref,
                   q_ref, kf_ref, kb_ref, vf_ref, vb_ref, yc_ref, tail_ref,
                   xn_s, gb_s, c_s, z_s, u_s, *, nseq, carry, qv_transposed):
    i = pl.program_id(0)
    j = pl.program_id(1)
    tm = x_ref.shape[0]
    seq = tm // nseq

    @pl.when(j == 0)
    def _():
        x = x_ref[...]
        inv = lax.rsqrt(jnp.mean(x * x, axis=-1, keepdims=True) + NORM_EPS)
        xn_s[...] = (x * inv * ng_ref[...]).astype(BF16)

    acc = jnp.dot(xn_s[...], w_ref[...], preferred_element_type=F32)

    def head_norm_rope(a, g):
        ms = jnp.dot((a * a).astype(BF16), gmat_ref[...], preferred_element_type=F32)
        y = a * lax.rsqrt(ms + NORM_EPS) * g
        reps = IN_TN // LANES
        cos = jnp.concatenate([cos_ref[...]] * reps, axis=1)
        sin = jnp.concatenate([sin_ref[...]] * reps, axis=1)
        lane = lax.broadcasted_iota(jnp.int32, y.shape, 1)
        first = (lane & (HEAD_DIM - 1)) < HEAD_DIM // 2
        partner = jnp.where(first, pltpu.roll(y, IN_TN - HEAD_DIM // 2, 1),
                            pltpu.roll(y, HEAD_DIM // 2, 1))
        return y * cos + partner * sin

    for jj in range(IN_NJ):
        section, half = divmod(jj, SECTION // IN_TN)
        cols = slice(half * IN_TN, (half + 1) * IN_TN)

        @pl.when(j == jj)
        def _(section=section, half=half, cols=cols):
            if section == 0:
                q = head_norm_rope(acc, qg_ref[:, cols]) * Q_SCALE
                if qv_transposed:
                    q_ref[0, cols, :] = q.T.astype(BF16)
                else:
                    q_ref[:, cols] = q.astype(BF16)
            elif section == 1:
                k = head_norm_rope(acc, kg_ref[:, cols])
                kf_ref[:, cols] = k
                kb_ref[:, cols] = k.astype(BF16)
            elif section == 2:
                vf_ref[:, cols] = acc
                if qv_transposed:
                    vb_ref[0, cols, :] = acc.T.astype(BF16)
                else:
                    vb_ref[:, cols] = acc.astype(BF16)
            elif section == 3:
                gb_s[:, cols] = acc
            elif section == 4:
                c_s[:, cols] = acc
            else:
                u = c_s[:, cols] * acc
                w0 = cw_ref[0:1, cols]
                w1 = cw_ref[1:2, cols]
                w2 = cw_ref[2:3, cols]
                for s in range(nseq):
                    rows = slice(s * seq, (s + 1) * seq)
                    if carry:
                        @pl.when(i == 0)
                        def _():
                            u_s[6:8, cols] = prev_ref[0, :, cols]
                    else:
                        u_s[6:8, cols] = prev_ref[s, :, cols]
                    u_s[8:8 + seq, cols] = u[rows]
                    conv = (w0 * u_s[6:6 + seq, cols] + w1 * u_s[7:7 + seq, cols]
                            + w2 * u_s[8:8 + seq, cols])
                    z_s[rows, cols] = gb_s[rows, cols] * conv
                    tail = u_s[seq + 6:seq + 8, cols]
                    tail_ref[s, :, cols] = tail
                    if carry:
                        u_s[6:8, cols] = tail
                if half == SECTION // IN_TN - 1:
                    z = z_s[...]
                    inv = lax.rsqrt(jnp.mean(z * z, axis=-1, keepdims=True) + NORM_EPS)
                    yc_ref[...] = (z * inv * cng_ref[...]).astype(BF16)


def _inproj(x2d, norm_g, w_in_bf, qg, kg, gmat, cos, sin, conv_prev, conv_w, conv_norm_g,
            *, tm, nseq, carry, qv_transposed):
    t = x2d.shape[0]
    ni = t // tm
    seq = tm // nseq
    row = lambda i, j: (i, 0)
    const = lambda i, j: (0, 0)
    if qv_transposed:
        qv_shape = jax.ShapeDtypeStruct((ni, QK_WIDTH, tm), BF16)
        qv_spec = pl.BlockSpec((1, QK_WIDTH, tm), lambda i, j: (i, 0, 0))
    else:
        qv_shape = jax.ShapeDtypeStruct((t, QK_WIDTH), BF16)
        qv_spec = pl.BlockSpec((tm, QK_WIDTH), row)
    if carry:
        prev_spec = pl.BlockSpec((1, CONV_K - 1, CONV_CH), lambda i, j: (0, 0, 0))
        tail_shape = jax.ShapeDtypeStruct((1, CONV_K - 1, CONV_CH), F32)
        tail_spec = pl.BlockSpec((1, CONV_K - 1, CONV_CH), lambda i, j: (0, 0, 0))
    else:
        prev_spec = pl.BlockSpec((nseq, CONV_K - 1, CONV_CH), lambda i, j: (i, 0, 0))
        tail_shape = jax.ShapeDtypeStruct((ni * nseq, CONV_K - 1, CONV_CH), F32)
        tail_spec = pl.BlockSpec((nseq, CONV_K - 1, CONV_CH), lambda i, j: (i, 0, 0))
    kern = functools.partial(_inproj_kernel, nseq=nseq, carry=carry, qv_transposed=qv_transposed)
    return pl.pallas_call(
        kern,
        grid=(ni, IN_NJ),
        in_specs=[
            pl.BlockSpec((tm, D_MODEL), row),
            pl.BlockSpec((1, D_MODEL), const),
            pl.BlockSpec((D_MODEL, IN_TN), lambda i, j: (0, j)),
            pl.BlockSpec((1, SECTION), const),
            pl.BlockSpec((1, SECTION), const),
            pl.BlockSpec((IN_TN, IN_TN), const),
            pl.BlockSpec((tm, LANES), row),
            pl.BlockSpec((tm, LANES), row),
            prev_spec,
            pl.BlockSpec((CONV_K, CONV_CH), const),
            pl.BlockSpec((1, CONV_CH), const),
        ],
        out_specs=[
            qv_spec,
            pl.BlockSpec((tm, QK_WIDTH), row),
            pl.BlockSpec((tm, QK_WIDTH), row),
            pl.BlockSpec((tm, ATT_WIDTH), row),
            qv_spec,
            pl.BlockSpec((tm, CONV_CH), row),
            tail_spec,
        ],
        out_shape=[
            qv_shape,
            jax.ShapeDtypeStruct((t, QK_WIDTH), F32),
            jax.ShapeDtypeStruct((t, QK_WIDTH), BF16),
            jax.ShapeDtypeStruct((t, ATT_WIDTH), F32),
            qv_shape,
            jax.ShapeDtypeStruct((t, CONV_CH), BF16),
            tail_shape,
        ],
        scratch_shapes=[
            pltpu.VMEM((tm, D_MODEL), BF16),
            pltpu.VMEM((tm, CONV_CH), F32),
            pltpu.VMEM((tm, CONV_CH), F32),
            pltpu.VMEM((tm, CONV_CH), F32),
            pltpu.VMEM((seq + 8, CONV_CH), F32),
        ],
        compiler_params=_params(("arbitrary", "arbitrary"), 56),
        name="inproj_carry" if carry else "inproj_seqs",
    )(x2d, norm_g, w_in_bf, qg, kg, gmat, cos, sin, conv_prev, conv_w, conv_norm_g)


def _lambda_full(lamv_ref, lambda_init):
    lv = lamv_ref[...]
    a = jnp.sum(lv[0:1] * lv[1:2], axis=-1, keepdims=True)
    b = jnp.sum(lv[2:3] * lv[3:4], axis=-1, keepdims=True)
    return jnp.exp(a) - jnp.exp(b) + lambda_init


def _split_heads(q):
    lane = lax.broadcasted_iota(jnp.int32, q.shape, 1)
    zero = jnp.zeros_like(q)
    return jnp.where(lane < HEAD_DIM, q, zero), jnp.where(lane >= HEAD_DIM, q, zero)


def _diff_finish(acc, l, lam, g, lambda_init, rows):
    a = acc / l
    d = a[0:rows] - lam * a[rows:2 * rows]
    inv = lax.rsqrt(jnp.mean(d * d, axis=-1, keepdims=True) + NORM_EPS)
    return d * inv * g * (1.0 - lambda_init)


def _attn_prompt_kernel(qt_ref, k_ref, vt_ref, lamv_ref, g_ref, o_ref, qq_s, m_s, l_s, acc_s,
                        *, lambda_init):
    i = pl.program_id(1)
    t = ATT_T
    qt = qt_ref[0]
    feat = lax.broadcasted_iota(jnp.int32, qt.shape, 0)
    zero = jnp.zeros_like(qt)
    qq_s[:, 0:t] = jnp.where(feat < HEAD_DIM, qt, zero)
    qq_s[:, t:2 * t] = jnp.where(feat >= HEAD_DIM, qt, zero)
    m_s[...] = jnp.full(m_s.shape, MASKED, F32)
    l_s[...] = jnp.zeros(l_s.shape, F32)
    acc_s[...] = jnp.zeros(acc_s.shape, F32)

    def step(kt, diagonal):
        k = k_ref[pl.ds(pl.multiple_of(kt * t, t), t), :]
        vt = vt_ref[kt]
        for cb in range(2 * t // ATT_CB):
            cs = slice(cb * ATT_CB, (cb + 1) * ATT_CB)
            s = jnp.dot(k, qq_s[:, cs], preferred_element_type=F32)
            if diagonal:
                key = lax.broadcasted_iota(jnp.int32, s.shape, 0)
                qry = lax.broadcasted_iota(jnp.int32, s.shape, 1) + (cb * ATT_CB) % t
                s = jnp.where((key // CHUNK) <= (qry // CHUNK), s, MASKED)
            m_prev = m_s[:, cs]
            m_new = jnp.maximum(m_prev, jnp.max(s, axis=0, keepdims=True))
            alpha = jnp.exp2(m_prev - m_new)
            p = jnp.exp2(s - m_new)
            l_s[:, cs] = alpha * l_s[:, cs] + jnp.sum(p, axis=0, keepdims=True)
            acc_s[:, cs] = alpha * acc_s[:, cs] + jnp.dot(vt, p.astype(BF16),
                                                          preferred_element_type=F32)
            m_s[:, cs] = m_new

    def body(kt, carry):
        step(kt, False)
        return carry

    lax.fori_loop(0, i, body, 0)
    step(i, True)
    lam = _lambda_full(lamv_ref, lambda_init)
    a = acc_s[...] / l_s[...]
    d = a[:, 0:t] - lam * a[:, t:2 * t]
    inv = lax.rsqrt(jnp.mean(d * d, axis=0, keepdims=True) + NORM_EPS)
    y = d * inv * g_ref[...] * (1.0 - lambda_init)
    o_ref[...] = y.T.astype(BF16)


def _attn_prompt(qt_bf, k_bf, vt_bf, lamv, subln_g_col, lambda_init):
    nq = qt_bf.shape[0]
    t = nq * ATT_T
    kern = functools.partial(_attn_prompt_kernel, lambda_init=lambda_init)
    return pl.pallas_call(
        kern,
        grid=(N_HEADS, nq),
        in_specs=[
            pl.BlockSpec((1, V_DIM, ATT_T), lambda h, i: (i, h, 0)),
            pl.BlockSpec((t, V_DIM), lambda h, i: (0, h)),
            pl.BlockSpec((nq, V_DIM, ATT_T), lambda h, i: (0, h, 0)),
            pl.BlockSpec((4, HEAD_DIM), lambda h, i: (0, 0)),
            pl.BlockSpec((V_DIM, 1), lambda h, i: (0, 0)),
        ],
        out_specs=pl.BlockSpec((ATT_T, V_DIM), lambda h, i: (i, h)),
        out_shape=jax.ShapeDtypeStruct((t, ATT_WIDTH), BF16),
        scratch_shapes=[
            pltpu.VMEM((V_DIM, 2 * ATT_T), BF16),
            pltpu.VMEM((1, 2 * ATT_T), F32),
            pltpu.VMEM((1, 2 * ATT_T), F32),
            pltpu.VMEM((V_DIM, 2 * ATT_T), F32),
        ],
        compiler_params=_params(("arbitrary", "arbitrary"), 48),
        name="attn_prompt",
    )(qt_bf, k_bf, vt_bf, lamv, subln_g_col)


def _attn_sample_kernel(q_ref, kc_ref, vc_ref, kn_ref, vn_ref, lamv_ref, g_ref, o_ref,
                        *, lambda_init, past):
    rows = q_ref.shape[0]
    q1, q2 = _split_heads(q_ref[...])
    qq = jnp.concatenate([q1, q2], axis=0)
    contract_last = (((1,), (1,)), ((), ()))
    s_c = lax.dot_general(qq, kc_ref[0].astype(BF16), contract_last, preferred_element_type=F32)
    s_n = lax.dot_general(qq, kn_ref[...], contract_last, preferred_element_type=F32)
    r = lax.broadcasted_iota(jnp.int32, s_n.shape, 0)
    c = lax.broadcasted_iota(jnp.int32, s_n.shape, 1)
    s_n = jnp.where(((past + c) // CHUNK) <= ((past + r % rows) // CHUNK), s_n, MASKED)
    m = jnp.maximum(jnp.max(s_c, axis=-1, keepdims=True), jnp.max(s_n, axis=-1, keepdims=True))
    p_c = jnp.exp2(s_c - m)
    p_n = jnp.exp2(s_n - m)
    l = jnp.sum(p_c, axis=-1, keepdims=True) + jnp.sum(p_n, axis=-1, keepdims=True)
    acc = (jnp.dot(p_c.astype(BF16), vc_ref[0].astype(BF16), preferred_element_type=F32)
           + jnp.dot(p_n.astype(BF16), vn_ref[...], preferred_element_type=F32))
    lam = _lambda_full(lamv_ref, lambda_init)
    o_ref[...] = _diff_finish(acc, l, lam, g_ref[...], lambda_init, rows).astype(BF16)


def _attn_sample(q_bf, k_cache, v_cache, k_bf, v_bf, lamv, subln_g, lambda_init):
    nb, past, _ = k_cache.shape
    rows = q_bf.shape[0] // nb
    kern = functools.partial(_attn_sample_kernel, lambda_init=lambda_init, past=past)
    new_spec = pl.BlockSpec((rows, V_DIM), lambda b, h: (b, h))
    cache_spec = pl.BlockSpec((1, past, V_DIM), lambda b, h: (b, 0, h))
    return pl.pallas_call(
        kern,
        grid=(nb, N_HEADS),
        in_specs=[
            new_spec, cache_spec, cache_spec, new_spec, new_spec,
            pl.BlockSpec((4, HEAD_DIM), lambda b, h: (0, 0)),
            pl.BlockSpec((1, V_DIM), lambda b, h: (0, 0)),
        ],
        out_specs=new_spec,
        out_shape=jax.ShapeDtypeStruct(q_bf.shape, BF16),
        compiler_params=_params(("arbitrary", "arbitrary"), 32),
        name="attn_sample",
    )(q_bf, k_cache, v_cache, k_bf, v_bf, lamv, subln_g)


def _outproj_kernel(att_p_ref, yc_p_ref, x_p_ref, att_s_ref, yc_s_ref, x_s_ref,
                    wt_ref, wb_ref, ng_ref, rw_ref, rb_ref, x1_ref, xf_ref, route_ref, *, n_prompt_tiles):
    i = pl.program_id(0)
    shared = (wt_ref, wb_ref, ng_ref, rw_ref, rb_ref, x1_ref, xf_ref, route_ref)

    @pl.when(i < n_prompt_tiles)
    def _():
        _outproj_tile(att_p_ref, yc_p_ref, x_p_ref, *shared)

    @pl.when(i >= n_prompt_tiles)
    def _():
        _outproj_tile(att_s_ref, yc_s_ref, x_s_ref, *shared)


def _outproj_tile(att_ref, yc_ref, x_ref, wt_ref, wb_ref, ng_ref, rw_ref, rb_ref,
                  x1_ref, xf_ref, route_ref):
    o = (jnp.dot(att_ref[...], wt_ref[...], preferred_element_type=F32)
         + jnp.dot(yc_ref[...], wb_ref[...], preferred_element_type=F32))
    x1 = x_ref[...] + o
    x1_ref[...] = x1
    xf = x1 * lax.rsqrt(jnp.mean(x1 * x1, axis=-1, keepdims=True) + NORM_EPS) * ng_ref[...]
    xf_ref[...] = xf
    logits = jnp.dot(xf.astype(BF16), rw_ref[...], preferred_element_type=F32) + rb_ref[...]

    lane = lax.broadcasted_iota(jnp.int32, logits.shape, 1)
    neg = -jnp.inf
    is_group = lane < N_GROUPS
    gl = jnp.where(is_group, logits, neg)
    gmax = jnp.max(gl, axis=-1, keepdims=True)
    grp = jnp.min(jnp.where(gl == gmax, lane, ROUTE_LANES), axis=-1, keepdims=True)
    gsum = jnp.sum(jnp.where(is_group, jnp.exp(gl - gmax), 0.0), axis=-1, keepdims=True)
    g_w = 1.0 / gsum
    e_lane = lane - N_GROUPS
    in_grp = (e_lane >= 0) & (e_lane < N_EXPERTS) & ((e_lane // EXPERTS_PER_GROUP) == grp)
    el = jnp.where(in_grp, logits, neg)
    t1 = jnp.max(el, axis=-1, keepdims=True)
    i1 = jnp.min(jnp.where(el == t1, lane, ROUTE_LANES), axis=-1, keepdims=True)
    el2 = jnp.where(lane == i1, neg, el)
    t2 = jnp.max(el2, axis=-1, keepdims=True)
    i2 = jnp.min(jnp.where(el2 == t2, lane, ROUTE_LANES), axis=-1, keepdims=True)
    r21 = jnp.exp(t2 - t1)
    w0 = g_w / (1.0 + r21)
    w1 = g_w * r21 / (1.0 + r21)
    e0 = (i1 - N_GROUPS).astype(F32)
    e1 = (i2 - N_GROUPS).astype(F32)
    route_ref[...] = jnp.where(lane == 0, e0, jnp.where(lane == 1, e1,
                               jnp.where(lane == 2, w0, jnp.where(lane == 3, w1, 0.0))))


def _outproj(att_p, yc_p, x_p, att_s, yc_s, x_s, w_top, w_bot, norm_g, rw, rb):
    tm = OUT_TM
    n_p = x_p.shape[0] // tm
    n_s = x_s.shape[0] // tm
    t_all = x_p.shape[0] + x_s.shape[0]
    prow = lambda i: (jnp.minimum(i, n_p - 1), 0)
    srow = lambda i: (jnp.maximum(i - n_p, 0), 0)
    row = lambda i: (i, 0)
    const = lambda i: (0, 0)
    kern = functools.partial(_outproj_kernel, n_prompt_tiles=n_p)
    return pl.pallas_call(
        kern,
        grid=(n_p + n_s,),
        in_specs=[
            pl.BlockSpec((tm, ATT_WIDTH), prow),
            pl.BlockSpec((tm, CONV_CH), prow),
            pl.BlockSpec((tm, D_MODEL), prow),
            pl.BlockSpec((tm, ATT_WIDTH), srow),
            pl.BlockSpec((tm, CONV_CH), srow),
            pl.BlockSpec((tm, D_MODEL), srow),
            pl.BlockSpec((ATT_WIDTH, D_MODEL), const),
            pl.BlockSpec((CONV_CH, D_MODEL), const),
            pl.BlockSpec((1, D_MODEL), const),
            pl.BlockSpec((D_MODEL, ROUTE_LANES), const),
            pl.BlockSpec((1, ROUTE_LANES), const),
        ],
        out_specs=[
            pl.BlockSpec((tm, D_MODEL), row),
            pl.BlockSpec((tm, D_MODEL), row),
            pl.BlockSpec((tm, ROUTE_LANES), row),
        ],
        out_shape=[
            jax.ShapeDtypeStruct((t_all, D_MODEL), F32),
            jax.ShapeDtypeStruct((t_all, D_MODEL), F32),
            jax.ShapeDtypeStruct((t_all, ROUTE_LANES), F32),
        ],
        compiler_params=_params(("arbitrary",), 52),
        name="outproj_router",
    )(att_p, yc_p, x_p, att_s, yc_s, x_s, w_top, w_bot, norm_g, rw, rb)


def _dispatch_copy(xf_ref, xs_hbm, sem, src_row, dst_row):
    return pltpu.make_async_copy(xf_ref.at[pl.ds(src_row, 1)], xs_hbm.at[pl.ds(dst_row, 1)], sem)


def _dispatch_kernel(dest_ref, xf_ref, xs_init_hbm, xs_hbm, sem):
    del xs_init_hbm
    tm = DISPATCH_TM

    def issue(t, carry):
        for k in range(TOP_K):
            _dispatch_copy(xf_ref, xs_hbm, sem, t, dest_ref[0, 0, TOP_K * t + k]).start()
        return carry

    def drain(t, carry):
        for k in range(TOP_K):
            _dispatch_copy(xf_ref, xs_hbm, sem, t, dest_ref[0, 0, TOP_K * t + k]).wait()
        return carry

    lax.fori_loop(0, tm, issue, 0)
    lax.fori_loop(0, tm, drain, 0)


def _dispatch(dest, xf, cap):
    t_all = xf.shape[0]
    tm = DISPATCH_TM
    dest3 = dest.reshape(t_all // tm, 1, TOP_K * tm)
    xs_init = jnp.zeros((cap, D_MODEL), F32)
    return pl.pallas_call(
        _dispatch_kernel,
        grid=(t_all // tm,),
        in_specs=[
            pl.BlockSpec((1, 1, TOP_K * tm), lambda i: (i, 0, 0), memory_space=pltpu.SMEM),
            pl.BlockSpec((tm, D_MODEL), lambda i: (i, 0)),
            pl.BlockSpec(memory_space=pl.ANY),
        ],
        out_specs=pl.BlockSpec(memory_space=pl.ANY),
        out_shape=jax.ShapeDtypeStruct((cap, D_MODEL), F32),
        scratch_shapes=[pltpu.SemaphoreType.DMA(())],
        input_output_aliases={2: 0},
        compiler_params=_params(("arbitrary",), 24),
        name="moe_dispatch",
    )(dest3, xf, xs_init)


def _experts_kernel(blk_e_ref, first_ref, nused_ref, x_ref, wg_ref, wu_ref, wd_ref, o_ref,
                    wgu_s, wd_s):
    del blk_e_ref
    i = pl.program_id(0)

    @pl.when(i < nused_ref[0])
    def _():
        @pl.when(first_ref[i] == 1)
        def _():
            wgu_s[:, 0:D_EXPERT] = wg_ref[0].astype(BF16)
            wgu_s[:, D_EXPERT:2 * D_EXPERT] = wu_ref[0].astype(BF16)
            wd_s[...] = wd_ref[0].astype(BF16)

        gu = jnp.dot(x_ref[...].astype(BF16), wgu_s[...], preferred_element_type=F32)
        g = gu[:, 0:D_EXPERT]
        u = gu[:, D_EXPERT:2 * D_EXPERT]
        h = g / (1.0 + jnp.exp(-g)) * u
        o_ref[...] = jnp.dot(h.astype(BF16), wd_s[...], preferred_element_type=F32)

    @pl.when(i >= nused_ref[0])
    def _():
        o_ref[...] = jnp.zeros(o_ref.shape, F32)


def _experts(blk_e, first, nused, xs, w_gate, w_up, w_down):
    cap = xs.shape[0]
    bm = MOE_BM
    rows = lambda i, be, fi, nu: (jnp.minimum(i, nu[0] - 1), 0)
    wsel = lambda i, be, fi, nu: (be[i], 0, 0)
    grid_spec = pltpu.PrefetchScalarGridSpec(
        num_scalar_prefetch=3,
        grid=(cap // bm,),
        in_specs=[
            pl.BlockSpec((bm, D_MODEL), rows),
            pl.BlockSpec((1, D_MODEL, D_EXPERT), wsel),
            pl.BlockSpec((1, D_MODEL, D_EXPERT), wsel),
            pl.BlockSpec((1, D_EXPERT, D_MODEL), wsel),
        ],
        out_specs=pl.BlockSpec((bm, D_MODEL), lambda i, be, fi, nu: (i, 0)),
        scratch_shapes=[
            pltpu.VMEM((D_MODEL, 2 * D_EXPERT), BF16),
            pltpu.VMEM((D_EXPERT, D_MODEL), BF16),
        ],
    )
    return pl.pallas_call(
        _experts_kernel,
        grid_spec=grid_spec,
        out_shape=jax.ShapeDtypeStruct((cap, D_MODEL), F32),
        compiler_params=_params(("arbitrary",), 52),
        name="moe_experts",
    )(blk_e, first, nused, xs, w_gate, w_up, w_down)


def _combine_copy(outs_hbm, g_s, sem, src_row, k, t):
    return pltpu.make_async_copy(outs_hbm.at[pl.ds(src_row, 1)], g_s.at[k, pl.ds(t, 1)], sem)


def _combine_kernel(dest_ref, x1_ref, route_ref, outs_hbm, y_ref, g_s, sem):
    tm = x1_ref.shape[0]

    def issue(t, carry):
        for k in range(TOP_K):
            _combine_copy(outs_hbm, g_s, sem, dest_ref[0, 0, TOP_K * t + k], k, t).start()
        return carry

    def drain(t, carry):
        for k in range(TOP_K):
            _combine_copy(outs_hbm, g_s, sem, dest_ref[0, 0, TOP_K * t + k], k, t).wait()
        return carry

    lax.fori_loop(0, tm, issue, 0)
    lax.fori_loop(0, tm, drain, 0)
    r = route_ref[...]
    y_ref[...] = x1_ref[...] + r[:, 2:3] * g_s[0] + r[:, 3:4] * g_s[1]


def _combine(dest, x1, route, outs, row_off, t):
    tm = OUT_TM
    t_all = x1.shape[0]
    off = row_off // tm
    dest3 = dest.reshape(t_all // tm, 1, TOP_K * tm)
    return pl.pallas_call(
        _combine_kernel,
        grid=(t // tm,),
        in_specs=[
            pl.BlockSpec((1, 1, TOP_K * tm), lambda i: (i + off, 0, 0), memory_space=pltpu.SMEM),
            pl.BlockSpec((tm, D_MODEL), lambda i: (i + off, 0)),
            pl.BlockSpec((tm, ROUTE_LANES), lambda i: (i + off, 0)),
            pl.BlockSpec(memory_space=pl.ANY),
        ],
        out_specs=pl.BlockSpec((tm, D_MODEL), lambda i: (i, 0)),
        out_shape=jax.ShapeDtypeStruct((t, D_MODEL), F32),
        scratch_shapes=[pltpu.VMEM((TOP_K, tm, D_MODEL), F32), pltpu.SemaphoreType.DMA(())],
        compiler_params=_params(("arbitrary",), 32),
        name="moe_combine",
    )(dest3, x1, route, outs)


def _routing_tables(route, n_blocks):
    bm = MOE_BM
    flat_e = route[:, 0:TOP_K].astype(jnp.int32).reshape(-1)
    onehot = (flat_e[:, None] == jnp.arange(N_EXPERTS, dtype=jnp.int32)[None, :]).astype(jnp.int32)
    csum = jnp.cumsum(onehot, axis=0)
    rank = jnp.sum(csum * onehot, axis=1) - 1
    counts = csum[-1]
    padded = (counts + bm - 1) // bm * bm
    pad_end = jnp.cumsum(padded)
    pad_start = pad_end - padded
    dest = (pad_start[flat_e] + rank).astype(jnp.int32)
    nused = (pad_end[-1] // bm).astype(jnp.int32)
    blk = jnp.arange(n_blocks, dtype=jnp.int32)
    blk_e = jnp.searchsorted(pad_end, jnp.minimum(blk, nused - 1) * bm, side='right').astype(jnp.int32)
    blk_e = jnp.minimum(blk_e, N_EXPERTS - 1)
    first = jnp.concatenate([jnp.ones((1,), jnp.int32),
                             (blk_e[1:] != blk_e[:-1]).astype(jnp.int32)])
    return dest, blk_e, first, nused.reshape(1)


def _rope_tables(pos):
    half = HEAD_DIM // 2
    inv = ROPE_THETA ** (-jnp.arange(half, dtype=F32) / half)
    ang = pos.astype(F32)[:, None] * inv[None, :]
    cos = jnp.cos(ang)
    sin = jnp.sin(ang)
    cos_h = jnp.concatenate([cos, cos], axis=-1)
    sin_h = jnp.concatenate([-sin, sin], axis=-1)
    reps = LANES // HEAD_DIM
    return jnp.tile(cos_h, (1, reps)), jnp.tile(sin_h, (1, reps))


def kernel(x_prompt, x_sample, cache_k, cache_v, state_conv, norm_mix_g, w_in, q_norm_g, k_norm_g, lambda_q1, lambda_k1, lambda_q2, lambda_k2, subln_g, conv_w, conv_norm_g, w_out, norm_ffn_g, router_group_w, router_group_b, router_expert_w, router_expert_b, expert_w_gate, expert_w_up, expert_w_down):
    assert w_in.shape[0] == 1, "single-layer step"
    bp, sp, _ = x_prompt.shape
    bs, ss, _ = x_sample.shape
    past = cache_k.shape[2]
    assert bp == 1 and sp % ATT_T == 0
    tp = bp * sp
    ts = bs * ss
    t_all = tp + ts
    lambda_init = 0.8 - 0.6 * math.exp(-0.3 * 0)

    w_in_bf = w_in[0].astype(BF16)
    w_out_bf = w_out[0].astype(BF16)
    w_top, w_bot = w_out_bf[:ATT_WIDTH], w_out_bf[ATT_WIDTH:]
    ng = norm_mix_g[0].reshape(1, D_MODEL)
    qg = jnp.tile(q_norm_g[0], QK_WIDTH // HEAD_DIM).reshape(1, QK_WIDTH)
    kg = jnp.tile(k_norm_g[0], QK_WIDTH // HEAD_DIM).reshape(1, QK_WIDTH)
    head_of = jnp.arange(IN_TN, dtype=jnp.int32) // HEAD_DIM
    gmat = jnp.where(head_of[:, None] == head_of[None, :], 1.0 / HEAD_DIM, 0.0).astype(BF16)
    lamv = jnp.stack([lambda_q1[0], lambda_k1[0], lambda_q2[0], lambda_k2[0]]).astype(F32)
    sg = subln_g[0].reshape(1, V_DIM)
    cw = conv_w[0]
    cng = conv_norm_g[0].reshape(1, CONV_CH)
    cos_p, sin_p = _rope_tables(jnp.arange(sp, dtype=jnp.int32))
    cos_s, sin_s = _rope_tables(jnp.tile(past + jnp.arange(ss, dtype=jnp.int32), bs))

    zero_conv = jnp.zeros((1, CONV_K - 1, CONV_CH), F32)
    qt_p, kf_p, kb_p, vf_p, vt_p, yc_p, tail_p = _inproj(
        x_prompt.reshape(tp, D_MODEL), ng, w_in_bf, qg, kg, gmat, cos_p, sin_p, zero_conv, cw, cng,
        tm=ATT_T, nseq=1, carry=True, qv_transposed=True)
    att_p = _attn_prompt(qt_p, kb_p, vt_p, lamv, subln_g[0].reshape(V_DIM, 1), lambda_init)

    seqs_per_tile = OUT_TM // ss
    q_s, kf_s, kb_s, vf_s, vb_s, yc_s, tail_s = _inproj(
        x_sample.reshape(ts, D_MODEL), ng, w_in_bf, qg, kg, gmat, cos_s, sin_s, state_conv[0], cw, cng,
        tm=OUT_TM, nseq=seqs_per_tile, carry=False, qv_transposed=False)
    att_s = _attn_sample(q_s, cache_k[0].reshape(bs, past, QK_WIDTH), cache_v[0].reshape(bs, past, ATT_WIDTH),
                         kb_s, vb_s, lamv, sg, lambda_init)

    rw = jnp.zeros((D_MODEL, ROUTE_LANES), F32)
    rw = rw.at[:, 0:N_GROUPS].set(router_group_w[0]).at[:, N_GROUPS:N_GROUPS + N_EXPERTS].set(router_expert_w[0])
    rb = jnp.zeros((1, ROUTE_LANES), F32)
    rb = rb.at[0, 0:N_GROUPS].set(router_group_b[0]).at[0, N_GROUPS:N_GROUPS + N_EXPERTS].set(router_expert_b[0])
    nf = norm_ffn_g[0].reshape(1, D_MODEL)
    rw_bf = rw.astype(BF16)
    x1, xf, route = _outproj(att_p, yc_p, x_prompt.reshape(tp, D_MODEL),
                             att_s, yc_s, x_sample.reshape(ts, D_MODEL), w_top, w_bot, nf, rw_bf, rb)

    n = t_all * TOP_K
    n_blocks = n // MOE_BM + N_EXPERTS
    dest, blk_e, first, nused = _routing_tables(route, n_blocks)
    xs = _dispatch(dest, xf, n_blocks * MOE_BM)
    outs = _experts(blk_e, first, nused, xs, expert_w_gate[0], expert_w_up[0], expert_w_down[0])
    y_p = _combine(dest, x1, route, outs, 0, tp)
    y_s = _combine(dest, x1, route, outs, tp, ts)

    return (y_p.reshape(bp, sp, D_MODEL),
            y_s.reshape(bs, ss, D_MODEL),
            kf_p.reshape(1, bp, sp, N_HEADS, 2, HEAD_DIM),
            vf_p.reshape(1, bp, sp, N_HEADS, V_DIM),
            tail_p.reshape(1, bp, CONV_K - 1, CONV_CH),
            kf_s.reshape(1, bs, ss, N_HEADS, 2, HEAD_DIM),
            vf_s.reshape(1, bs, ss, N_HEADS, V_DIM),
            tail_s.reshape(1, bs, CONV_K - 1, CONV_CH))
```

```python
import functools
import math

import jax
import jax.numpy as jnp
from jax import lax
from jax.experimental import pallas as pl
from jax.experimental.pallas import tpu as pltpu

F32 = jnp.float32
BF16 = jnp.bfloat16

D_MODEL = 2048
CHUNK = 64
HEAD_DIM = 64
V_DIM = 2 * HEAD_DIM
N_HEADS = 8
QK_WIDTH = N_HEADS * 2 * HEAD_DIM
ATT_WIDTH = N_HEADS * V_DIM
CONV_CH = 1024
CONV_K = 3
ROPE_THETA = 10000.0
N_GROUPS = 8
EXPERTS_PER_GROUP = 8
N_EXPERTS = N_GROUPS * EXPERTS_PER_GROUP
TOP_K = 2
D_EXPERT = D_MODEL // 4
NORM_EPS = 1e-6
SECTION = 1024
N_SECTIONS = 6

LANES = 128
IN_CH = 256
ATT_AHEAD = 2
ATT_CB = 256
ATT_T = 512
OUT_TM = 256
MOE_BM = 256
DISPATCH_TM = 512
ROUTE_LANES = LANES
MASKED = -1e30
Q_SCALE = HEAD_DIM ** -0.5 * math.log2(math.e)
MIB = 1024 * 1024


def _params(sem, vmem_mib):
    return pltpu.CompilerParams(dimension_semantics=sem, vmem_limit_bytes=vmem_mib * MIB)


def _inproj_kernel(x_ref, ng_ref, w_ref, qg_ref, kg_ref, gmat_ref, cos_ref, sin_ref,
                   prev_ref, cw_ref, cng_ref,
                   q_ref, kf_ref, kb_ref, vf_ref, vb_ref, yc_ref, tail_ref,
                   xn_s, gb_s, c_s, z_s, u_s, *, nseq, carry, qv_transposed):
    i = pl.program_id(0)
    j = pl.program_id(1)
    tm = x_ref.shape[0]
    seq = tm // nseq

    @pl.when(j == 0)
    def _():
        x = x_ref[...]
        inv = lax.rsqrt(jnp.mean(x * x, axis=-1, keepdims=True) + NORM_EPS)
        xn_s[...] = (x * inv * ng_ref[...]).astype(BF16)
        if carry:
            @pl.when(i == 0)
            def _():
                u_s[6:8, :] = prev_ref[0]

    def chunk_dot(c):
        return jnp.dot(xn_s[...], w_ref[:, c * IN_CH:(c + 1) * IN_CH], preferred_element_type=F32)

    def pipelined(epilogue):
        acc = chunk_dot(0)
        for c in range(SECTION // IN_CH):
            nxt = chunk_dot(c + 1) if c + 1 < SECTION // IN_CH else None
            epilogue(acc, slice(c * IN_CH, (c + 1) * IN_CH))
            acc = nxt

    def head_norm_rope(a, g):
        ms = jnp.dot((a * a).astype(BF16), gmat_ref[...], preferred_element_type=F32)
        y = a * lax.rsqrt(ms + NORM_EPS) * g
        reps = IN_CH // LANES
        cos = jnp.concatenate([cos_ref[...]] * reps, axis=1)
        sin = jnp.concatenate([sin_ref[...]] * reps, axis=1)
        lane = lax.broadcasted_iota(jnp.int32, y.shape, 1)
        first = (lane & (HEAD_DIM - 1)) < HEAD_DIM // 2
        partner = jnp.where(first, pltpu.roll(y, IN_CH - HEAD_DIM // 2, 1),
                            pltpu.roll(y, HEAD_DIM // 2, 1))
        return y * cos + partner * sin

    for section in range(N_SECTIONS):

        def epilogue(acc, cols, section=section):
            if section == 0:
                q = head_norm_rope(acc, qg_ref[:, cols]) * Q_SCALE
                if qv_transposed:
                    q_ref[0, cols, :] = q.T.astype(BF16)
                else:
                    q_ref[:, cols] = q.astype(BF16)
            elif section == 1:
                k = head_norm_rope(acc, kg_ref[:, cols])
                kf_ref[:, cols] = k
                kb_ref[:, cols] = k.astype(BF16)
            elif section == 2:
                vf_ref[:, cols] = acc
                if qv_transposed:
                    vb_ref[0, cols, :] = acc.T.astype(BF16)
                else:
                    vb_ref[:, cols] = acc.astype(BF16)
            elif section == 3:
                gb_s[:, cols] = acc
            elif section == 4:
                c_s[:, cols] = acc
            else:
                u = c_s[:, cols] * acc
                w0 = cw_ref[0:1, cols]
                w1 = cw_ref[1:2, cols]
                w2 = cw_ref[2:3, cols]
                for s in range(nseq):
                    rows = slice(s * seq, (s + 1) * seq)
                    if not carry:
                        u_s[6:8, cols] = prev_ref[s, :, cols]
                    u_s[8:8 + seq, cols] = u[rows]
                    conv = (w0 * u_s[6:6 + seq, cols] + w1 * u_s[7:7 + seq, cols]
                            + w2 * u_s[8:8 + seq, cols])
                    z_s[rows, cols] = gb_s[rows, cols] * conv
                    tail = u_s[seq + 6:seq + 8, cols]
                    tail_ref[s, :, cols] = tail
                    if carry:
                        u_s[6:8, cols] = tail

        @pl.when(j == section)
        def _(section=section, epilogue=epilogue):
            pipelined(epilogue)
            if section == N_SECTIONS - 1:
                z = z_s[...]
                inv = lax.rsqrt(jnp.mean(z * z, axis=-1, keepdims=True) + NORM_EPS)
                yc_ref[...] = (z * inv * cng_ref[...]).astype(BF16)


def _inproj(x2d, norm_g, w_in_bf, qg, kg, gmat, cos, sin, conv_prev, conv_w, conv_norm_g,
            *, tm, nseq, carry, qv_transposed):
    t = x2d.shape[0]
    ni = t // tm
    seq = tm // nseq
    row = lambda i, j: (i, 0)
    const = lambda i, j: (0, 0)
    if qv_transposed:
        qv_shape = jax.ShapeDtypeStruct((ni, QK_WIDTH, tm), BF16)
        qv_spec = pl.BlockSpec((1, QK_WIDTH, tm), lambda i, j: (i, 0, 0))
    else:
        qv_shape = jax.ShapeDtypeStruct((t, QK_WIDTH), BF16)
        qv_spec = pl.BlockSpec((tm, QK_WIDTH), row)
    if carry:
        prev_spec = pl.BlockSpec((1, CONV_K - 1, CONV_CH), lambda i, j: (0, 0, 0))
        tail_shape = jax.ShapeDtypeStruct((1, CONV_K - 1, CONV_CH), F32)
        tail_spec = pl.BlockSpec((1, CONV_K - 1, CONV_CH), lambda i, j: (0, 0, 0))
    else:
        prev_spec = pl.BlockSpec((nseq, CONV_K - 1, CONV_CH), lambda i, j: (i, 0, 0))
        tail_shape = jax.ShapeDtypeStruct((ni * nseq, CONV_K - 1, CONV_CH), F32)
        tail_spec = pl.BlockSpec((nseq, CONV_K - 1, CONV_CH), lambda i, j: (i, 0, 0))
    kern = functools.partial(_inproj_kernel, nseq=nseq, carry=carry, qv_transposed=qv_transposed)
    return pl.pallas_call(
        kern,
        grid=(ni, N_SECTIONS),
        in_specs=[
            pl.BlockSpec((tm, D_MODEL), row),
            pl.BlockSpec((1, D_MODEL), const),
            pl.BlockSpec((D_MODEL, SECTION), lambda i, j: (0, j)),
            pl.BlockSpec((1, SECTION), const),
            pl.BlockSpec((1, SECTION), const),
            pl.BlockSpec((IN_CH, IN_CH), const),
            pl.BlockSpec((tm, LANES), row),
            pl.BlockSpec((tm, LANES), row),
            prev_spec,
            pl.BlockSpec((CONV_K, CONV_CH), const),
            pl.BlockSpec((1, CONV_CH), const),
        ],
        out_specs=[
            qv_spec,
            pl.BlockSpec((tm, QK_WIDTH), row),
            pl.BlockSpec((tm, QK_WIDTH), row),
            pl.BlockSpec((tm, ATT_WIDTH), row),
            qv_spec,
            pl.BlockSpec((tm, CONV_CH), row),
            tail_spec,
        ],
        out_shape=[
            qv_shape,
            jax.ShapeDtypeStruct((t, QK_WIDTH), F32),
            jax.ShapeDtypeStruct((t, QK_WIDTH), BF16),
            jax.ShapeDtypeStruct((t, ATT_WIDTH), F32),
            qv_shape,
            jax.ShapeDtypeStruct((t, CONV_CH), BF16),
            tail_shape,
        ],
        scratch_shapes=[
            pltpu.VMEM((tm, D_MODEL), BF16),
            pltpu.VMEM((tm, CONV_CH), F32),
            pltpu.VMEM((tm, CONV_CH), F32),
            pltpu.VMEM((tm, CONV_CH), F32),
            pltpu.VMEM((seq + 8, CONV_CH), F32),
        ],
        compiler_params=_params(("arbitrary", "arbitrary"), 56),
        name="inproj_carry" if carry else "inproj_seqs",
    )(x2d, norm_g, w_in_bf, qg, kg, gmat, cos, sin, conv_prev, conv_w, conv_norm_g)


def _lambda_full(lamv_ref, lambda_init):
    lv = lamv_ref[...]
    a = jnp.sum(lv[0:1] * lv[1:2], axis=-1, keepdims=True)
    b = jnp.sum(lv[2:3] * lv[3:4], axis=-1, keepdims=True)
    return jnp.exp(a) - jnp.exp(b) + lambda_init


def _split_heads(q):
    lane = lax.broadcasted_iota(jnp.int32, q.shape, 1)
    zero = jnp.zeros_like(q)
    return jnp.where(lane < HEAD_DIM, q, zero), jnp.where(lane >= HEAD_DIM, q, zero)


def _diff_finish(acc, l, lam, g, lambda_init, rows):
    a = acc / l
    d = a[0:rows] - lam * a[rows:2 * rows]
    inv = lax.rsqrt(jnp.mean(d * d, axis=-1, keepdims=True) + NORM_EPS)
    return d * inv * g * (1.0 - lambda_init)


def _attn_prompt_kernel(qt_ref, k_ref, vt_ref, lamv_ref, g_ref, o_ref, qq_s, m_s, l_s, acc_s, s_s,
                        *, lambda_init):
    i = pl.program_id(1)
    t = ATT_T
    qt = qt_ref[0]
    feat = lax.broadcasted_iota(jnp.int32, qt.shape, 0)
    zero = jnp.zeros_like(qt)
    qq_s[:, 0:t] = jnp.where(feat < HEAD_DIM, qt, zero)
    qq_s[:, t:2 * t] = jnp.where(feat >= HEAD_DIM, qt, zero)
    m_s[...] = jnp.full(m_s.shape, MASKED, F32)
    l_s[...] = jnp.zeros(l_s.shape, F32)
    acc_s[...] = jnp.zeros(acc_s.shape, F32)

    nblk = 2 * t // ATT_CB

    def scores(kt, cb):
        k = k_ref[pl.ds(pl.multiple_of(kt * t, t), t), :]
        s_s[cb] = jnp.dot(k, qq_s[:, cb * ATT_CB:(cb + 1) * ATT_CB], preferred_element_type=F32)

    def step(kt, diagonal):
        vt = vt_ref[kt]
        for cb in range(nblk):
            cs = slice(cb * ATT_CB, (cb + 1) * ATT_CB)
            s = s_s[cb]
            if diagonal:
                key = lax.broadcasted_iota(jnp.int32, s.shape, 0)
                qry = lax.broadcasted_iota(jnp.int32, s.shape, 1) + (cb * ATT_CB) % t
                s = jnp.where((key // CHUNK) <= (qry // CHUNK), s, MASKED)
            m_prev = m_s[:, cs]
            m_new = jnp.maximum(m_prev, jnp.max(s, axis=0, keepdims=True))
            alpha = jnp.exp2(m_prev - m_new)
            p = jnp.exp2(s - m_new)
            l_s[:, cs] = alpha * l_s[:, cs] + jnp.sum(p, axis=0, keepdims=True)
            m_s[:, cs] = m_new
            ahead = cb + ATT_AHEAD
            if ahead < nblk:
                scores(kt, ahead)
            elif not diagonal:
                scores(kt + 1, ahead - nblk)
            acc_s[:, cs] = alpha * acc_s[:, cs] + jnp.dot(vt, p.astype(BF16),
                                                          preferred_element_type=F32)

    def body(pair, carry):
        step(2 * pair, False)
        step(2 * pair + 1, False)
        return carry

    for cb in range(ATT_AHEAD):
        scores(0, cb)
    lax.fori_loop(0, i // 2, body, 0)

    @pl.when(i % 2 == 1)
    def _():
        step(i - 1, False)

    step(i, True)
    lam = _lambda_full(lamv_ref, lambda_init)
    a = acc_s[...] / l_s[...]
    d = a[:, 0:t] - lam * a[:, t:2 * t]
    inv = lax.rsqrt(jnp.mean(d * d, axis=0, keepdims=True) + NORM_EPS)
    y = d * inv * g_ref[...] * (1.0 - lambda_init)
    o_ref[...] = y.T.astype(BF16)


def _attn_prompt(qt_bf, k_bf, vt_bf, lamv, subln_g_col, lambda_init):
    nq = qt_bf.shape[0]
    t = nq * ATT_T
    kern = functools.partial(_attn_prompt_kernel, lambda_init=lambda_init)
    return pl.pallas_call(
        kern,
        grid=(N_HEADS, nq),
        in_specs=[
            pl.BlockSpec((1, V_DIM, ATT_T), lambda h, i: (i, h, 0)),
            pl.BlockSpec((t, V_DIM), lambda h, i: (0, h)),
            pl.BlockSpec((nq, V_DIM, ATT_T), lambda h, i: (0, h, 0)),
            pl.BlockSpec((4, HEAD_DIM), lambda h, i: (0, 0)),
            pl.BlockSpec((V_DIM, 1), lambda h, i: (0, 0)),
        ],
        out_specs=pl.BlockSpec((ATT_T, V_DIM), lambda h, i: (i, h)),
        out_shape=jax.ShapeDtypeStruct((t, ATT_WIDTH), BF16),
        scratch_shapes=[
            pltpu.VMEM((V_DIM, 2 * ATT_T), BF16),
            pltpu.VMEM((1, 2 * ATT_T), F32),
            pltpu.VMEM((1, 2 * ATT_T), F32),
            pltpu.VMEM((V_DIM, 2 * ATT_T), F32),
            pltpu.VMEM((2 * ATT_T // ATT_CB, ATT_T, ATT_CB), F32),
        ],
        compiler_params=_params(("arbitrary", "arbitrary"), 48),
        name="attn_prompt",
    )(qt_bf, k_bf, vt_bf, lamv, subln_g_col)


def _attn_sample_kernel(q_ref, kc_ref, vc_ref, kn_ref, vn_ref, lamv_ref, g_ref, o_ref,
                        *, lambda_init, past):
    rows = q_ref.shape[0]
    lam = _lambda_full(lamv_ref, lambda_init)
    contract_last = (((1,), (1,)), ((), ()))
    for h in range(N_HEADS):
        hs = slice(h * V_DIM, (h + 1) * V_DIM)
        q1, q2 = _split_heads(q_ref[:, hs])
        qq = jnp.concatenate([q1, q2], axis=0)
        s_c = jnp.dot(qq, kc_ref[0, hs, :].astype(BF16), preferred_element_type=F32)
        s_n = lax.dot_general(qq, kn_ref[:, hs], contract_last, preferred_element_type=F32)
        r = lax.broadcasted_iota(jnp.int32, s_n.shape, 0)
        c = lax.broadcasted_iota(jnp.int32, s_n.shape, 1)
        s_n = jnp.where(((past + c) // CHUNK) <= ((past + r % rows) // CHUNK), s_n, MASKED)
        m = jnp.maximum(jnp.max(s_c, axis=-1, keepdims=True), jnp.max(s_n, axis=-1, keepdims=True))
        p_c = jnp.exp2(s_c - m)
        p_n = jnp.exp2(s_n - m)
        l = jnp.sum(p_c, axis=-1, keepdims=True) + jnp.sum(p_n, axis=-1, keepdims=True)
        v_c = vc_ref[0, pl.ds(h, past, stride=N_HEADS), :].astype(BF16)
        acc = (jnp.dot(p_c.astype(BF16), v_c, preferred_element_type=F32)
               + jnp.dot(p_n.astype(BF16), vn_ref[:, hs], preferred_element_type=F32))
        o_ref[:, hs] = _diff_finish(acc, l, lam, g_ref[...], lambda_init, rows).astype(BF16)


def _attn_sample(q_bf, kt_cache, v_cache, k_bf, v_bf, lamv, subln_g, lambda_init):
    nb, _, past = kt_cache.shape
    rows = q_bf.shape[0] // nb
    kern = functools.partial(_attn_sample_kernel, lambda_init=lambda_init, past=past)
    new_spec = pl.BlockSpec((rows, ATT_WIDTH), lambda b: (b, 0))
    return pl.pallas_call(
        kern,
        grid=(nb,),
        in_specs=[
            new_spec,
            pl.BlockSpec((1, QK_WIDTH, past), lambda b: (b, 0, 0)),
            pl.BlockSpec((1, past * N_HEADS, V_DIM), lambda b: (b, 0, 0)),
            new_spec, new_spec,
            pl.BlockSpec((4, HEAD_DIM), lambda b: (0, 0)),
            pl.BlockSpec((1, V_DIM), lambda b: (0, 0)),
        ],
        out_specs=new_spec,
        out_shape=jax.ShapeDtypeStruct(q_bf.shape, BF16),
        compiler_params=_params(("arbitrary",), 48),
        name="attn_sample",
    )(q_bf, kt_cache, v_cache, k_bf, v_bf, lamv, subln_g)


def _outproj_kernel(att_p_ref, yc_p_ref, x_p_ref, att_s_ref, yc_s_ref, x_s_ref,
                    wt_ref, wb_ref, ng_ref, rw_ref, rb_ref, x1_ref, xf_ref, route_ref, *, n_prompt_tiles):
    i = pl.program_id(0)
    shared = (wt_ref, wb_ref, ng_ref, rw_ref, rb_ref, x1_ref, xf_ref, route_ref)

    @pl.when(i < n_prompt_tiles)
    def _():
        _outproj_tile(att_p_ref, yc_p_ref, x_p_ref, *shared)

    @pl.when(i >= n_prompt_tiles)
    def _():
        _outproj_tile(att_s_ref, yc_s_ref, x_s_ref, *shared)


def _outproj_tile(att_ref, yc_ref, x_ref, wt_ref, wb_ref, ng_ref, rw_ref, rb_ref,
                  x1_ref, xf_ref, route_ref):
    o = (jnp.dot(att_ref[...], wt_ref[...], preferred_element_type=F32)
         + jnp.dot(yc_ref[...], wb_ref[...], preferred_element_type=F32))
    x1 = x_ref[...] + o
    x1_ref[...] = x1
    xf = x1 * lax.rsqrt(jnp.mean(x1 * x1, axis=-1, keepdims=True) + NORM_EPS) * ng_ref[...]
    xf_ref[...] = xf
    logits = jnp.dot(xf.astype(BF16), rw_ref[...], preferred_element_type=F32) + rb_ref[...]

    lane = lax.broadcasted_iota(jnp.int32, logits.shape, 1)
    neg = -jnp.inf
    is_group = lane < N_GROUPS
    gl = jnp.where(is_group, logits, neg)
    gmax = jnp.max(gl, axis=-1, keepdims=True)
    grp = jnp.min(jnp.where(gl == gmax, lane, ROUTE_LANES), axis=-1, keepdims=True)
    gsum = jnp.sum(jnp.where(is_group, jnp.exp(gl - gmax), 0.0), axis=-1, keepdims=True)
    g_w = 1.0 / gsum
    e_lane = lane - N_GROUPS
    in_grp = (e_lane >= 0) & (e_lane < N_EXPERTS) & ((e_lane // EXPERTS_PER_GROUP) == grp)
    el = jnp.where(in_grp, logits, neg)
    t1 = jnp.max(el, axis=-1, keepdims=True)
    i1 = jnp.min(jnp.where(el == t1, lane, ROUTE_LANES), axis=-1, keepdims=True)
    el2 = jnp.where(lane == i1, neg, el)
    t2 = jnp.max(el2, axis=-1, keepdims=True)
    i2 = jnp.min(jnp.where(el2 == t2, lane, ROUTE_LANES), axis=-1, keepdims=True)
    r21 = jnp.exp(t2 - t1)
    w0 = g_w / (1.0 + r21)
    w1 = g_w * r21 / (1.0 + r21)
    e0 = (i1 - N_GROUPS).astype(F32)
    e1 = (i2 - N_GROUPS).astype(F32)
    route_ref[...] = jnp.where(lane == 0, e0, jnp.where(lane == 1, e1,
                               jnp.where(lane == 2, w0, jnp.where(lane == 3, w1, 0.0))))


def _outproj(att_p, yc_p, x_p, att_s, yc_s, x_s, w_top, w_bot, norm_g, rw, rb):
    tm = OUT_TM
    n_p = x_p.shape[0] // tm
    n_s = x_s.shape[0] // tm
    t_all = x_p.shape[0] + x_s.shape[0]
    prow = lambda i: (jnp.minimum(i, n_p - 1), 0)
    srow = lambda i: (jnp.maximum(i - n_p, 0), 0)
    row = lambda i: (i, 0)
    const = lambda i: (0, 0)
    kern = functools.partial(_outproj_kernel, n_prompt_tiles=n_p)
    return pl.pallas_call(
        kern,
        grid=(n_p + n_s,),
        in_specs=[
            pl.BlockSpec((tm, ATT_WIDTH), prow),
            pl.BlockSpec((tm, CONV_CH), prow),
            pl.BlockSpec((tm, D_MODEL), prow),
            pl.BlockSpec((tm, ATT_WIDTH), srow),
            pl.BlockSpec((tm, CONV_CH), srow),
            pl.BlockSpec((tm, D_MODEL), srow),
            pl.BlockSpec((ATT_WIDTH, D_MODEL), const),
            pl.BlockSpec((CONV_CH, D_MODEL), const),
            pl.BlockSpec((1, D_MODEL), const),
            pl.BlockSpec((D_MODEL, ROUTE_LANES), const),
            pl.BlockSpec((1, ROUTE_LANES), const),
        ],
        out_specs=[
            pl.BlockSpec((tm, D_MODEL), row),
            pl.BlockSpec((tm, D_MODEL), row),
            pl.BlockSpec((tm, ROUTE_LANES), row),
        ],
        out_shape=[
            jax.ShapeDtypeStruct((t_all, D_MODEL), F32),
            jax.ShapeDtypeStruct((t_all, D_MODEL), F32),
            jax.ShapeDtypeStruct((t_all, ROUTE_LANES), F32),
        ],
        compiler_params=_params(("arbitrary",), 52),
        name="outproj_router",
    )(att_p, yc_p, x_p, att_s, yc_s, x_s, w_top, w_bot, norm_g, rw, rb)


def _dispatch_copy(xf_ref, xs_hbm, sem, src_row, dst_row):
    return pltpu.make_async_copy(xf_ref.at[pl.ds(src_row, 1)], xs_hbm.at[pl.ds(dst_row, 1)], sem)


def _dispatch_kernel(dest_ref, xf_ref, xs_init_hbm, xs_hbm, sem):
    del xs_init_hbm
    tm = DISPATCH_TM

    def issue(t, carry):
        for k in range(TOP_K):
            _dispatch_copy(xf_ref, xs_hbm, sem, t, dest_ref[0, 0, TOP_K * t + k]).start()
        return carry

    def drain(t, carry):
        for k in range(TOP_K):
            _dispatch_copy(xf_ref, xs_hbm, sem, t, dest_ref[0, 0, TOP_K * t + k]).wait()
        return carry

    lax.fori_loop(0, tm, issue, 0)
    lax.fori_loop(0, tm, drain, 0)


def _dispatch(dest, xf, cap):
    t_all = xf.shape[0]
    tm = DISPATCH_TM
    dest3 = dest.reshape(t_all // tm, 1, TOP_K * tm)
    xs_init = jnp.zeros((cap, D_MODEL), F32)
    return pl.pallas_call(
        _dispatch_kernel,
        grid=(t_all // tm,),
        in_specs=[
            pl.BlockSpec((1, 1, TOP_K * tm), lambda i: (i, 0, 0), memory_space=pltpu.SMEM),
            pl.BlockSpec((tm, D_MODEL), lambda i: (i, 0)),
            pl.BlockSpec(memory_space=pl.ANY),
        ],
        out_specs=pl.BlockSpec(memory_space=pl.ANY),
        out_shape=jax.ShapeDtypeStruct((cap, D_MODEL), F32),
        scratch_shapes=[pltpu.SemaphoreType.DMA(())],
        input_output_aliases={2: 0},
        compiler_params=_params(("arbitrary",), 24),
        name="moe_dispatch",
    )(dest3, xf, xs_init)


def _experts_kernel(blk_e_ref, first_ref, nused_ref, x_ref, wg_ref, wu_ref, wd_ref, o_ref,
                    wgu_s, wd_s):
    del blk_e_ref
    i = pl.program_id(0)

    @pl.when(i < nused_ref[0])
    def _():
        @pl.when(first_ref[i] == 1)
        def _():
            wgu_s[:, 0:D_EXPERT] = wg_ref[0].astype(BF16)
            wgu_s[:, D_EXPERT:2 * D_EXPERT] = wu_ref[0].astype(BF16)
            wd_s[...] = wd_ref[0].astype(BF16)

        gu = jnp.dot(x_ref[...].astype(BF16), wgu_s[...], preferred_element_type=F32)
        g = gu[:, 0:D_EXPERT]
        u = gu[:, D_EXPERT:2 * D_EXPERT]
        h = g / (1.0 + jnp.exp(-g)) * u
        o_ref[...] = jnp.dot(h.astype(BF16), wd_s[...], preferred_element_type=F32)

    @pl.when(i >= nused_ref[0])
    def _():
        o_ref[...] = jnp.zeros(o_ref.shape, F32)


def _experts(blk_e, first, nused, xs, w_gate, w_up, w_down):
    cap = xs.shape[0]
    bm = MOE_BM
    rows = lambda i, be, fi, nu: (jnp.minimum(i, nu[0] - 1), 0)
    wsel = lambda i, be, fi, nu: (be[i], 0, 0)
    grid_spec = pltpu.PrefetchScalarGridSpec(
        num_scalar_prefetch=3,
        grid=(cap // bm,),
        in_specs=[
            pl.BlockSpec((bm, D_MODEL), rows),
            pl.BlockSpec((1, D_MODEL, D_EXPERT), wsel),
            pl.BlockSpec((1, D_MODEL, D_EXPERT), wsel),
            pl.BlockSpec((1, D_EXPERT, D_MODEL), wsel),
        ],
        out_specs=pl.BlockSpec((bm, D_MODEL), lambda i, be, fi, nu: (i, 0)),
        scratch_shapes=[
            pltpu.VMEM((D_MODEL, 2 * D_EXPERT), BF16),
            pltpu.VMEM((D_EXPERT, D_MODEL), BF16),
        ],
    )
    return pl.pallas_call(
        _experts_kernel,
        grid_spec=grid_spec,
        out_shape=jax.ShapeDtypeStruct((cap, D_MODEL), F32),
        compiler_params=_params(("arbitrary",), 52),
        name="moe_experts",
    )(blk_e, first, nused, xs, w_gate, w_up, w_down)


def _combine_copy(outs_hbm, g_s, sem, src_row, k, t):
    return pltpu.make_async_copy(outs_hbm.at[pl.ds(src_row, 1)], g_s.at[k, pl.ds(t, 1)], sem)


def _combine_kernel(dest_ref, x1_ref, route_ref, outs_hbm, y_ref, g_s, sem):
    tm = x1_ref.shape[0]

    def issue(t, carry):
        for k in range(TOP_K):
            _combine_copy(outs_hbm, g_s, sem, dest_ref[0, 0, TOP_K * t + k], k, t).start()
        return carry

    def drain(t, carry):
        for k in range(TOP_K):
            _combine_copy(outs_hbm, g_s, sem, dest_ref[0, 0, TOP_K * t + k], k, t).wait()
        return carry

    lax.fori_loop(0, tm, issue, 0)
    lax.fori_loop(0, tm, drain, 0)
    r = route_ref[...]
    y_ref[...] = x1_ref[...] + r[:, 2:3] * g_s[0] + r[:, 3:4] * g_s[1]


def _combine(dest, x1, route, outs, row_off, t):
    tm = OUT_TM
    t_all = x1.shape[0]
    off = row_off // tm
    dest3 = dest.reshape(t_all // tm, 1, TOP_K * tm)
    return pl.pallas_call(
        _combine_kernel,
        grid=(t // tm,),
        in_specs=[
            pl.BlockSpec((1, 1, TOP_K * tm), lambda i: (i + off, 0, 0), memory_space=pltpu.SMEM),
            pl.BlockSpec((tm, D_MODEL), lambda i: (i + off, 0)),
            pl.BlockSpec((tm, ROUTE_LANES), lambda i: (i + off, 0)),
            pl.BlockSpec(memory_space=pl.ANY),
        ],
        out_specs=pl.BlockSpec((tm, D_MODEL), lambda i: (i, 0)),
        out_shape=jax.ShapeDtypeStruct((t, D_MODEL), F32),
        scratch_shapes=[pltpu.VMEM((TOP_K, tm, D_MODEL), F32), pltpu.SemaphoreType.DMA(())],
        compiler_params=_params(("arbitrary",), 32),
        name="moe_combine",
    )(dest3, x1, route, outs)


def _routing_tables(route, n_blocks):
    bm = MOE_BM
    flat_e = route[:, 0:TOP_K].astype(jnp.int32).reshape(-1)
    onehot = (flat_e[:, None] == jnp.arange(N_EXPERTS, dtype=jnp.int32)[None, :]).astype(jnp.int32)
    csum = jnp.cumsum(onehot, axis=0)
    rank = jnp.sum(csum * onehot, axis=1) - 1
    counts = csum[-1]
    padded = (counts + bm - 1) // bm * bm
    pad_end = jnp.cumsum(padded)
    pad_start = pad_end - padded
    dest = (pad_start[flat_e] + rank).astype(jnp.int32)
    nused = (pad_end[-1] // bm).astype(jnp.int32)
    blk = jnp.arange(n_blocks, dtype=jnp.int32)
    blk_e = jnp.searchsorted(pad_end, jnp.minimum(blk, nused - 1) * bm, side='right').astype(jnp.int32)
    blk_e = jnp.minimum(blk_e, N_EXPERTS - 1)
    first = jnp.concatenate([jnp.ones((1,), jnp.int32),
                             (blk_e[1:] != blk_e[:-1]).astype(jnp.int32)])
    return dest, blk_e, first, nused.reshape(1)


def _rope_tables(pos):
    half = HEAD_DIM // 2
    inv = ROPE_THETA ** (-jnp.arange(half, dtype=F32) / half)
    ang = pos.astype(F32)[:, None] * inv[None, :]
    cos = jnp.cos(ang)
    sin = jnp.sin(ang)
    cos_h = jnp.concatenate([cos, cos], axis=-1)
    sin_h = jnp.concatenate([-sin, sin], axis=-1)
    reps = LANES // HEAD_DIM
    return jnp.tile(cos_h, (1, reps)), jnp.tile(sin_h, (1, reps))


def kernel(x_prompt, x_sample, cache_k, cache_v, state_conv, norm_mix_g, w_in, q_norm_g, k_norm_g, lambda_q1, lambda_k1, lambda_q2, lambda_k2, subln_g, conv_w, conv_norm_g, w_out, norm_ffn_g, router_group_w, router_group_b, router_expert_w, router_expert_b, expert_w_gate, expert_w_up, expert_w_down):
    assert w_in.shape[0] == 1, "single-layer step"
    bp, sp, _ = x_prompt.shape
    bs, ss, _ = x_sample.shape
    past = cache_k.shape[2]
    assert bp == 1 and sp % ATT_T == 0
    tp = bp * sp
    ts = bs * ss
    t_all = tp + ts
    lambda_init = 0.8 - 0.6 * math.exp(-0.3 * 0)

    w_in_bf = w_in[0].astype(BF16)
    w_out_bf = w_out[0].astype(BF16)
    w_top, w_bot = w_out_bf[:ATT_WIDTH], w_out_bf[ATT_WIDTH:]
    ng = norm_mix_g[0].reshape(1, D_MODEL)
    qg = jnp.tile(q_norm_g[0], QK_WIDTH // HEAD_DIM).reshape(1, QK_WIDTH)
    kg = jnp.tile(k_norm_g[0], QK_WIDTH // HEAD_DIM).reshape(1, QK_WIDTH)
    head_of = jnp.arange(IN_CH, dtype=jnp.int32) // HEAD_DIM
    gmat = jnp.where(head_of[:, None] == head_of[None, :], 1.0 / HEAD_DIM, 0.0).astype(BF16)
    lamv = jnp.stack([lambda_q1[0], lambda_k1[0], lambda_q2[0], lambda_k2[0]]).astype(F32)
    sg = subln_g[0].reshape(1, V_DIM)
    cw = conv_w[0]
    cng = conv_norm_g[0].reshape(1, CONV_CH)
    cos_p, sin_p = _rope_tables(jnp.arange(sp, dtype=jnp.int32))
    cos_s, sin_s = _rope_tables(jnp.tile(past + jnp.arange(ss, dtype=jnp.int32), bs))

    zero_conv = jnp.zeros((1, CONV_K - 1, CONV_CH), F32)
    qt_p, kf_p, kb_p, vf_p, vt_p, yc_p, tail_p = _inproj(
        x_prompt.reshape(tp, D_MODEL), ng, w_in_bf, qg, kg, gmat, cos_p, sin_p, zero_conv, cw, cng,
        tm=ATT_T, nseq=1, carry=True, qv_transposed=True)
    att_p = _attn_prompt(qt_p, kb_p, vt_p, lamv, subln_g[0].reshape(V_DIM, 1), lambda_init)

    seqs_per_tile = OUT_TM // ss
    q_s, kf_s, kb_s, vf_s, vb_s, yc_s, tail_s = _inproj(
        x_sample.reshape(ts, D_MODEL), ng, w_in_bf, qg, kg, gmat, cos_s, sin_s, state_conv[0], cw, cng,
        tm=OUT_TM, nseq=seqs_per_tile, carry=False, qv_transposed=False)
    kt_cache = jnp.transpose(cache_k[0], (0, 2, 3, 4, 1)).reshape(bs, QK_WIDTH, past)
    att_s = _attn_sample(q_s, kt_cache, cache_v[0].reshape(bs, past * N_HEADS, V_DIM),
                         kb_s, vb_s, lamv, sg, lambda_init)

    rw = jnp.zeros((D_MODEL, ROUTE_LANES), F32)
    rw = rw.at[:, 0:N_GROUPS].set(router_group_w[0]).at[:, N_GROUPS:N_GROUPS + N_EXPERTS].set(router_expert_w[0])
    rb = jnp.zeros((1, ROUTE_LANES), F32)
    rb = rb.at[0, 0:N_GROUPS].set(router_group_b[0]).at[0, N_GROUPS:N_GROUPS + N_EXPERTS].set(router_expert_b[0])
    nf = norm_ffn_g[0].reshape(1, D_MODEL)
    rw_bf = rw.astype(BF16)
    x1, xf, route = _outproj(att_p, yc_p, x_prompt.reshape(tp, D_MODEL),
                             att_s, yc_s, x_sample.reshape(ts, D_MODEL), w_top, w_bot, nf, rw_bf, rb)

    n = t_all * TOP_K
    n_blocks = n // MOE_BM + N_EXPERTS
    dest, blk_e, first, nused = _routing_tables(route, n_blocks)
    xs = _dispatch(dest, xf, n_blocks * MOE_BM)
    outs = _experts(blk_e, first, nused, xs, expert_w_gate[0], expert_w_up[0], expert_w_down[0])
    y_p = _combine(dest, x1, route, outs, 0, tp)
    y_s = _combine(dest, x1, route, outs, tp, ts)

    return (y_p.reshape(bp, sp, D_MODEL),
            y_s.reshape(bs, ss, D_MODEL),
            kf_p.reshape(1, bp, sp, N_HEADS, 2, HEAD_DIM),
            vf_p.reshape(1, bp, sp, N_HEADS, V_DIM),
            tail_p.reshape(1, bp, CONV_K - 1, CONV_CH),
            kf_s.reshape(1, bs, ss, N_HEADS, 2, HEAD_DIM),
            vf_s.reshape(1, bs, ss, N_HEADS, V_DIM),
            tail_s.reshape(1, bs, CONV_K - 1, CONV_CH))
```

```python
import functools
import math

import jax
import jax.numpy as jnp
from jax import lax
from jax.experimental import pallas as pl
from jax.experimental.pallas import tpu as pltpu

F32 = jnp.float32
BF16 = jnp.bfloat16

D_MODEL = 2048
CHUNK = 64
HEAD_DIM = 64
V_DIM = 2 * HEAD_DIM
N_HEADS = 8
QK_WIDTH = N_HEADS * 2 * HEAD_DIM
ATT_WIDTH = N_HEADS * V_DIM
CONV_CH = 1024
CONV_K = 3
ROPE_THETA = 10000.0
N_GROUPS = 8
EXPERTS_PER_GROUP = 8
N_EXPERTS = N_GROUPS * EXPERTS_PER_GROUP
TOP_K = 2
D_EXPERT = D_MODEL // 4
NORM_EPS = 1e-6
SECTION = 1024
N_SECTIONS = 6

LANES = 128
IN_CH = 256
ATT_AHEAD = 2
ATT_UNROLL = 4
ATT_CB = 256
ATT_T = 512
OUT_TM = 256
MOE_BM = 256
DISPATCH_TM = 512
SLAB = D_MODEL // LANES
ROUTE_LANES = LANES
MASKED = -1e30
Q_SCALE = HEAD_DIM ** -0.5 * math.log2(math.e)
MIB = 1024 * 1024


def _store_token_slabs(ref, x):
    rows = x.shape[0]
    for c in range(SLAB):
        ref[pl.ds(c, rows, stride=SLAB), :] = x[:, c * LANES:(c + 1) * LANES]


def _load_token_slabs(ref, rows):
    return jnp.concatenate([ref[pl.ds(c, rows, stride=SLAB), :] for c in range(SLAB)], axis=1)


def _params(sem, vmem_mib):
    return pltpu.CompilerParams(dimension_semantics=sem, vmem_limit_bytes=vmem_mib * MIB)


def _inproj_kernel(x_ref, ng_ref, w_ref, qg_ref, kg_ref, gmat_ref, cos_ref, sin_ref,
                   prev_ref, cw_ref, cng_ref,
                   q_ref, kf_ref, kb_ref, vf_ref, vb_ref, yc_ref, tail_ref,
                   xn_s, gb_s, c_s, z_s, u_s, *, nseq, carry, qv_transposed):
    i = pl.program_id(0)
    j = pl.program_id(1)
    tm = x_ref.shape[0]
    seq = tm // nseq

    @pl.when(j == 0)
    def _():
        x = x_ref[...]
        inv = lax.rsqrt(jnp.mean(x * x, axis=-1, keepdims=True) + NORM_EPS)
        xn_s[...] = (x * inv * ng_ref[...]).astype(BF16)
        if carry:
            @pl.when(i == 0)
            def _():
                u_s[6:8, :] = prev_ref[0]

    def chunk_dot(c):
        return jnp.dot(xn_s[...], w_ref[:, c * IN_CH:(c + 1) * IN_CH], preferred_element_type=F32)

    def pipelined(epilogue):
        acc = chunk_dot(0)
        for c in range(SECTION // IN_CH):
            nxt = chunk_dot(c + 1) if c + 1 < SECTION // IN_CH else None
            epilogue(acc, slice(c * IN_CH, (c + 1) * IN_CH))
            acc = nxt

    def head_norm_rope(a, g):
        ms = jnp.dot((a * a).astype(BF16), gmat_ref[...], preferred_element_type=F32)
        y = a * lax.rsqrt(ms + NORM_EPS) * g
        reps = IN_CH // LANES
        cos = jnp.concatenate([cos_ref[...]] * reps, axis=1)
        sin = jnp.concatenate([sin_ref[...]] * reps, axis=1)
        lane = lax.broadcasted_iota(jnp.int32, y.shape, 1)
        first = (lane & (HEAD_DIM - 1)) < HEAD_DIM // 2
        partner = jnp.where(first, pltpu.roll(y, IN_CH - HEAD_DIM // 2, 1),
                            pltpu.roll(y, HEAD_DIM // 2, 1))
        return y * cos + partner * sin

    for section in range(N_SECTIONS):

        def epilogue(acc, cols, section=section):
            if section == 0:
                q = head_norm_rope(acc, qg_ref[:, cols]) * Q_SCALE
                if qv_transposed:
                    q_ref[0, cols, :] = q.T.astype(BF16)
                else:
                    q_ref[:, cols] = q.astype(BF16)
            elif section == 1:
                k = head_norm_rope(acc, kg_ref[:, cols])
                kf_ref[:, cols] = k
                kb_ref[:, cols] = k.astype(BF16)
            elif section == 2:
                vf_ref[:, cols] = acc
                if qv_transposed:
                    vb_ref[0, cols, :] = acc.T.astype(BF16)
                else:
                    vb_ref[:, cols] = acc.astype(BF16)
            elif section == 3:
                gb_s[:, cols] = acc
            elif section == 4:
                c_s[:, cols] = acc
            else:
                u = c_s[:, cols] * acc
                w0 = cw_ref[0:1, cols]
                w1 = cw_ref[1:2, cols]
                w2 = cw_ref[2:3, cols]
                for s in range(nseq):
                    rows = slice(s * seq, (s + 1) * seq)
                    if not carry:
                        u_s[6:8, cols] = prev_ref[s, :, cols]
                    u_s[8:8 + seq, cols] = u[rows]
                    conv = (w0 * u_s[6:6 + seq, cols] + w1 * u_s[7:7 + seq, cols]
                            + w2 * u_s[8:8 + seq, cols])
                    z_s[rows, cols] = gb_s[rows, cols] * conv
                    tail = u_s[seq + 6:seq + 8, cols]
                    tail_ref[s, :, cols] = tail
                    if carry:
                        u_s[6:8, cols] = tail

        @pl.when(j == section)
        def _(section=section, epilogue=epilogue):
            pipelined(epilogue)
            if section == N_SECTIONS - 1:
                z = z_s[...]
                inv = lax.rsqrt(jnp.mean(z * z, axis=-1, keepdims=True) + NORM_EPS)
                yc_ref[...] = (z * inv * cng_ref[...]).astype(BF16)


def _inproj(x2d, norm_g, w_in_bf, qg, kg, gmat, cos, sin, conv_prev, conv_w, conv_norm_g,
            *, tm, nseq, carry, qv_transposed):
    t = x2d.shape[0]
    ni = t // tm
    seq = tm // nseq
    row = lambda i, j: (i, 0)
    const = lambda i, j: (0, 0)
    if qv_transposed:
        qv_shape = jax.ShapeDtypeStruct((ni, QK_WIDTH, tm), BF16)
        qv_spec = pl.BlockSpec((1, QK_WIDTH, tm), lambda i, j: (i, 0, 0))
    else:
        qv_shape = jax.ShapeDtypeStruct((t, QK_WIDTH), BF16)
        qv_spec = pl.BlockSpec((tm, QK_WIDTH), row)
    if carry:
        prev_spec = pl.BlockSpec((1, CONV_K - 1, CONV_CH), lambda i, j: (0, 0, 0))
        tail_shape = jax.ShapeDtypeStruct((1, CONV_K - 1, CONV_CH), F32)
        tail_spec = pl.BlockSpec((1, CONV_K - 1, CONV_CH), lambda i, j: (0, 0, 0))
    else:
        prev_spec = pl.BlockSpec((nseq, CONV_K - 1, CONV_CH), lambda i, j: (i, 0, 0))
        tail_shape = jax.ShapeDtypeStruct((ni * nseq, CONV_K - 1, CONV_CH), F32)
        tail_spec = pl.BlockSpec((nseq, CONV_K - 1, CONV_CH), lambda i, j: (i, 0, 0))
    kern = functools.partial(_inproj_kernel, nseq=nseq, carry=carry, qv_transposed=qv_transposed)
    return pl.pallas_call(
        kern,
        grid=(ni, N_SECTIONS),
        in_specs=[
            pl.BlockSpec((tm, D_MODEL), row),
            pl.BlockSpec((1, D_MODEL), const),
            pl.BlockSpec((D_MODEL, SECTION), lambda i, j: (0, j)),
            pl.BlockSpec((1, SECTION), const),
            pl.BlockSpec((1, SECTION), const),
            pl.BlockSpec((IN_CH, IN_CH), const),
            pl.BlockSpec((tm, LANES), row),
            pl.BlockSpec((tm, LANES), row),
            prev_spec,
            pl.BlockSpec((CONV_K, CONV_CH), const),
            pl.BlockSpec((1, CONV_CH), const),
        ],
        out_specs=[
            qv_spec,
            pl.BlockSpec((tm, QK_WIDTH), row),
            pl.BlockSpec((tm, QK_WIDTH), row),
            pl.BlockSpec((tm, ATT_WIDTH), row),
            qv_spec,
            pl.BlockSpec((tm, CONV_CH), row),
            tail_spec,
        ],
        out_shape=[
            qv_shape,
            jax.ShapeDtypeStruct((t, QK_WIDTH), F32),
            jax.ShapeDtypeStruct((t, QK_WIDTH), BF16),
            jax.ShapeDtypeStruct((t, ATT_WIDTH), F32),
            qv_shape,
            jax.ShapeDtypeStruct((t, CONV_CH), BF16),
            tail_shape,
        ],
        scratch_shapes=[
            pltpu.VMEM((tm, D_MODEL), BF16),
            pltpu.VMEM((tm, CONV_CH), F32),
            pltpu.VMEM((tm, CONV_CH), F32),
            pltpu.VMEM((tm, CONV_CH), F32),
            pltpu.VMEM((seq + 8, CONV_CH), F32),
        ],
        compiler_params=_params(("arbitrary", "arbitrary"), 56),
        name="inproj_carry" if carry else "inproj_seqs",
    )(x2d, norm_g, w_in_bf, qg, kg, gmat, cos, sin, conv_prev, conv_w, conv_norm_g)


def _lambda_full(lamv_ref, lambda_init):
    lv = lamv_ref[...]
    a = jnp.sum(lv[0:1] * lv[1:2], axis=-1, keepdims=True)
    b = jnp.sum(lv[2:3] * lv[3:4], axis=-1, keepdims=True)
    return jnp.exp(a) - jnp.exp(b) + lambda_init


def _split_heads(q):
    lane = lax.broadcasted_iota(jnp.int32, q.shape, 1)
    zero = jnp.zeros_like(q)
    return jnp.where(lane < HEAD_DIM, q, zero), jnp.where(lane >= HEAD_DIM, q, zero)


def _diff_finish(acc, l, lam, g, lambda_init, rows):
    a = acc / l
    d = a[0:rows] - lam * a[rows:2 * rows]
    inv = lax.rsqrt(jnp.mean(d * d, axis=-1, keepdims=True) + NORM_EPS)
    return d * inv * g * (1.0 - lambda_init)


def _attn_prompt_kernel(qt_ref, k_ref, vt_ref, lamv_ref, g_ref, o_ref, qq_s, m_s, l_s, acc_s, s_s,
                        *, lambda_init):
    i = pl.program_id(1)
    t = ATT_T
    qt = qt_ref[0]
    feat = lax.broadcasted_iota(jnp.int32, qt.shape, 0)
    zero = jnp.zeros_like(qt)
    qq_s[:, 0:t] = jnp.where(feat < HEAD_DIM, qt, zero)
    qq_s[:, t:2 * t] = jnp.where(feat >= HEAD_DIM, qt, zero)
    m_s[...] = jnp.full(m_s.shape, MASKED, F32)
    l_s[...] = jnp.zeros(l_s.shape, F32)
    acc_s[...] = jnp.zeros(acc_s.shape, F32)

    nblk = 2 * t // ATT_CB

    def scores(kt, cb):
        k = k_ref[pl.ds(pl.multiple_of(kt * t, t), t), :]
        s_s[cb] = jnp.dot(k, qq_s[:, cb * ATT_CB:(cb + 1) * ATT_CB], preferred_element_type=F32)

    def step(kt, diagonal):
        vt = vt_ref[kt]
        for cb in range(nblk):
            cs = slice(cb * ATT_CB, (cb + 1) * ATT_CB)
            s = s_s[cb]
            if diagonal:
                key = lax.broadcasted_iota(jnp.int32, s.shape, 0)
                qry = lax.broadcasted_iota(jnp.int32, s.shape, 1) + (cb * ATT_CB) % t
                s = jnp.where((key // CHUNK) <= (qry // CHUNK), s, MASKED)
            m_prev = m_s[:, cs]
            m_new = jnp.maximum(m_prev, jnp.max(s, axis=0, keepdims=True))
            alpha = jnp.exp2(m_prev - m_new)
            p = jnp.exp2(s - m_new)
            l_s[:, cs] = alpha * l_s[:, cs] + jnp.sum(p, axis=0, keepdims=True)
            m_s[:, cs] = m_new
            ahead = cb + ATT_AHEAD
            if ahead < nblk:
                scores(kt, ahead)
            elif not diagonal:
                scores(kt + 1, ahead - nblk)
            acc_s[:, cs] = alpha * acc_s[:, cs] + jnp.dot(vt, p.astype(BF16),
                                                          preferred_element_type=F32)

    def group_body(g, carry):
        for u in range(ATT_UNROLL):
            step(ATT_UNROLL * g + u, False)
        return carry

    def single_body(kt, carry):
        step(kt, False)
        return carry

    for cb in range(ATT_AHEAD):
        scores(0, cb)
    n_grouped = i // ATT_UNROLL * ATT_UNROLL
    lax.fori_loop(0, i // ATT_UNROLL, group_body, 0)
    lax.fori_loop(n_grouped, i, single_body, 0)
    step(i, True)
    lam = _lambda_full(lamv_ref, lambda_init)
    a = acc_s[...] / l_s[...]
    d = a[:, 0:t] - lam * a[:, t:2 * t]
    inv = lax.rsqrt(jnp.mean(d * d, axis=0, keepdims=True) + NORM_EPS)
    y = d * inv * g_ref[...] * (1.0 - lambda_init)
    o_ref[...] = y.T.astype(BF16)


def _attn_prompt(qt_bf, k_bf, vt_bf, lamv, subln_g_col, lambda_init):
    nq = qt_bf.shape[0]
    t = nq * ATT_T
    kern = functools.partial(_attn_prompt_kernel, lambda_init=lambda_init)
    return pl.pallas_call(
        kern,
        grid=(N_HEADS, nq),
        in_specs=[
            pl.BlockSpec((1, V_DIM, ATT_T), lambda h, i: (i, h, 0)),
            pl.BlockSpec((t, V_DIM), lambda h, i: (0, h)),
            pl.BlockSpec((nq, V_DIM, ATT_T), lambda h, i: (0, h, 0)),
            pl.BlockSpec((4, HEAD_DIM), lambda h, i: (0, 0)),
            pl.BlockSpec((V_DIM, 1), lambda h, i: (0, 0)),
        ],
        out_specs=pl.BlockSpec((ATT_T, V_DIM), lambda h, i: (i, h)),
        out_shape=jax.ShapeDtypeStruct((t, ATT_WIDTH), BF16),
        scratch_shapes=[
            pltpu.VMEM((V_DIM, 2 * ATT_T), BF16),
            pltpu.VMEM((1, 2 * ATT_T), F32),
            pltpu.VMEM((1, 2 * ATT_T), F32),
            pltpu.VMEM((V_DIM, 2 * ATT_T), F32),
            pltpu.VMEM((2 * ATT_T // ATT_CB, ATT_T, ATT_CB), F32),
        ],
        compiler_params=_params(("arbitrary", "arbitrary"), 48),
        name="attn_prompt",
    )(qt_bf, k_bf, vt_bf, lamv, subln_g_col)


def _attn_sample_kernel(q_ref, kc_ref, vc_ref, kn_ref, vn_ref, lamv_ref, g_ref, o_ref,
                        *, lambda_init, past):
    rows = q_ref.shape[0]
    lam = _lambda_full(lamv_ref, lambda_init)
    contract_last = (((1,), (1,)), ((), ()))
    for h in range(N_HEADS):
        hs = slice(h * V_DIM, (h + 1) * V_DIM)
        q1, q2 = _split_heads(q_ref[:, hs])
        qq = jnp.concatenate([q1, q2], axis=0)
        s_c = jnp.dot(qq, kc_ref[0, hs, :].astype(BF16), preferred_element_type=F32)
        s_n = lax.dot_general(qq, kn_ref[:, hs], contract_last, preferred_element_type=F32)
        r = lax.broadcasted_iota(jnp.int32, s_n.shape, 0)
        c = lax.broadcasted_iota(jnp.int32, s_n.shape, 1)
        s_n = jnp.where(((past + c) // CHUNK) <= ((past + r % rows) // CHUNK), s_n, MASKED)
        m = jnp.maximum(jnp.max(s_c, axis=-1, keepdims=True), jnp.max(s_n, axis=-1, keepdims=True))
        p_c = jnp.exp2(s_c - m)
        p_n = jnp.exp2(s_n - m)
        l = jnp.sum(p_c, axis=-1, keepdims=True) + jnp.sum(p_n, axis=-1, keepdims=True)
        v_c = vc_ref[0, pl.ds(h, past, stride=N_HEADS), :].astype(BF16)
        acc = (jnp.dot(p_c.astype(BF16), v_c, preferred_element_type=F32)
               + jnp.dot(p_n.astype(BF16), vn_ref[:, hs], preferred_element_type=F32))
        o_ref[:, hs] = _diff_finish(acc, l, lam, g_ref[...], lambda_init, rows).astype(BF16)


def _attn_sample(q_bf, kt_cache, v_cache, k_bf, v_bf, lamv, subln_g, lambda_init):
    nb, _, past = kt_cache.shape
    rows = q_bf.shape[0] // nb
    kern = functools.partial(_attn_sample_kernel, lambda_init=lambda_init, past=past)
    new_spec = pl.BlockSpec((rows, ATT_WIDTH), lambda b: (b, 0))
    return pl.pallas_call(
        kern,
        grid=(nb,),
        in_specs=[
            new_spec,
            pl.BlockSpec((1, QK_WIDTH, past), lambda b: (b, 0, 0)),
            pl.BlockSpec((1, past * N_HEADS, V_DIM), lambda b: (b, 0, 0)),
            new_spec, new_spec,
            pl.BlockSpec((4, HEAD_DIM), lambda b: (0, 0)),
            pl.BlockSpec((1, V_DIM), lambda b: (0, 0)),
        ],
        out_specs=new_spec,
        out_shape=jax.ShapeDtypeStruct(q_bf.shape, BF16),
        compiler_params=_params(("arbitrary",), 48),
        name="attn_sample",
    )(q_bf, kt_cache, v_cache, k_bf, v_bf, lamv, subln_g)


def _outproj_kernel(att_p_ref, yc_p_ref, x_p_ref, att_s_ref, yc_s_ref, x_s_ref,
                    wt_ref, wb_ref, ng_ref, rw_ref, rb_ref, x1_ref, xf_ref, route_ref, *, n_prompt_tiles):
    i = pl.program_id(0)
    shared = (wt_ref, wb_ref, ng_ref, rw_ref, rb_ref, x1_ref, xf_ref, route_ref)

    @pl.when(i < n_prompt_tiles)
    def _():
        _outproj_tile(att_p_ref, yc_p_ref, x_p_ref, *shared)

    @pl.when(i >= n_prompt_tiles)
    def _():
        _outproj_tile(att_s_ref, yc_s_ref, x_s_ref, *shared)


def _outproj_tile(att_ref, yc_ref, x_ref, wt_ref, wb_ref, ng_ref, rw_ref, rb_ref,
                  x1_ref, xf_ref, route_ref):
    o = (jnp.dot(att_ref[...], wt_ref[...], preferred_element_type=F32)
         + jnp.dot(yc_ref[...], wb_ref[...], preferred_element_type=F32))
    x1 = x_ref[...] + o
    x1_ref[...] = x1
    xf = x1 * lax.rsqrt(jnp.mean(x1 * x1, axis=-1, keepdims=True) + NORM_EPS) * ng_ref[...]
    _store_token_slabs(xf_ref, xf)
    logits =jnp.dot(xf.astype(BF16), rw_ref[...], preferred_element_type=F32) + rb_ref[...]

    lane = lax.broadcasted_iota(jnp.int32, logits.shape, 1)
    neg = -jnp.inf
    is_group = lane < N_GROUPS
    gl = jnp.where(is_group, logits, neg)
    gmax = jnp.max(gl, axis=-1, keepdims=True)
    grp = jnp.min(jnp.where(gl == gmax, lane, ROUTE_LANES), axis=-1, keepdims=True)
    gsum = jnp.sum(jnp.where(is_group, jnp.exp(gl - gmax), 0.0), axis=-1, keepdims=True)
    g_w = 1.0 / gsum
    e_lane = lane - N_GROUPS
    in_grp = (e_lane >= 0) & (e_lane < N_EXPERTS) & ((e_lane // EXPERTS_PER_GROUP) == grp)
    el = jnp.where(in_grp, logits, neg)
    t1 = jnp.max(el, axis=-1, keepdims=True)
    i1 = jnp.min(jnp.where(el == t1, lane, ROUTE_LANES), axis=-1, keepdims=True)
    el2 = jnp.where(lane == i1, neg, el)
    t2 = jnp.max(el2, axis=-1, keepdims=True)
    i2 = jnp.min(jnp.where(el2 == t2, lane, ROUTE_LANES), axis=-1, keepdims=True)
    r21 = jnp.exp(t2 - t1)
    w0 = g_w / (1.0 + r21)
    w1 = g_w * r21 / (1.0 + r21)
    e0 = (i1 - N_GROUPS).astype(F32)
    e1 = (i2 - N_GROUPS).astype(F32)
    route_ref[...] = jnp.where(lane == 0, e0, jnp.where(lane == 1, e1,
                               jnp.where(lane == 2, w0, jnp.where(lane == 3, w1, 0.0))))


def _outproj(att_p, yc_p, x_p, att_s, yc_s, x_s, w_top, w_bot, norm_g, rw, rb):
    tm = OUT_TM
    n_p = x_p.shape[0] // tm
    n_s = x_s.shape[0] // tm
    t_all = x_p.shape[0] + x_s.shape[0]
    prow = lambda i: (jnp.minimum(i, n_p - 1), 0)
    srow = lambda i: (jnp.maximum(i - n_p, 0), 0)
    row = lambda i: (i, 0)
    const = lambda i: (0, 0)
    kern = functools.partial(_outproj_kernel, n_prompt_tiles=n_p)
    return pl.pallas_call(
        kern,
        grid=(n_p + n_s,),
        in_specs=[
            pl.BlockSpec((tm, ATT_WIDTH), prow),
            pl.BlockSpec((tm, CONV_CH), prow),
            pl.BlockSpec((tm, D_MODEL), prow),
            pl.BlockSpec((tm, ATT_WIDTH), srow),
            pl.BlockSpec((tm, CONV_CH), srow),
            pl.BlockSpec((tm, D_MODEL), srow),
            pl.BlockSpec((ATT_WIDTH, D_MODEL), const),
            pl.BlockSpec((CONV_CH, D_MODEL), const),
            pl.BlockSpec((1, D_MODEL), const),
            pl.BlockSpec((D_MODEL, ROUTE_LANES), const),
            pl.BlockSpec((1, ROUTE_LANES), const),
        ],
        out_specs=[
            pl.BlockSpec((tm, D_MODEL), row),
            pl.BlockSpec((tm * SLAB, LANES), row),
            pl.BlockSpec((tm, ROUTE_LANES), row),
        ],
        out_shape=[
            jax.ShapeDtypeStruct((t_all, D_MODEL), F32),
            jax.ShapeDtypeStruct((t_all * SLAB, LANES), F32),
            jax.ShapeDtypeStruct((t_all, ROUTE_LANES), F32),
        ],
        compiler_params=_params(("arbitrary",), 52),
        name="outproj_router",
    )(att_p, yc_p, x_p, att_s, yc_s, x_s, w_top, w_bot, norm_g, rw, rb)


def _dispatch_copy(xf_ref, xs_hbm, sem, src_row, dst_row):
    return pltpu.make_async_copy(xf_ref.at[pl.ds(src_row * SLAB, SLAB)],
                                 xs_hbm.at[pl.ds(dst_row * SLAB, SLAB)], sem)


def _dispatch_kernel(dest_ref, xf_ref, xs_init_hbm, xs_hbm, sem):
    del xs_init_hbm
    tm = DISPATCH_TM

    def issue(t, carry):
        for k in range(TOP_K):
            _dispatch_copy(xf_ref, xs_hbm, sem, t, dest_ref[0, 0, TOP_K * t + k]).start()
        return carry

    def drain(t, carry):
        for k in range(TOP_K):
            _dispatch_copy(xf_ref, xs_hbm, sem, t, dest_ref[0, 0, TOP_K * t + k]).wait()
        return carry

    lax.fori_loop(0, tm, issue, 0)
    lax.fori_loop(0, tm, drain, 0)


def _dispatch(dest, xf, cap):
    t_all = xf.shape[0] // SLAB
    tm = DISPATCH_TM
    dest3 = dest.reshape(t_all // tm, 1, TOP_K * tm)
    xs_init = jnp.zeros((cap * SLAB, LANES), F32)
    return pl.pallas_call(
        _dispatch_kernel,
        grid=(t_all // tm,),
        in_specs=[
            pl.BlockSpec((1, 1, TOP_K * tm), lambda i: (i, 0, 0), memory_space=pltpu.SMEM),
            pl.BlockSpec((tm * SLAB, LANES), lambda i: (i, 0)),
            pl.BlockSpec(memory_space=pl.ANY),
        ],
        out_specs=pl.BlockSpec(memory_space=pl.ANY),
        out_shape=jax.ShapeDtypeStruct((cap * SLAB, LANES), F32),
        scratch_shapes=[pltpu.SemaphoreType.DMA(())],
        input_output_aliases={2: 0},
        compiler_params=_params(("arbitrary",), 24),
        name="moe_dispatch",
    )(dest3, xf, xs_init)


def _experts_kernel(blk_e_ref, first_ref, nused_ref, x_ref, wg_ref, wu_ref, wd_ref, o_ref,
                    wgu_s, wd_s):
    del blk_e_ref
    i = pl.program_id(0)

    @pl.when(i < nused_ref[0])
    def _():
        @pl.when(first_ref[i] == 1)
        def _():
            wgu_s[:, 0:D_EXPERT] = wg_ref[0].astype(BF16)
            wgu_s[:, D_EXPERT:2 * D_EXPERT] = wu_ref[0].astype(BF16)
            wd_s[...] = wd_ref[0].astype(BF16)

        x = _load_token_slabs(x_ref, MOE_BM).astype(BF16)
        gu = jnp.dot(x, wgu_s[...], preferred_element_type=F32)
        g = gu[:, 0:D_EXPERT]
        u = gu[:, D_EXPERT:2 * D_EXPERT]
        h = g / (1.0 + jnp.exp(-g)) * u
        _store_token_slabs(o_ref, jnp.dot(h.astype(BF16), wd_s[...], preferred_element_type=F32))

    @pl.when(i >= nused_ref[0])
    def _():
        o_ref[...] = jnp.zeros(o_ref.shape, F32)


def _experts(blk_e, first, nused, xs, w_gate, w_up, w_down):
    cap = xs.shape[0] // SLAB
    bm = MOE_BM
    rows = lambda i, be, fi, nu: (jnp.minimum(i, nu[0] - 1), 0)
    wsel = lambda i, be, fi, nu: (be[i], 0, 0)
    grid_spec = pltpu.PrefetchScalarGridSpec(
        num_scalar_prefetch=3,
        grid=(cap // bm,),
        in_specs=[
            pl.BlockSpec((bm * SLAB, LANES), rows),
            pl.BlockSpec((1, D_MODEL, D_EXPERT), wsel),
            pl.BlockSpec((1, D_MODEL, D_EXPERT), wsel),
            pl.BlockSpec((1, D_EXPERT, D_MODEL), wsel),
        ],
        out_specs=pl.BlockSpec((bm * SLAB, LANES), lambda i, be, fi, nu: (i, 0)),
        scratch_shapes=[
            pltpu.VMEM((D_MODEL, 2 * D_EXPERT), BF16),
            pltpu.VMEM((D_EXPERT, D_MODEL), BF16),
        ],
    )
    return pl.pallas_call(
        _experts_kernel,
        grid_spec=grid_spec,
        out_shape=jax.ShapeDtypeStruct((cap * SLAB, LANES), F32),
        compiler_params=_params(("arbitrary",), 52),
        name="moe_experts",
    )(blk_e, first, nused, xs, w_gate, w_up, w_down)


def _combine_copy(outs_hbm, g_refs, sem, src_row, k, t):
    return pltpu.make_async_copy(outs_hbm.at[pl.ds(src_row * SLAB, SLAB)],
                                 g_refs[k].at[pl.ds(t * SLAB, SLAB)], sem)


def _combine_kernel(dest_ref, x1_ref, route_ref, outs_hbm, y_ref, g0_s, g1_s, sem):
    tm = x1_ref.shape[0]
    g_refs = (g0_s, g1_s)

    def issue(t, carry):
        for k in range(TOP_K):
            _combine_copy(outs_hbm, g_refs, sem, dest_ref[0, 0, TOP_K * t + k], k, t).start()
        return carry

    def drain(t, carry):
        for k in range(TOP_K):
            _combine_copy(outs_hbm, g_refs, sem, dest_ref[0, 0, TOP_K * t + k], k, t).wait()
        return carry

    lax.fori_loop(0, tm, issue, 0)
    lax.fori_loop(0, tm, drain, 0)
    r = route_ref[...]
    w0 = jnp.broadcast_to(r[:, 2:3], (tm, LANES))
    w1 = jnp.broadcast_to(r[:, 3:4], (tm, LANES))
    for c in range(SLAB):
        cs = slice(c * LANES, (c + 1) * LANES)
        y_ref[:, cs] = (x1_ref[:, cs] + w0 * g0_s[pl.ds(c, tm, stride=SLAB), :]
                        + w1 * g1_s[pl.ds(c, tm, stride=SLAB), :])


def _combine(dest, x1, route, outs, row_off, t):
    tm = OUT_TM
    t_all = x1.shape[0]
    off = row_off // tm
    dest3 = dest.reshape(t_all // tm, 1, TOP_K * tm)
    return pl.pallas_call(
        _combine_kernel,
        grid=(t // tm,),
        in_specs=[
            pl.BlockSpec((1, 1, TOP_K * tm), lambda i: (i + off, 0, 0), memory_space=pltpu.SMEM),
            pl.BlockSpec((tm, D_MODEL), lambda i: (i + off, 0)),
            pl.BlockSpec((tm, ROUTE_LANES), lambda i: (i + off, 0)),
            pl.BlockSpec(memory_space=pl.ANY),
        ],
        out_specs=pl.BlockSpec((tm, D_MODEL), lambda i: (i, 0)),
        out_shape=jax.ShapeDtypeStruct((t, D_MODEL), F32),
        scratch_shapes=[pltpu.VMEM((tm * SLAB, LANES), F32), pltpu.VMEM((tm * SLAB, LANES), F32),
                        pltpu.SemaphoreType.DMA(())],
        compiler_params=_params(("arbitrary",), 32),
        name="moe_combine",
    )(dest3, x1, route, outs)


def _routing_tables(route, n_blocks):
    bm = MOE_BM
    flat_e = route[:, 0:TOP_K].astype(jnp.int32).reshape(-1)
    onehot = (flat_e[:, None] == jnp.arange(N_EXPERTS, dtype=jnp.int32)[None, :]).astype(jnp.int32)
    csum = jnp.cumsum(onehot, axis=0)
    rank = jnp.sum(csum * onehot, axis=1) - 1
    counts = csum[-1]
    padded = (counts + bm - 1) // bm * bm
    pad_end = jnp.cumsum(padded)
    pad_start = pad_end - padded
    dest = (pad_start[flat_e] + rank).astype(jnp.int32)
    nused = (pad_end[-1] // bm).astype(jnp.int32)
    blk = jnp.arange(n_blocks, dtype=jnp.int32)
    blk_e = jnp.searchsorted(pad_end, jnp.minimum(blk, nused - 1) * bm, side='right').astype(jnp.int32)
    blk_e = jnp.minimum(blk_e, N_EXPERTS - 1)
    first = jnp.concatenate([jnp.ones((1,), jnp.int32),
                             (blk_e[1:] != blk_e[:-1]).astype(jnp.int32)])
    return dest, blk_e, first, nused.reshape(1)


def _rope_tables(pos):
    half = HEAD_DIM // 2
    inv = ROPE_THETA ** (-jnp.arange(half, dtype=F32) / half)
    ang = pos.astype(F32)[:, None] * inv[None, :]
    cos = jnp.cos(ang)
    sin = jnp.sin(ang)
    cos_h = jnp.concatenate([cos, cos], axis=-1)
    sin_h = jnp.concatenate([-sin, sin], axis=-1)
    reps = LANES // HEAD_DIM
    return jnp.tile(cos_h, (1, reps)), jnp.tile(sin_h, (1, reps))


def kernel(x_prompt, x_sample, cache_k, cache_v, state_conv, norm_mix_g, w_in, q_norm_g, k_norm_g, lambda_q1, lambda_k1, lambda_q2, lambda_k2, subln_g, conv_w, conv_norm_g, w_out, norm_ffn_g, router_group_w, router_group_b, router_expert_w, router_expert_b, expert_w_gate, expert_w_up, expert_w_down):
    assert w_in.shape[0] == 1, "single-layer step"
    bp, sp, _ = x_prompt.shape
    bs, ss, _ = x_sample.shape
    past = cache_k.shape[2]
    assert bp == 1 and sp % ATT_T == 0
    tp = bp * sp
    ts = bs * ss
    t_all = tp + ts
    lambda_init = 0.8 - 0.6 * math.exp(-0.3 * 0)

    w_in_bf = w_in[0].astype(BF16)
    w_out_bf = w_out[0].astype(BF16)
    w_top, w_bot = w_out_bf[:ATT_WIDTH], w_out_bf[ATT_WIDTH:]
    ng = norm_mix_g[0].reshape(1, D_MODEL)
    qg = jnp.tile(q_norm_g[0], QK_WIDTH // HEAD_DIM).reshape(1, QK_WIDTH)
    kg = jnp.tile(k_norm_g[0], QK_WIDTH // HEAD_DIM).reshape(1, QK_WIDTH)
    head_of = jnp.arange(IN_CH, dtype=jnp.int32) // HEAD_DIM
    gmat = jnp.where(head_of[:, None] == head_of[None, :], 1.0 / HEAD_DIM, 0.0).astype(BF16)
    lamv = jnp.stack([lambda_q1[0], lambda_k1[0], lambda_q2[0], lambda_k2[0]]).astype(F32)
    sg = subln_g[0].reshape(1, V_DIM)
    cw = conv_w[0]
    cng = conv_norm_g[0].reshape(1, CONV_CH)
    cos_p, sin_p = _rope_tables(jnp.arange(sp, dtype=jnp.int32))
    cos_s, sin_s = _rope_tables(jnp.tile(past + jnp.arange(ss, dtype=jnp.int32), bs))

    zero_conv = jnp.zeros((1, CONV_K - 1, CONV_CH), F32)
    qt_p, kf_p, kb_p, vf_p, vt_p, yc_p, tail_p = _inproj(
        x_prompt.reshape(tp, D_MODEL), ng, w_in_bf, qg, kg, gmat, cos_p, sin_p, zero_conv, cw, cng,
        tm=ATT_T, nseq=1, carry=True, qv_transposed=True)
    att_p = _attn_prompt(qt_p, kb_p, vt_p, lamv, subln_g[0].reshape(V_DIM, 1), lambda_init)

    seqs_per_tile = OUT_TM // ss
    q_s, kf_s, kb_s, vf_s, vb_s, yc_s, tail_s = _inproj(
        x_sample.reshape(ts, D_MODEL), ng, w_in_bf, qg, kg, gmat, cos_s, sin_s, state_conv[0], cw, cng,
        tm=OUT_TM, nseq=seqs_per_tile, carry=False, qv_transposed=False)
    kt_cache = jnp.transpose(cache_k[0], (0, 2, 3, 4, 1)).reshape(bs, QK_WIDTH, past)
    att_s = _attn_sample(q_s, kt_cache, cache_v[0].reshape(bs, past * N_HEADS, V_DIM),
                         kb_s, vb_s, lamv, sg, lambda_init)

    rw = jnp.zeros((D_MODEL, ROUTE_LANES), F32)
    rw = rw.at[:, 0:N_GROUPS].set(router_group_w[0]).at[:, N_GROUPS:N_GROUPS + N_EXPERTS].set(router_expert_w[0])
    rb = jnp.zeros((1, ROUTE_LANES), F32)
    rb = rb.at[0, 0:N_GROUPS].set(router_group_b[0]).at[0, N_GROUPS:N_GROUPS + N_EXPERTS].set(router_expert_b[0])
    nf = norm_ffn_g[0].reshape(1, D_MODEL)
    rw_bf = rw.astype(BF16)
    x1, xf, route = _outproj(att_p, yc_p, x_prompt.reshape(tp, D_MODEL),
                             att_s, yc_s, x_sample.reshape(ts, D_MODEL), w_top, w_bot, nf, rw_bf, rb)

    n = t_all * TOP_K
    n_blocks = n // MOE_BM + N_EXPERTS
    dest, blk_e, first, nused = _routing_tables(route, n_blocks)
    xs = _dispatch(dest, xf, n_blocks * MOE_BM)
    outs = _experts(blk_e, first, nused, xs, expert_w_gate[0], expert_w_up[0], expert_w_down[0])
    y_p = _combine(dest, x1, route, outs, 0, tp)
    y_s = _combine(dest, x1, route, outs, tp, ts)

    return (y_p.reshape(bp, sp, D_MODEL),
            y_s.reshape(bs, ss, D_MODEL),
            kf_p.reshape(1, bp, sp, N_HEADS, 2, HEAD_DIM),
            vf_p.reshape(1, bp, sp, N_HEADS, V_DIM),
            tail_p.reshape(1, bp, CONV_K - 1, CONV_CH),
            kf_s.reshape(1, bs, ss, N_HEADS, 2, HEAD_DIM),
            vf_s.reshape(1, bs, ss, N_HEADS, V_DIM),
            tail_s.reshape(1, bs, CONV_K - 1, CONV_CH))
```

```python
import functools
import math

import jax
import jax.numpy as jnp
from jax import lax
from jax.experimental import pallas as pl
from jax.experimental.pallas import tpu as pltpu

F32 = jnp.float32
BF16 = jnp.bfloat16

D_MODEL = 2048
CHUNK = 64
HEAD_DIM = 64
V_DIM = 2 * HEAD_DIM
N_HEADS = 8
QK_WIDTH = N_HEADS * 2 * HEAD_DIM
ATT_WIDTH = N_HEADS * V_DIM
CONV_CH = 1024
CONV_K = 3
ROPE_THETA = 10000.0
N_GROUPS = 8
EXPERTS_PER_GROUP = 8
N_EXPERTS = N_GROUPS * EXPERTS_PER_GROUP
TOP_K = 2
D_EXPERT = D_MODEL // 4
NORM_EPS = 1e-6
SECTION = 1024
N_SECTIONS = 6

LANES = 128
IN_CH = 256
ATT_AHEAD = 2
ATT_UNROLL = 4
ATT_CB = 256
ATT_T = 512
OUT_TM = 256
MOE_BM = 256
DISPATCH_TM = 512
SLAB = D_MODEL // LANES
ROUTE_LANES = LANES
MASKED = -1e30
Q_SCALE = HEAD_DIM ** -0.5 * math.log2(math.e)
MIB = 1024 * 1024


def _store_token_slabs(ref, x):
    rows = x.shape[0]
    for c in range(SLAB):
        ref[pl.ds(c, rows, stride=SLAB), :] = x[:, c * LANES:(c + 1) * LANES]


def _load_token_slabs(ref, rows):
    return jnp.concatenate([ref[pl.ds(c, rows, stride=SLAB), :] for c in range(SLAB)], axis=1)


def _params(sem, vmem_mib):
    return pltpu.CompilerParams(dimension_semantics=sem, vmem_limit_bytes=vmem_mib * MIB)


def _inproj_kernel(x_ref, ng_ref, w_ref, qg_ref, kg_ref, gmat_ref, cos_ref, sin_ref,
                   prev_ref, cw_ref, cng_ref,
                   q_ref, kf_ref, kb_ref, vf_ref, vb_ref, yc_ref, tail_ref,
                   xn_s, gb_s, c_s, z_s, u_s, *, nseq, carry, qv_transposed):
    i = pl.program_id(0)
    j = pl.program_id(1)
    tm = x_ref.shape[0]
    seq = tm // nseq

    @pl.when(j == 0)
    def _():
        x = x_ref[...]
        inv = lax.rsqrt(jnp.mean(x * x, axis=-1, keepdims=True) + NORM_EPS)
        xn_s[...] = (x * inv * ng_ref[...]).astype(BF16)
        if carry:
            @pl.when(i == 0)
            def _():
                u_s[6:8, :] = prev_ref[0]

    def chunk_dot(c):
        return jnp.dot(xn_s[...], w_ref[:, c * IN_CH:(c + 1) * IN_CH], preferred_element_type=F32)

    def pipelined(epilogue):
        acc = chunk_dot(0)
        for c in range(SECTION // IN_CH):
            nxt = chunk_dot(c + 1) if c + 1 < SECTION // IN_CH else None
            epilogue(acc, slice(c * IN_CH, (c + 1) * IN_CH))
            acc = nxt

    def head_norm_rope(a, g):
        ms = jnp.dot((a * a).astype(BF16), gmat_ref[...], preferred_element_type=F32)
        y = a * lax.rsqrt(ms + NORM_EPS) * g
        reps = IN_CH // LANES
        cos = jnp.concatenate([cos_ref[...]] * reps, axis=1)
        sin = jnp.concatenate([sin_ref[...]] * reps, axis=1)
        lane = lax.broadcasted_iota(jnp.int32, y.shape, 1)
        first = (lane & (HEAD_DIM - 1)) < HEAD_DIM // 2
        partner = jnp.where(first, pltpu.roll(y, IN_CH - HEAD_DIM // 2, 1),
                            pltpu.roll(y, HEAD_DIM // 2, 1))
        return y * cos + partner * sin

    for section in range(N_SECTIONS):

        def epilogue(acc, cols, section=section):
            if section == 0:
                q = head_norm_rope(acc, qg_ref[:, cols]) * Q_SCALE
                if qv_transposed:
                    q_ref[0, cols, :] = q.T.astype(BF16)
                else:
                    q_ref[:, cols] = q.astype(BF16)
            elif section == 1:
                k = head_norm_rope(acc, kg_ref[:, cols])
                kf_ref[:, cols] = k
                kb_ref[:, cols] = k.astype(BF16)
            elif section == 2:
                vf_ref[:, cols] = acc
                if qv_transposed:
                    vb_ref[0, cols, :] = acc.T.astype(BF16)
                else:
                    vb_ref[:, cols] = acc.astype(BF16)
            elif section == 3:
                gb_s[:, cols] = acc
            elif section == 4:
                c_s[:, cols] = acc
            else:
                u = c_s[:, cols] * acc
                w0 = cw_ref[0:1, cols]
                w1 = cw_ref[1:2, cols]
                w2 = cw_ref[2:3, cols]
                for s in range(nseq):
                    rows = slice(s * seq, (s + 1) * seq)
                    if not carry:
                        u_s[6:8, cols] = prev_ref[s, :, cols]
                    u_s[8:8 + seq, cols] = u[rows]
                    conv = (w0 * u_s[6:6 + seq, cols] + w1 * u_s[7:7 + seq, cols]
                            + w2 * u_s[8:8 + seq, cols])
                    z_s[rows, cols] = gb_s[rows, cols] * conv
                    tail = u_s[seq + 6:seq + 8, cols]
                    tail_ref[s, :, cols] = tail
                    if carry:
                        u_s[6:8, cols] = tail

        @pl.when(j == section)
        def _(section=section, epilogue=epilogue):
            pipelined(epilogue)
            if section == N_SECTIONS - 1:
                z = z_s[...]
                inv = lax.rsqrt(jnp.mean(z * z, axis=-1, keepdims=True) + NORM_EPS)
                yc_ref[...] = (z * inv * cng_ref[...]).astype(BF16)


def _inproj(x2d, norm_g, w_in_bf, qg, kg, gmat, cos, sin, conv_prev, conv_w, conv_norm_g,
            *, tm, nseq, carry, qv_transposed):
    t = x2d.shape[0]
    ni = t // tm
    seq = tm // nseq
    row = lambda i, j: (i, 0)
    const = lambda i, j: (0, 0)
    if qv_transposed:
        qv_shape = jax.ShapeDtypeStruct((ni, QK_WIDTH, tm), BF16)
        qv_spec = pl.BlockSpec((1, QK_WIDTH, tm), lambda i, j: (i, 0, 0))
    else:
        qv_shape = jax.ShapeDtypeStruct((t, QK_WIDTH), BF16)
        qv_spec = pl.BlockSpec((tm, QK_WIDTH), row)
    if carry:
        prev_spec = pl.BlockSpec((1, CONV_K - 1, CONV_CH), lambda i, j: (0, 0, 0))
        tail_shape = jax.ShapeDtypeStruct((1, CONV_K - 1, CONV_CH), F32)
        tail_spec = pl.BlockSpec((1, CONV_K - 1, CONV_CH), lambda i, j: (0, 0, 0))
    else:
        prev_spec = pl.BlockSpec((nseq, CONV_K - 1, CONV_CH), lambda i, j: (i, 0, 0))
        tail_shape = jax.ShapeDtypeStruct((ni * nseq, CONV_K - 1, CONV_CH), F32)
        tail_spec = pl.BlockSpec((nseq, CONV_K - 1, CONV_CH), lambda i, j: (i, 0, 0))
    kern = functools.partial(_inproj_kernel, nseq=nseq, carry=carry, qv_transposed=qv_transposed)
    return pl.pallas_call(
        kern,
        grid=(ni, N_SECTIONS),
        in_specs=[
            pl.BlockSpec((tm, D_MODEL), row),
            pl.BlockSpec((1, D_MODEL), const),
            pl.BlockSpec((D_MODEL, SECTION), lambda i, j: (0, j)),
            pl.BlockSpec((1, SECTION), const),
            pl.BlockSpec((1, SECTION), const),
            pl.BlockSpec((IN_CH, IN_CH), const),
            pl.BlockSpec((tm, LANES), row),
            pl.BlockSpec((tm, LANES), row),
            prev_spec,
            pl.BlockSpec((CONV_K, CONV_CH), const),
            pl.BlockSpec((1, CONV_CH), const),
        ],
        out_specs=[
            qv_spec,
            pl.BlockSpec((tm, QK_WIDTH), row),
            pl.BlockSpec((tm, QK_WIDTH), row),
            pl.BlockSpec((tm, ATT_WIDTH), row),
            qv_spec,
            pl.BlockSpec((tm, CONV_CH), row),
            tail_spec,
        ],
        out_shape=[
            qv_shape,
            jax.ShapeDtypeStruct((t, QK_WIDTH), F32),
            jax.ShapeDtypeStruct((t, QK_WIDTH), BF16),
            jax.ShapeDtypeStruct((t, ATT_WIDTH), F32),
            qv_shape,
            jax.ShapeDtypeStruct((t, CONV_CH), BF16),
            tail_shape,
        ],
        scratch_shapes=[
            pltpu.VMEM((tm, D_MODEL), BF16),
            pltpu.VMEM((tm, CONV_CH), F32),
            pltpu.VMEM((tm, CONV_CH), F32),
            pltpu.VMEM((tm, CONV_CH), F32),
            pltpu.VMEM((seq + 8, CONV_CH), F32),
        ],
        compiler_params=_params(("arbitrary", "arbitrary"), 56),
        name="inproj_carry" if carry else "inproj_seqs",
    )(x2d, norm_g, w_in_bf, qg, kg, gmat, cos, sin, conv_prev, conv_w, conv_norm_g)


def _lambda_full(lamv_ref, lambda_init):
    lv = lamv_ref[...]
    a = jnp.sum(lv[0:1] * lv[1:2], axis=-1, keepdims=True)
    b = jnp.sum(lv[2:3] * lv[3:4], axis=-1, keepdims=True)
    return jnp.exp(a) - jnp.exp(b) + lambda_init


def _split_heads(q):
    lane = lax.broadcasted_iota(jnp.int32, q.shape, 1)
    zero = jnp.zeros_like(q)
    return jnp.where(lane < HEAD_DIM, q, zero), jnp.where(lane >= HEAD_DIM, q, zero)


def _diff_finish(acc, l, lam, g, lambda_init, rows):
    a = acc / l
    d = a[0:rows] - lam * a[rows:2 * rows]
    inv = lax.rsqrt(jnp.mean(d * d, axis=-1, keepdims=True) + NORM_EPS)
    return d * inv * g * (1.0 - lambda_init)


def _attn_prompt_kernel(qt_ref, k_ref, vt_ref, lamv_ref, g_ref, o_ref, qq_s, m_s, l_s, acc_s, s_s,
                        *, lambda_init):
    i = pl.program_id(1)
    t = ATT_T
    qt = qt_ref[0]
    feat = lax.broadcasted_iota(jnp.int32, qt.shape, 0)
    zero = jnp.zeros_like(qt)
    qq_s[:, 0:t] = jnp.where(feat < HEAD_DIM, qt, zero)
    qq_s[:, t:2 * t] = jnp.where(feat >= HEAD_DIM, qt, zero)
    m_s[...] = jnp.full(m_s.shape, MASKED, F32)
    l_s[...] = jnp.zeros(l_s.shape, F32)
    acc_s[...] = jnp.zeros(acc_s.shape, F32)

    nblk = 2 * t // ATT_CB

    def scores(kt, cb):
        k = k_ref[pl.ds(pl.multiple_of(kt * t, t), t), :]
        s_s[cb] = jnp.dot(k, qq_s[:, cb * ATT_CB:(cb + 1) * ATT_CB], preferred_element_type=F32)

    def step(kt, diagonal):
        vt = vt_ref[kt]
        for cb in range(nblk):
            cs = slice(cb * ATT_CB, (cb + 1) * ATT_CB)
            s = s_s[cb]
            if diagonal:
                key = lax.broadcasted_iota(jnp.int32, s.shape, 0)
                qry = lax.broadcasted_iota(jnp.int32, s.shape, 1) + (cb * ATT_CB) % t
                s = jnp.where((key // CHUNK) <= (qry // CHUNK), s, MASKED)
            m_prev = m_s[:, cs]
            m_new = jnp.maximum(m_prev, jnp.max(s, axis=0, keepdims=True))
            alpha = jnp.exp2(m_prev - m_new)
            p = jnp.exp2(s - m_new)
            l_s[:, cs] = alpha * l_s[:, cs] + jnp.sum(p, axis=0, keepdims=True)
            m_s[:, cs] = m_new
            ahead = cb + ATT_AHEAD
            if ahead < nblk:
                scores(kt, ahead)
            elif not diagonal:
                scores(kt + 1, ahead - nblk)
            acc_s[:, cs] = alpha * acc_s[:, cs] + jnp.dot(vt, p.astype(BF16),
                                                          preferred_element_type=F32)

    def group_body(g, carry):
        for u in range(ATT_UNROLL):
            step(ATT_UNROLL * g + u, False)
        return carry

    def single_body(kt, carry):
        step(kt, False)
        return carry

    for cb in range(ATT_AHEAD):
        scores(0, cb)
    n_grouped = i // ATT_UNROLL * ATT_UNROLL
    lax.fori_loop(0, i // ATT_UNROLL, group_body, 0)
    lax.fori_loop(n_grouped, i, single_body, 0)
    step(i, True)
    lam = _lambda_full(lamv_ref, lambda_init)
    a = acc_s[...] / l_s[...]
    d = a[:, 0:t] - lam * a[:, t:2 * t]
    inv = lax.rsqrt(jnp.mean(d * d, axis=0, keepdims=True) + NORM_EPS)
    y = d * inv * g_ref[...] * (1.0 - lambda_init)
    o_ref[...] = y.T.astype(BF16)


def _attn_prompt(qt_bf, k_bf, vt_bf, lamv, subln_g_col, lambda_init):
    nq = qt_bf.shape[0]
    t = nq * ATT_T
    kern = functools.partial(_attn_prompt_kernel, lambda_init=lambda_init)
    return pl.pallas_call(
        kern,
        grid=(N_HEADS, nq),
        in_specs=[
            pl.BlockSpec((1, V_DIM, ATT_T), lambda h, i: (i, h, 0)),
            pl.BlockSpec((t, V_DIM), lambda h, i: (0, h)),
            pl.BlockSpec((nq, V_DIM, ATT_T), lambda h, i: (0, h, 0)),
            pl.BlockSpec((4, HEAD_DIM), lambda h, i: (0, 0)),
            pl.BlockSpec((V_DIM, 1), lambda h, i: (0, 0)),
        ],
        out_specs=pl.BlockSpec((ATT_T, V_DIM), lambda h, i: (i, h)),
        out_shape=jax.ShapeDtypeStruct((t, ATT_WIDTH), BF16),
        scratch_shapes=[
            pltpu.VMEM((V_DIM, 2 * ATT_T), BF16),
            pltpu.VMEM((1, 2 * ATT_T), F32),
            pltpu.VMEM((1, 2 * ATT_T), F32),
            pltpu.VMEM((V_DIM, 2 * ATT_T), F32),
            pltpu.VMEM((2 * ATT_T // ATT_CB, ATT_T, ATT_CB), F32),
        ],
        compiler_params=_params(("arbitrary", "arbitrary"), 48),
        name="attn_prompt",
    )(qt_bf, k_bf, vt_bf, lamv, subln_g_col)


def _attn_sample_kernel(q_ref, kc_ref, vc_ref, kn_ref, vn_ref, lamv_ref, g_ref, o_ref,
                        *, lambda_init, past):
    rows = q_ref.shape[0]
    lam = _lambda_full(lamv_ref, lambda_init)
    contract_last = (((1,), (1,)), ((), ()))
    for h in range(N_HEADS):
        hs = slice(h * V_DIM, (h + 1) * V_DIM)
        q1, q2 = _split_heads(q_ref[:, hs])
        qq = jnp.concatenate([q1, q2], axis=0)
        s_c = jnp.dot(qq, kc_ref[0, hs, :].astype(BF16), preferred_element_type=F32)
        s_n = lax.dot_general(qq, kn_ref[:, hs], contract_last, preferred_element_type=F32)
        r = lax.broadcasted_iota(jnp.int32, s_n.shape, 0)
        c = lax.broadcasted_iota(jnp.int32, s_n.shape, 1)
        s_n = jnp.where(((past + c) // CHUNK) <= ((past + r % rows) // CHUNK), s_n, MASKED)
        m = jnp.maximum(jnp.max(s_c, axis=-1, keepdims=True), jnp.max(s_n, axis=-1, keepdims=True))
        p_c = jnp.exp2(s_c - m)
        p_n = jnp.exp2(s_n - m)
        l = jnp.sum(p_c, axis=-1, keepdims=True) + jnp.sum(p_n, axis=-1, keepdims=True)
        v_c = vc_ref[0, pl.ds(h, past, stride=N_HEADS), :].astype(BF16)
        acc = (jnp.dot(p_c.astype(BF16), v_c, preferred_element_type=F32)
               + jnp.dot(p_n.astype(BF16), vn_ref[:, hs], preferred_element_type=F32))
        o_ref[:, hs] = _diff_finish(acc, l, lam, g_ref[...], lambda_init, rows).astype(BF16)


def _attn_sample(q_bf, kt_cache, v_cache, k_bf, v_bf, lamv, subln_g, lambda_init):
    nb, _, past = kt_cache.shape
    rows = q_bf.shape[0] // nb
    kern = functools.partial(_attn_sample_kernel, lambda_init=lambda_init, past=past)
    new_spec = pl.BlockSpec((rows, ATT_WIDTH), lambda b: (b, 0))
    return pl.pallas_call(
        kern,
        grid=(nb,),
        in_specs=[
            new_spec,
            pl.BlockSpec((1, QK_WIDTH, past), lambda b: (b, 0, 0)),
            pl.BlockSpec((1, past * N_HEADS, V_DIM), lambda b: (b, 0, 0)),
            new_spec, new_spec,
            pl.BlockSpec((4, HEAD_DIM), lambda b: (0, 0)),
            pl.BlockSpec((1, V_DIM), lambda b: (0, 0)),
        ],
        out_specs=new_spec,
        out_shape=jax.ShapeDtypeStruct(q_bf.shape, BF16),
        compiler_params=_params(("arbitrary",), 48),
        name="attn_sample",
    )(q_bf, kt_cache, v_cache, k_bf, v_bf, lamv, subln_g)


def _outproj_kernel(att_p_ref, yc_p_ref, x_p_ref, att_s_ref, yc_s_ref, x_s_ref,
                    wt_ref, wb_ref, ng_ref, rw_ref, rb_ref, x1_ref, xf_ref, route_ref, *, n_prompt_tiles):
    i = pl.program_id(0)
    shared = (wt_ref, wb_ref, ng_ref, rw_ref, rb_ref, x1_ref, xf_ref, route_ref)

    @pl.when(i < n_prompt_tiles)
    def _():
        _outproj_tile(att_p_ref, yc_p_ref, x_p_ref, *shared)

    @pl.when(i >= n_prompt_tiles)
    def _():
        _outproj_tile(att_s_ref, yc_s_ref, x_s_ref, *shared)


def _outproj_tile(att_ref, yc_ref, x_ref, wt_ref, wb_ref, ng_ref, rw_ref, rb_ref,
                  x1_ref, xf_ref, route_ref):
    o = (jnp.dot(att_ref[...], wt_ref[...], preferred_element_type=F32)
         + jnp.dot(yc_ref[...], wb_ref[...], preferred_element_type=F32))
    x1 = x_ref[...] + o
    x1_ref[...] = x1
    xf = x1 * lax.rsqrt(jnp.mean(x1 * x1, axis=-1, keepdims=True) + NORM_EPS) * ng_ref[...]
    for c in range(SLAB):
        xf_ref[c] = xf[:, c * LANES:(c + 1) * LANES]
    logits =jnp.dot(xf.astype(BF16), rw_ref[...], preferred_element_type=F32) + rb_ref[...]

    lane = lax.broadcasted_iota(jnp.int32, logits.shape, 1)
    neg = -jnp.inf
    is_group = lane < N_GROUPS
    gl = jnp.where(is_group, logits, neg)
    gmax = jnp.max(gl, axis=-1, keepdims=True)
    grp = jnp.min(jnp.where(gl == gmax, lane, ROUTE_LANES), axis=-1, keepdims=True)
    gsum = jnp.sum(jnp.where(is_group, jnp.exp(gl - gmax), 0.0), axis=-1, keepdims=True)
    g_w = 1.0 / gsum
    e_lane = lane - N_GROUPS
    in_grp = (e_lane >= 0) & (e_lane < N_EXPERTS) & ((e_lane // EXPERTS_PER_GROUP) == grp)
    el = jnp.where(in_grp, logits, neg)
    t1 = jnp.max(el, axis=-1, keepdims=True)
    i1 = jnp.min(jnp.where(el == t1, lane, ROUTE_LANES), axis=-1, keepdims=True)
    el2 = jnp.where(lane == i1, neg, el)
    t2 = jnp.max(el2, axis=-1, keepdims=True)
    i2 = jnp.min(jnp.where(el2 == t2, lane, ROUTE_LANES), axis=-1, keepdims=True)
    r21 = jnp.exp(t2 - t1)
    w0 = g_w / (1.0 + r21)
    w1 = g_w * r21 / (1.0 + r21)
    e0 = (i1 - N_GROUPS).astype(F32)
    e1 = (i2 - N_GROUPS).astype(F32)
    route_ref[...] = jnp.where(lane == 0, e0, jnp.where(lane == 1, e1,
                               jnp.where(lane == 2, w0, jnp.where(lane == 3, w1, 0.0))))


def _outproj(att_p, yc_p, x_p, att_s, yc_s, x_s, w_top, w_bot, norm_g, rw, rb):
    tm = OUT_TM
    n_p = x_p.shape[0] // tm
    n_s = x_s.shape[0] // tm
    t_all = x_p.shape[0] + x_s.shape[0]
    prow = lambda i: (jnp.minimum(i, n_p - 1), 0)
    srow = lambda i: (jnp.maximum(i - n_p, 0), 0)
    row = lambda i: (i, 0)
    const = lambda i: (0, 0)
    kern = functools.partial(_outproj_kernel, n_prompt_tiles=n_p)
    return pl.pallas_call(
        kern,
        grid=(n_p + n_s,),
        in_specs=[
            pl.BlockSpec((tm, ATT_WIDTH), prow),
            pl.BlockSpec((tm, CONV_CH), prow),
            pl.BlockSpec((tm, D_MODEL), prow),
            pl.BlockSpec((tm, ATT_WIDTH), srow),
            pl.BlockSpec((tm, CONV_CH), srow),
            pl.BlockSpec((tm, D_MODEL), srow),
            pl.BlockSpec((ATT_WIDTH, D_MODEL), const),
            pl.BlockSpec((CONV_CH, D_MODEL), const),
            pl.BlockSpec((1, D_MODEL), const),
            pl.BlockSpec((D_MODEL, ROUTE_LANES), const),
            pl.BlockSpec((1, ROUTE_LANES), const),
        ],
        out_specs=[
            pl.BlockSpec((tm, D_MODEL), row),
            pl.BlockSpec((SLAB, tm, LANES), lambda i: (0, i, 0)),
            pl.BlockSpec((tm, ROUTE_LANES), row),
        ],
        out_shape=[
            jax.ShapeDtypeStruct((t_all, D_MODEL), F32),
            jax.ShapeDtypeStruct((SLAB, t_all, LANES), F32),
            jax.ShapeDtypeStruct((t_all, ROUTE_LANES), F32),
        ],
        compiler_params=_params(("arbitrary",), 52),
        name="outproj_router",
    )(att_p, yc_p, x_p, att_s, yc_s, x_s, w_top, w_bot, norm_g, rw, rb)


def _dispatch_copy(xf_ref, xs_hbm, sem, src_row, dst_row):
    return pltpu.make_async_copy(xf_ref.at[:, src_row, :],
                                 xs_hbm.at[pl.ds(dst_row * SLAB, SLAB)], sem)


def _dispatch_kernel(dest_ref, xf_ref, xs_init_hbm, xs_hbm, sem):
    del xs_init_hbm
    tm = DISPATCH_TM

    def issue(t, carry):
        for k in range(TOP_K):
            _dispatch_copy(xf_ref, xs_hbm, sem, t, dest_ref[0, 0, TOP_K * t + k]).start()
        return carry

    def drain(t, carry):
        for k in range(TOP_K):
            _dispatch_copy(xf_ref, xs_hbm, sem, t, dest_ref[0, 0, TOP_K * t + k]).wait()
        return carry

    lax.fori_loop(0, tm, issue, 0)
    lax.fori_loop(0, tm, drain, 0)


def _dispatch(dest, xf, cap):
    t_all = xf.shape[1]
    tm = DISPATCH_TM
    dest3 = dest.reshape(t_all // tm, 1, TOP_K * tm)
    xs_init = jnp.zeros((cap * SLAB, LANES), F32)
    return pl.pallas_call(
        _dispatch_kernel,
        grid=(t_all // tm,),
        in_specs=[
            pl.BlockSpec((1, 1, TOP_K * tm), lambda i: (i, 0, 0), memory_space=pltpu.SMEM),
            pl.BlockSpec((SLAB, tm, LANES), lambda i: (0, i, 0)),
            pl.BlockSpec(memory_space=pl.ANY),
        ],
        out_specs=pl.BlockSpec(memory_space=pl.ANY),
        out_shape=jax.ShapeDtypeStruct((cap * SLAB, LANES), F32),
        scratch_shapes=[pltpu.SemaphoreType.DMA(())],
        input_output_aliases={2: 0},
        compiler_params=_params(("arbitrary",), 24),
        name="moe_dispatch",
    )(dest3, xf, xs_init)


def _expert_weight_copies(hbm_refs, stage_refs, sems, e):
    return [pltpu.make_async_copy(hbm.at[e], stage, sems.at[n])
            for n, (hbm, stage) in enumerate(zip(hbm_refs, stage_refs))]


def _experts_kernel(blk_e_ref, first_ref, next_e_ref, nused_ref, x_ref, wg_hbm, wu_hbm, wd_hbm, o_ref,
                    wg_f, wu_f, wd_f, wgu_s, wd_s, sems):
    i = pl.program_id(0)
    hbm_refs = (wg_hbm, wu_hbm, wd_hbm)
    stage_refs = (wg_f, wu_f, wd_f)

    @pl.when(i == 0)
    def _():
        for copy in _expert_weight_copies(hbm_refs, stage_refs, sems, blk_e_ref[0]):
            copy.start()

    @pl.when(i < nused_ref[0])
    def _():
        @pl.when(first_ref[i] == 1)
        def _():
            for copy in _expert_weight_copies(hbm_refs, stage_refs, sems, blk_e_ref[i]):
                copy.wait()
            wgu_s[:, 0:D_EXPERT] = wg_f[...].astype(BF16)
            wgu_s[:, D_EXPERT:2 * D_EXPERT] = wu_f[...].astype(BF16)
            wd_s[...] = wd_f[...].astype(BF16)

            @pl.when(next_e_ref[i] >= 0)
            def _():
                for copy in _expert_weight_copies(hbm_refs, stage_refs, sems, next_e_ref[i]):
                    copy.start()

        x = _load_token_slabs(x_ref, MOE_BM).astype(BF16)
        gu = jnp.dot(x, wgu_s[...], preferred_element_type=F32)
        g = gu[:, 0:D_EXPERT]
        u = gu[:, D_EXPERT:2 * D_EXPERT]
        h = g / (1.0 + jnp.exp(-g)) * u
        _store_token_slabs(o_ref, jnp.dot(h.astype(BF16), wd_s[...], preferred_element_type=F32))

    @pl.when(i >= nused_ref[0])
    def _():
        o_ref[...] = jnp.zeros(o_ref.shape, F32)


def _experts(blk_e, first, next_e, nused, xs, w_gate, w_up, w_down):
    cap = xs.shape[0] // SLAB
    bm = MOE_BM
    rows = lambda i, be, fi, ne, nu: (jnp.minimum(i, nu[0] - 1), 0)
    grid_spec = pltpu.PrefetchScalarGridSpec(
        num_scalar_prefetch=4,
        grid=(cap // bm,),
        in_specs=[
            pl.BlockSpec((bm * SLAB, LANES), rows),
            pl.BlockSpec(memory_space=pl.ANY),
            pl.BlockSpec(memory_space=pl.ANY),
            pl.BlockSpec(memory_space=pl.ANY),
        ],
        out_specs=pl.BlockSpec((bm * SLAB, LANES), lambda i, be, fi, ne, nu: (i, 0)),
        scratch_shapes=[
            pltpu.VMEM((D_MODEL, D_EXPERT), F32),
            pltpu.VMEM((D_MODEL, D_EXPERT), F32),
            pltpu.VMEM((D_EXPERT, D_MODEL), F32),
            pltpu.VMEM((D_MODEL, 2 * D_EXPERT), BF16),
            pltpu.VMEM((D_EXPERT, D_MODEL), BF16),
            pltpu.SemaphoreType.DMA((3,)),
        ],
    )
    return pl.pallas_call(
        _experts_kernel,
        grid_spec=grid_spec,
        out_shape=jax.ShapeDtypeStruct((cap * SLAB, LANES), F32),
        compiler_params=_params(("arbitrary",), 48),
        name="moe_experts",
    )(blk_e, first, next_e, nused, xs, w_gate, w_up, w_down)


def _combine_copy(outs_hbm, g_s, sems, slot, src_row, k, t):
    return pltpu.make_async_copy(outs_hbm.at[pl.ds(src_row * SLAB, SLAB)],
                                 g_s.at[slot, k, :, t, :], sems.at[slot])


def _combine_kernel(dest_ref, dest_next_ref, x1_ref, route_ref, outs_hbm, y_ref, g_s, sems, *, n_tiles):
    i = pl.program_id(0)
    tm = x1_ref.shape[0]
    slot = i % 2

    def gather(table_ref, into, start):
        def body(t, carry):
            for k in range(TOP_K):
                copy = _combine_copy(outs_hbm, g_s, sems, into, table_ref[0, 0, TOP_K * t + k], k, t)
                if start:
                    copy.start()
                else:
                    copy.wait()
            return carry
        lax.fori_loop(0, tm, body, 0)

    @pl.when(i == 0)
    def _():
        gather(dest_ref, 0, True)

    @pl.when(i + 1 < n_tiles)
    def _():
        gather(dest_next_ref, 1 - slot, True)

    gather(dest_ref, slot, False)
    r = route_ref[...]
    w0 = jnp.broadcast_to(r[:, 2:3], (tm, LANES))
    w1 = jnp.broadcast_to(r[:, 3:4], (tm, LANES))
    for c in range(SLAB):
        cs = slice(c * LANES, (c + 1) * LANES)
        y_ref[:, cs] = x1_ref[:, cs] + w0 * g_s[slot, 0, c] + w1 * g_s[slot, 1, c]


def _combine(dest, x1, route, outs, row_off, t):
    tm = OUT_TM
    t_all = x1.shape[0]
    off = row_off // tm
    dest3 = dest.reshape(t_all // tm, 1, TOP_K * tm)
    n = t // tm
    return pl.pallas_call(
        functools.partial(_combine_kernel, n_tiles=n),
        grid=(n,),
        in_specs=[
            pl.BlockSpec((1, 1, TOP_K * tm), lambda i: (i + off, 0, 0), memory_space=pltpu.SMEM),
            pl.BlockSpec((1, 1, TOP_K * tm), lambda i: (jnp.minimum(i + 1, n - 1) + off, 0, 0),
                         memory_space=pltpu.SMEM),
            pl.BlockSpec((tm, D_MODEL), lambda i: (i + off, 0)),
            pl.BlockSpec((tm, ROUTE_LANES), lambda i: (i + off, 0)),
            pl.BlockSpec(memory_space=pl.ANY),
        ],
        out_specs=pl.BlockSpec((tm, D_MODEL), lambda i: (i, 0)),
        out_shape=jax.ShapeDtypeStruct((t, D_MODEL), F32),
        scratch_shapes=[pltpu.VMEM((2, TOP_K, SLAB, tm, LANES), F32), pltpu.SemaphoreType.DMA((2,))],
        compiler_params=_params(("arbitrary",), 32),
        name="moe_combine",
    )(dest3, dest3, x1, route, outs)


def _routing_tables(route, n_blocks):
    bm = MOE_BM
    flat_e = route[:, 0:TOP_K].astype(jnp.int32).reshape(-1)
    onehot = (flat_e[:, None] == jnp.arange(N_EXPERTS, dtype=jnp.int32)[None, :]).astype(jnp.int32)
    csum = jnp.cumsum(onehot, axis=0)
    rank = jnp.sum(csum * onehot, axis=1) - 1
    counts = csum[-1]
    padded = (counts + bm - 1) // bm * bm
    pad_end = jnp.cumsum(padded)
    pad_start = pad_end - padded
    dest = (pad_start[flat_e] + rank).astype(jnp.int32)
    nused = (pad_end[-1] // bm).astype(jnp.int32)
    blk = jnp.arange(n_blocks, dtype=jnp.int32)
    blk_e = jnp.searchsorted(pad_end, jnp.minimum(blk, nused - 1) * bm, side='right').astype(jnp.int32)
    blk_e = jnp.minimum(blk_e, N_EXPERTS - 1)
    first = jnp.concatenate([jnp.ones((1,), jnp.int32),
                             (blk_e[1:] != blk_e[:-1]).astype(jnp.int32)])
    experts = jnp.arange(N_EXPERTS, dtype=jnp.int32)
    later_owner = jnp.where((counts[None, :] > 0) & (experts[None, :] > experts[:, None]),
                            experts[None, :], N_EXPERTS)
    next_owner = jnp.min(later_owner, axis=1)
    next_owner = jnp.where(next_owner == N_EXPERTS, -1, next_owner).astype(jnp.int32)
    return dest, blk_e, first, next_owner[blk_e], nused.reshape(1)


def _rope_tables(pos):
    half = HEAD_DIM // 2
    inv = ROPE_THETA ** (-jnp.arange(half, dtype=F32) / half)
    ang = pos.astype(F32)[:, None] * inv[None, :]
    cos = jnp.cos(ang)
    sin = jnp.sin(ang)
    cos_h = jnp.concatenate([cos, cos], axis=-1)
    sin_h = jnp.concatenate([-sin, sin], axis=-1)
    reps = LANES // HEAD_DIM
    return jnp.tile(cos_h, (1, reps)), jnp.tile(sin_h, (1, reps))


def kernel(x_prompt, x_sample, cache_k, cache_v, state_conv, norm_mix_g, w_in, q_norm_g, k_norm_g, lambda_q1, lambda_k1, lambda_q2, lambda_k2, subln_g, conv_w, conv_norm_g, w_out, norm_ffn_g, router_group_w, router_group_b, router_expert_w, router_expert_b, expert_w_gate, expert_w_up, expert_w_down):
    assert w_in.shape[0] == 1, "single-layer step"
    bp, sp, _ = x_prompt.shape
    bs, ss, _ = x_sample.shape
    past = cache_k.shape[2]
    assert bp == 1 and sp % ATT_T == 0
    tp = bp * sp
    ts = bs * ss
    t_all = tp + ts
    lambda_init = 0.8 - 0.6 * math.exp(-0.3 * 0)

    w_in_bf = w_in[0].astype(BF16)
    w_out_bf = w_out[0].astype(BF16)
    w_top, w_bot = w_out_bf[:ATT_WIDTH], w_out_bf[ATT_WIDTH:]
    ng = norm_mix_g[0].reshape(1, D_MODEL)
    qg = jnp.tile(q_norm_g[0], QK_WIDTH // HEAD_DIM).reshape(1, QK_WIDTH)
    kg = jnp.tile(k_norm_g[0], QK_WIDTH // HEAD_DIM).reshape(1, QK_WIDTH)
    head_of = jnp.arange(IN_CH, dtype=jnp.int32) // HEAD_DIM
    gmat = jnp.where(head_of[:, None] == head_of[None, :], 1.0 / HEAD_DIM, 0.0).astype(BF16)
    lamv = jnp.stack([lambda_q1[0], lambda_k1[0], lambda_q2[0], lambda_k2[0]]).astype(F32)
    sg = subln_g[0].reshape(1, V_DIM)
    cw = conv_w[0]
    cng = conv_norm_g[0].reshape(1, CONV_CH)
    cos_p, sin_p = _rope_tables(jnp.arange(sp, dtype=jnp.int32))
    cos_s, sin_s = _rope_tables(jnp.tile(past + jnp.arange(ss, dtype=jnp.int32), bs))

    zero_conv = jnp.zeros((1, CONV_K - 1, CONV_CH), F32)
    qt_p, kf_p, kb_p, vf_p, vt_p, yc_p, tail_p = _inproj(
        x_prompt.reshape(tp, D_MODEL), ng, w_in_bf, qg, kg, gmat, cos_p, sin_p, zero_conv, cw, cng,
        tm=ATT_T, nseq=1, carry=True, qv_transposed=True)
    att_p = _attn_prompt(qt_p, kb_p, vt_p, lamv, subln_g[0].reshape(V_DIM, 1), lambda_init)

    seqs_per_tile = OUT_TM // ss
    q_s, kf_s, kb_s, vf_s, vb_s, yc_s, tail_s = _inproj(
        x_sample.reshape(ts, D_MODEL), ng, w_in_bf, qg, kg, gmat, cos_s, sin_s, state_conv[0], cw, cng,
        tm=OUT_TM, nseq=seqs_per_tile, carry=False, qv_transposed=False)
    kt_cache = jnp.transpose(cache_k[0], (0, 2, 3, 4, 1)).reshape(bs, QK_WIDTH, past)
    att_s = _attn_sample(q_s, kt_cache, cache_v[0].reshape(bs, past * N_HEADS, V_DIM),
                         kb_s, vb_s, lamv, sg, lambda_init)

    rw = jnp.zeros((D_MODEL, ROUTE_LANES), F32)
    rw = rw.at[:, 0:N_GROUPS].set(router_group_w[0]).at[:, N_GROUPS:N_GROUPS + N_EXPERTS].set(router_expert_w[0])
    rb = jnp.zeros((1, ROUTE_LANES), F32)
    rb = rb.at[0, 0:N_GROUPS].set(router_group_b[0]).at[0, N_GROUPS:N_GROUPS + N_EXPERTS].set(router_expert_b[0])
    nf = norm_ffn_g[0].reshape(1, D_MODEL)
    rw_bf = rw.astype(BF16)
    x1, xf, route = _outproj(att_p, yc_p, x_prompt.reshape(tp, D_MODEL),
                             att_s, yc_s, x_sample.reshape(ts, D_MODEL), w_top, w_bot, nf, rw_bf, rb)

    n = t_all * TOP_K
    n_blocks = n // MOE_BM + N_EXPERTS
    dest, blk_e, first, next_e, nused = _routing_tables(route, n_blocks)
    xs = _dispatch(dest, xf, n_blocks * MOE_BM)
    outs = _experts(blk_e, first, next_e, nused, xs, expert_w_gate[0], expert_w_up[0], expert_w_down[0])
    y_p = _combine(dest, x1, route, outs, 0, tp)
    y_s = _combine(dest, x1, route, outs, tp, ts)

    return (y_p.reshape(bp, sp, D_MODEL),
            y_s.reshape(bs, ss, D_MODEL),
            kf_p.reshape(1, bp, sp, N_HEADS, 2, HEAD_DIM),
            vf_p.reshape(1, bp, sp, N_HEADS, V_DIM),
            tail_p.reshape(1, bp, CONV_K - 1, CONV_CH),
            kf_s.reshape(1, bs, ss, N_HEADS, 2, HEAD_DIM),
            vf_s.reshape(1, bs, ss, N_HEADS, V_DIM),
            tail_s.reshape(1, bs, CONV_K - 1, CONV_CH))
```

```python
import functools
import math

import jax
import jax.numpy as jnp
from jax import lax
from jax.experimental import pallas as pl
from jax.experimental.pallas import tpu as pltpu

F32 = jnp.float32
BF16 = jnp.bfloat16

D_MODEL = 2048
CHUNK = 64
HEAD_DIM = 64
V_DIM = 2 * HEAD_DIM
N_HEADS = 8
QK_WIDTH = N_HEADS * 2 * HEAD_DIM
ATT_WIDTH = N_HEADS * V_DIM
CONV_CH = 1024
CONV_K = 3
ROPE_THETA = 10000.0
N_GROUPS = 8
EXPERTS_PER_GROUP = 8
N_EXPERTS = N_GROUPS * EXPERTS_PER_GROUP
TOP_K = 2
D_EXPERT = D_MODEL // 4
NORM_EPS = 1e-6
SECTION = 1024
N_SECTIONS = 6

LANES = 128
IN_CH = 256
ATT_AHEAD = 2
ATT_ONES = 16
ATT_UNROLL = 4
ATT_CB = 512
ATT_T = 512
OUT_TM = 256
MOE_BM = 256
DISPATCH_TM = 512
SLAB = D_MODEL // LANES
ROUTE_LANES = LANES
MASKED = -1e30
Q_SCALE = HEAD_DIM ** -0.5 * math.log2(math.e)
MIB = 1024 * 1024


def _store_token_slabs(ref, x):
    rows = x.shape[0]
    for c in range(SLAB):
        ref[pl.ds(c, rows, stride=SLAB), :] = x[:, c * LANES:(c + 1) * LANES]


def _load_token_slabs(ref, rows):
    return jnp.concatenate([ref[pl.ds(c, rows, stride=SLAB), :] for c in range(SLAB)], axis=1)


def _params(sem, vmem_mib):
    return pltpu.CompilerParams(dimension_semantics=sem, vmem_limit_bytes=vmem_mib * MIB)


def _inproj_kernel(x_ref, ng_ref, w_ref, qg_ref, kg_ref, gmat_ref, cos_ref, sin_ref,
                   prev_ref, cw_ref, cng_ref,
                   q_ref, kf_ref, kb_ref, vf_ref, vb_ref, yc_ref, tail_ref,
                   xn_s, gb_s, c_s, z_s, u_s, *, nseq, carry, qv_transposed):
    i = pl.program_id(0)
    j = pl.program_id(1)
    tm = x_ref.shape[0]
    seq = tm // nseq

    @pl.when(j == 0)
    def _():
        x = x_ref[...]
        inv = lax.rsqrt(jnp.mean(x * x, axis=-1, keepdims=True) + NORM_EPS)
        xn_s[...] = (x * inv * ng_ref[...]).astype(BF16)
        if carry:
            @pl.when(i == 0)
            def _():
                u_s[6:8, :] = prev_ref[0]

    def chunk_dot(c):
        return jnp.dot(xn_s[...], w_ref[:, c * IN_CH:(c + 1) * IN_CH], preferred_element_type=F32)

    def pipelined(epilogue):
        acc = chunk_dot(0)
        for c in range(SECTION // IN_CH):
            nxt = chunk_dot(c + 1) if c + 1 < SECTION // IN_CH else None
            epilogue(acc, slice(c * IN_CH, (c + 1) * IN_CH))
            acc = nxt

    def head_norm_rope(a, g):
        ms = jnp.dot((a * a).astype(BF16), gmat_ref[...], preferred_element_type=F32)
        y = a * lax.rsqrt(ms + NORM_EPS) * g
        reps = IN_CH // LANES
        cos = jnp.concatenate([cos_ref[...]] * reps, axis=1)
        sin = jnp.concatenate([sin_ref[...]] * reps, axis=1)
        lane = lax.broadcasted_iota(jnp.int32, y.shape, 1)
        first = (lane & (HEAD_DIM - 1)) < HEAD_DIM // 2
        partner = jnp.where(first, pltpu.roll(y, IN_CH - HEAD_DIM // 2, 1),
                            pltpu.roll(y, HEAD_DIM // 2, 1))
        return y * cos + partner * sin

    for section in range(N_SECTIONS):

        def epilogue(acc, cols, section=section):
            if section == 0:
                q = head_norm_rope(acc, qg_ref[:, cols]) * Q_SCALE
                if qv_transposed:
                    q_ref[0, cols, :] = q.T.astype(BF16)
                else:
                    q_ref[:, cols] = q.astype(BF16)
            elif section == 1:
                k = head_norm_rope(acc, kg_ref[:, cols])
                kf_ref[:, cols] = k
                kb_ref[:, cols] = k.astype(BF16)
            elif section == 2:
                vf_ref[:, cols] = acc
                if qv_transposed:
                    vb_ref[0, cols, :] = acc.T.astype(BF16)
                else:
                    vb_ref[:, cols] = acc.astype(BF16)
            elif section == 3:
                gb_s[:, cols] = acc
            elif section == 4:
                c_s[:, cols] = acc
            else:
                u = c_s[:, cols] * acc
                w0 = cw_ref[0:1, cols]
                w1 = cw_ref[1:2, cols]
                w2 = cw_ref[2:3, cols]
                for s in range(nseq):
                    rows = slice(s * seq, (s + 1) * seq)
                    if not carry:
                        u_s[6:8, cols] = prev_ref[s, :, cols]
                    u_s[8:8 + seq, cols] = u[rows]
                    conv = (w0 * u_s[6:6 + seq, cols] + w1 * u_s[7:7 + seq, cols]
                            + w2 * u_s[8:8 + seq, cols])
                    z_s[rows, cols] = gb_s[rows, cols] * conv
                    tail = u_s[seq + 6:seq + 8, cols]
                    tail_ref[s, :, cols] = tail
                    if carry:
                        u_s[6:8, cols] = tail

        @pl.when(j == section)
        def _(section=section, epilogue=epilogue):
            pipelined(epilogue)
            if section == N_SECTIONS - 1:
                z = z_s[...]
                inv = lax.rsqrt(jnp.mean(z * z, axis=-1, keepdims=True) + NORM_EPS)
                yc_ref[...] = (z * inv * cng_ref[...]).astype(BF16)


def _inproj(x2d, norm_g, w_in_bf, qg, kg, gmat, cos, sin, conv_prev, conv_w, conv_norm_g,
            *, tm, nseq, carry, qv_transposed):
    t = x2d.shape[0]
    ni = t // tm
    seq = tm // nseq
    row = lambda i, j: (i, 0)
    const = lambda i, j: (0, 0)
    if qv_transposed:
        qv_shape = jax.ShapeDtypeStruct((ni, QK_WIDTH, tm), BF16)
        qv_spec = pl.BlockSpec((1, QK_WIDTH, tm), lambda i, j: (i, 0, 0))
    else:
        qv_shape = jax.ShapeDtypeStruct((t, QK_WIDTH), BF16)
        qv_spec = pl.BlockSpec((tm, QK_WIDTH), row)
    if carry:
        prev_spec = pl.BlockSpec((1, CONV_K - 1, CONV_CH), lambda i, j: (0, 0, 0))
        tail_shape = jax.ShapeDtypeStruct((1, CONV_K - 1, CONV_CH), F32)
        tail_spec = pl.BlockSpec((1, CONV_K - 1, CONV_CH), lambda i, j: (0, 0, 0))
    else:
        prev_spec = pl.BlockSpec((nseq, CONV_K - 1, CONV_CH), lambda i, j: (i, 0, 0))
        tail_shape = jax.ShapeDtypeStruct((ni * nseq, CONV_K - 1, CONV_CH), F32)
        tail_spec = pl.BlockSpec((nseq, CONV_K - 1, CONV_CH), lambda i, j: (i, 0, 0))
    kern = functools.partial(_inproj_kernel, nseq=nseq, carry=carry, qv_transposed=qv_transposed)
    return pl.pallas_call(
        kern,
        grid=(ni, N_SECTIONS),
        in_specs=[
            pl.BlockSpec((tm, D_MODEL), row),
            pl.BlockSpec((1, D_MODEL), const),
            pl.BlockSpec((D_MODEL, SECTION), lambda i, j: (0, j)),
            pl.BlockSpec((1, SECTION), const),
            pl.BlockSpec((1, SECTION), const),
            pl.BlockSpec((IN_CH, IN_CH), const),
            pl.BlockSpec((tm, LANES), row),
            pl.BlockSpec((tm, LANES), row),
            prev_spec,
            pl.BlockSpec((CONV_K, CONV_CH), const),
            pl.BlockSpec((1, CONV_CH), const),
        ],
        out_specs=[
            qv_spec,
            pl.BlockSpec((tm, QK_WIDTH), row),
            pl.BlockSpec((tm, QK_WIDTH), row),
            pl.BlockSpec((tm, ATT_WIDTH), row),
            qv_spec,
            pl.BlockSpec((tm, CONV_CH), row),
            tail_spec,
        ],
        out_shape=[
            qv_shape,
            jax.ShapeDtypeStruct((t, QK_WIDTH), F32),
            jax.ShapeDtypeStruct((t, QK_WIDTH), BF16),
            jax.ShapeDtypeStruct((t, ATT_WIDTH), F32),
            qv_shape,
            jax.ShapeDtypeStruct((t, CONV_CH), BF16),
            tail_shape,
        ],
        scratch_shapes=[
            pltpu.VMEM((tm, D_MODEL), BF16),
            pltpu.VMEM((tm, CONV_CH), F32),
            pltpu.VMEM((tm, CONV_CH), F32),
            pltpu.VMEM((tm, CONV_CH), F32),
            pltpu.VMEM((seq + 8, CONV_CH), F32),
        ],
        compiler_params=_params(("arbitrary", "arbitrary"), 56),
        name="inproj_carry" if carry else "inproj_seqs",
    )(x2d, norm_g, w_in_bf, qg, kg, gmat, cos, sin, conv_prev, conv_w, conv_norm_g)


def _lambda_full(lamv_ref, lambda_init):
    lv = lamv_ref[...]
    a = jnp.sum(lv[0:1] * lv[1:2], axis=-1, keepdims=True)
    b = jnp.sum(lv[2:3] * lv[3:4], axis=-1, keepdims=True)
    return jnp.exp(a) - jnp.exp(b) + lambda_init


def _split_heads(q):
    lane = lax.broadcasted_iota(jnp.int32, q.shape, 1)
    zero = jnp.zeros_like(q)
    return jnp.where(lane < HEAD_DIM, q, zero), jnp.where(lane >= HEAD_DIM, q, zero)


def _diff_finish(acc, l, lam, g, lambda_init, rows):
    a = acc / l
    d = a[0:rows] - lam * a[rows:2 * rows]
    inv = lax.rsqrt(jnp.mean(d * d, axis=-1, keepdims=True) + NORM_EPS)
    return d * inv * g * (1.0 - lambda_init)


def _attn_prompt_kernel(qt_ref, k_ref, vt_ref, lamv_ref, g_ref, o_ref, qq_s, m_s, acc_s, s_s,
                        *, lambda_init):
    i = pl.program_id(1)
    t = ATT_T
    qt = qt_ref[0]
    feat = lax.broadcasted_iota(jnp.int32, qt.shape, 0)
    zero = jnp.zeros_like(qt)
    qq_s[:, 0:t] = jnp.where(feat < HEAD_DIM, qt, zero)
    qq_s[:, t:2 * t] = jnp.where(feat >= HEAD_DIM, qt, zero)
    m_s[...] = jnp.full(m_s.shape, MASKED, F32)
    acc_s[...] = jnp.zeros(acc_s.shape, F32)
    ones_rows = (lax.broadcasted_iota(jnp.int32, (ATT_ONES, t), 0) == 0).astype(BF16)

    nblk = 2 * t // ATT_CB

    def scores(kt, cb):
        k = k_ref[pl.ds(pl.multiple_of(kt * t, t), t), :]
        s = jnp.dot(k, qq_s[:, cb * ATT_CB:(cb + 1) * ATT_CB], preferred_element_type=F32)
        s_s[cb] = s.astype(BF16)

    def step(kt, diagonal):
        vt = jnp.concatenate([vt_ref[kt], ones_rows], axis=0)
        for cb in range(nblk):
            cs = slice(cb * ATT_CB, (cb + 1) * ATT_CB)
            s = s_s[cb]
            if diagonal:
                key = lax.broadcasted_iota(jnp.int32, s.shape, 0)
                qry = lax.broadcasted_iota(jnp.int32, s.shape, 1) + (cb * ATT_CB) % t
                s = s + jnp.where((key // CHUNK) <= (qry // CHUNK), 0.0, MASKED).astype(BF16)
            m_prev = m_s[:, cs]
            m_new = jnp.maximum(m_prev, jnp.max(s, axis=0, keepdims=True).astype(F32))
            alpha = jnp.exp2(m_prev - m_new)
            p = jnp.exp2(s - m_new.astype(BF16))
            m_s[:, cs] = m_new
            ahead = cb + ATT_AHEAD
            if ahead < nblk:
                scores(kt, ahead)
            elif not diagonal:
                scores(kt + 1, ahead - nblk)
            acc_s[:, cs] = alpha * acc_s[:, cs] + jnp.dot(vt, p, preferred_element_type=F32)

    def group_body(g, carry):
        for u in range(ATT_UNROLL):
            step(ATT_UNROLL * g + u, False)
        return carry

    def single_body(kt, carry):
        step(kt, False)
        return carry

    for cb in range(ATT_AHEAD):
        scores(0, cb)
    n_grouped = i // ATT_UNROLL * ATT_UNROLL
    lax.fori_loop(0, i // ATT_UNROLL, group_body, 0)
    lax.fori_loop(n_grouped, i, single_body, 0)
    step(i, True)
    lam = _lambda_full(lamv_ref, lambda_init)
    a = acc_s[0:V_DIM, :] / acc_s[V_DIM:V_DIM + 1, :]
    d = a[:, 0:t] - lam * a[:, t:2 * t]
    inv = lax.rsqrt(jnp.mean(d * d, axis=0, keepdims=True) + NORM_EPS)
    y = d * inv * g_ref[...] * (1.0 - lambda_init)
    o_ref[...] = y.T.astype(BF16)


def _attn_prompt(qt_bf, k_bf, vt_bf, lamv, subln_g_col, lambda_init):
    nq = qt_bf.shape[0]
    t = nq * ATT_T
    kern = functools.partial(_attn_prompt_kernel, lambda_init=lambda_init)
    return pl.pallas_call(
        kern,
        grid=(N_HEADS, nq),
        in_specs=[
            pl.BlockSpec((1, V_DIM, ATT_T), lambda h, i: (i, h, 0)),
            pl.BlockSpec((t, V_DIM), lambda h, i: (0, h)),
            pl.BlockSpec((nq, V_DIM, ATT_T), lambda h, i: (0, h, 0)),
            pl.BlockSpec((4, HEAD_DIM), lambda h, i: (0, 0)),
            pl.BlockSpec((V_DIM, 1), lambda h, i: (0, 0)),
        ],
        out_specs=pl.BlockSpec((ATT_T, V_DIM), lambda h, i: (i, h)),
        out_shape=jax.ShapeDtypeStruct((t, ATT_WIDTH), BF16),
        scratch_shapes=[
            pltpu.VMEM((V_DIM, 2 * ATT_T), BF16),
            pltpu.VMEM((1, 2 * ATT_T), F32),
            pltpu.VMEM((V_DIM + ATT_ONES, 2 * ATT_T), F32),
            pltpu.VMEM((2 * ATT_T // ATT_CB, ATT_T, ATT_CB), BF16),
        ],
        compiler_params=_params(("arbitrary", "arbitrary"), 48),
        name="attn_prompt",
    )(qt_bf, k_bf, vt_bf, lamv, subln_g_col)


def _attn_sample_kernel(q_ref, kc_ref, vc_ref, kn_ref, vn_ref, lamv_ref, g_ref, o_ref,
                        *, lambda_init, past):
    rows = q_ref.shape[0]
    lam = _lambda_full(lamv_ref, lambda_init)
    contract_last = (((1,), (1,)), ((), ()))
    for h in range(N_HEADS):
        hs = slice(h * V_DIM, (h + 1) * V_DIM)
        q1, q2 = _split_heads(q_ref[:, hs])
        qq = jnp.concatenate([q1, q2], axis=0)
        s_c = jnp.dot(qq, kc_ref[0, hs, :].astype(BF16), preferred_element_type=F32)
        s_n = lax.dot_general(qq, kn_ref[:, hs], contract_last, preferred_element_type=F32)
        r = lax.broadcasted_iota(jnp.int32, s_n.shape, 0)
        c = lax.broadcasted_iota(jnp.int32, s_n.shape, 1)
        s_n = jnp.where(((past + c) // CHUNK) <= ((past + r % rows) // CHUNK), s_n, MASKED)
        m = jnp.maximum(jnp.max(s_c, axis=-1, keepdims=True), jnp.max(s_n, axis=-1, keepdims=True))
        p_c = jnp.exp2(s_c - m)
        p_n = jnp.exp2(s_n - m)
        l = jnp.sum(p_c, axis=-1, keepdims=True) + jnp.sum(p_n, axis=-1, keepdims=True)
        v_c = vc_ref[0, pl.ds(h, past, stride=N_HEADS), :].astype(BF16)
        acc = (jnp.dot(p_c.astype(BF16), v_c, preferred_element_type=F32)
               + jnp.dot(p_n.astype(BF16), vn_ref[:, hs], preferred_element_type=F32))
        o_ref[:, hs] = _diff_finish(acc, l, lam, g_ref[...], lambda_init, rows).astype(BF16)


def _attn_sample(q_bf, kt_cache, v_cache, k_bf, v_bf, lamv, subln_g, lambda_init):
    nb, _, past = kt_cache.shape
    rows = q_bf.shape[0] // nb
    kern = functools.partial(_attn_sample_kernel, lambda_init=lambda_init, past=past)
    new_spec = pl.BlockSpec((rows, ATT_WIDTH), lambda b: (b, 0))
    return pl.pallas_call(
        kern,
        grid=(nb,),
        in_specs=[
            new_spec,
            pl.BlockSpec((1, QK_WIDTH, past), lambda b: (b, 0, 0)),
            pl.BlockSpec((1, past * N_HEADS, V_DIM), lambda b: (b, 0, 0)),
            new_spec, new_spec,
            pl.BlockSpec((4, HEAD_DIM), lambda b: (0, 0)),
            pl.BlockSpec((1, V_DIM), lambda b: (0, 0)),
        ],
        out_specs=new_spec,
        out_shape=jax.ShapeDtypeStruct(q_bf.shape, BF16),
        compiler_params=_params(("arbitrary",), 48),
        name="attn_sample",
    )(q_bf, kt_cache, v_cache, k_bf, v_bf, lamv, subln_g)


def _outproj_kernel(att_p_ref, yc_p_ref, x_p_ref, att_s_ref, yc_s_ref, x_s_ref,
                    wt_ref, wb_ref, ng_ref, rw_ref, rb_ref, x1_ref, xf_ref, route_ref, count_ref, count_s,
                    *, n_prompt_tiles):
    i = pl.program_id(0)
    shared = (wt_ref, wb_ref, ng_ref, rw_ref, rb_ref, x1_ref, xf_ref, route_ref, count_ref, count_s)

    @pl.when(i == 0)
    def _():
        count_s[...] = jnp.zeros(count_s.shape, F32)

    @pl.when(i < n_prompt_tiles)
    def _():
        _outproj_tile(att_p_ref, yc_p_ref, x_p_ref, *shared)

    @pl.when(i >= n_prompt_tiles)
    def _():
        _outproj_tile(att_s_ref, yc_s_ref, x_s_ref, *shared)


def _outproj_tile(att_ref, yc_ref, x_ref, wt_ref, wb_ref, ng_ref, rw_ref, rb_ref,
                  x1_ref, xf_ref, route_ref, count_ref, count_s):
    o = (jnp.dot(att_ref[...], wt_ref[...], preferred_element_type=F32)
         + jnp.dot(yc_ref[...], wb_ref[...], preferred_element_type=F32))
    x1 = x_ref[...] + o
    x1_ref[...] = x1
    xf = x1 * lax.rsqrt(jnp.mean(x1 * x1, axis=-1, keepdims=True) + NORM_EPS) * ng_ref[...]
    for c in range(SLAB):
        xf_ref[c] = xf[:, c * LANES:(c + 1) * LANES]
    logits =jnp.dot(xf.astype(BF16), rw_ref[...], preferred_element_type=F32) + rb_ref[...]

    lane = lax.broadcasted_iota(jnp.int32, logits.shape, 1)
    neg = -jnp.inf
    is_group = lane < N_GROUPS
    gl = jnp.where(is_group, logits, neg)
    gmax = jnp.max(gl, axis=-1, keepdims=True)
    grp = jnp.min(jnp.where(gl == gmax, lane, ROUTE_LANES), axis=-1, keepdims=True)
    gsum = jnp.sum(jnp.where(is_group, jnp.exp(gl - gmax), 0.0), axis=-1, keepdims=True)
    g_w = 1.0 / gsum
    e_lane = lane - N_GROUPS
    in_grp = (e_lane >= 0) & (e_lane < N_EXPERTS) & ((e_lane // EXPERTS_PER_GROUP) == grp)
    el = jnp.where(in_grp, logits, neg)
    t1 = jnp.max(el, axis=-1, keepdims=True)
    i1 = jnp.min(jnp.where(el == t1, lane, ROUTE_LANES), axis=-1, keepdims=True)
    el2 = jnp.where(lane == i1, neg, el)
    t2 = jnp.max(el2, axis=-1, keepdims=True)
    i2 = jnp.min(jnp.where(el2 == t2, lane, ROUTE_LANES), axis=-1, keepdims=True)
    r21 = jnp.exp(t2 - t1)
    w0 = g_w / (1.0 + r21)
    w1 = g_w * r21 / (1.0 + r21)
    e0 = (i1 - N_GROUPS).astype(F32)
    e1 = (i2 - N_GROUPS).astype(F32)
    rows = logits.shape[0]
    picked0 = lane == i1
    picked1 = lane == i2
    member = (picked0 | picked1).astype(BF16)
    earlier = (lax.broadcasted_iota(jnp.int32, (rows, rows), 1)
               < lax.broadcasted_iota(jnp.int32, (rows, rows), 0)).astype(BF16)
    before = jnp.dot(earlier, member, preferred_element_type=F32) + count_s[...]
    r0 = jnp.sum(jnp.where(picked0, before, 0.0), axis=-1, keepdims=True)
    r1 = jnp.sum(jnp.where(picked1, before, 0.0), axis=-1, keepdims=True)
    count_s[...] += jnp.sum(member.astype(F32), axis=0, keepdims=True)
    count_ref[...] = count_s[...]
    route_ref[...] = jnp.where(lane == 0, e0, jnp.where(lane == 1, e1,
                               jnp.where(lane == 2, w0, jnp.where(lane == 3, w1,
                               jnp.where(lane == 4, r0, jnp.where(lane == 5, r1, 0.0))))))


def _outproj(att_p, yc_p, x_p, att_s, yc_s, x_s, w_top, w_bot, norm_g, rw, rb):
    tm = OUT_TM
    n_p = x_p.shape[0] // tm
    n_s = x_s.shape[0] // tm
    t_all = x_p.shape[0] + x_s.shape[0]
    prow = lambda i: (jnp.minimum(i, n_p - 1), 0)
    srow = lambda i: (jnp.maximum(i - n_p, 0), 0)
    row = lambda i: (i, 0)
    const = lambda i: (0, 0)
    kern = functools.partial(_outproj_kernel, n_prompt_tiles=n_p)
    return pl.pallas_call(
        kern,
        grid=(n_p + n_s,),
        in_specs=[
            pl.BlockSpec((tm, ATT_WIDTH), prow),
            pl.BlockSpec((tm, CONV_CH), prow),
            pl.BlockSpec((tm, D_MODEL), prow),
            pl.BlockSpec((tm, ATT_WIDTH), srow),
            pl.BlockSpec((tm, CONV_CH), srow),
            pl.BlockSpec((tm, D_MODEL), srow),
            pl.BlockSpec((ATT_WIDTH, D_MODEL), const),
            pl.BlockSpec((CONV_CH, D_MODEL), const),
            pl.BlockSpec((1, D_MODEL), const),
            pl.BlockSpec((D_MODEL, ROUTE_LANES), const),
            pl.BlockSpec((1, ROUTE_LANES), const),
        ],
        out_specs=[
            pl.BlockSpec((tm, D_MODEL), row),
            pl.BlockSpec((SLAB, tm, LANES), lambda i: (0, i, 0)),
            pl.BlockSpec((tm, ROUTE_LANES), row),
            pl.BlockSpec((1, ROUTE_LANES), const),
        ],
        out_shape=[
            jax.ShapeDtypeStruct((t_all, D_MODEL), F32),
            jax.ShapeDtypeStruct((SLAB, t_all, LANES), F32),
            jax.ShapeDtypeStruct((t_all, ROUTE_LANES), F32),
            jax.ShapeDtypeStruct((1, ROUTE_LANES), F32),
        ],
        scratch_shapes=[pltpu.VMEM((1, ROUTE_LANES), F32)],
        compiler_params=_params(("arbitrary",), 52),
        name="outproj_router",
    )(att_p, yc_p, x_p, att_s, yc_s, x_s, w_top, w_bot, norm_g, rw, rb)


def _dispatch_copy(xf_ref, xs_hbm, sem, src_row, dst_row):
    return pltpu.make_async_copy(xf_ref.at[:, src_row, :],
                                 xs_hbm.at[pl.ds(dst_row * SLAB, SLAB)], sem)


def _zero_block_copy(zero_s, xs_hbm, sem, blk):
    span = MOE_BM * SLAB
    return pltpu.make_async_copy(zero_s, xs_hbm.at[pl.ds(blk * span, span)], sem)


def _dispatch_kernel(dest_ref, zero_blk_ref, xf_ref, xs_hbm, zero_s, sem):
    tm = DISPATCH_TM

    @pl.when(pl.program_id(0) == 0)
    def _():
        zero_s[...] = jnp.zeros(zero_s.shape, F32)

        def fill(j, start):
            blk = zero_blk_ref[0, 0, j]

            @pl.when(blk >= 0)
            def _():
                copy = _zero_block_copy(zero_s, xs_hbm, sem, blk)
                if start:
                    copy.start()
                else:
                    copy.wait()

        lax.fori_loop(0, zero_blk_ref.shape[-1], lambda j, c: (fill(j, True), c)[1], 0)
        lax.fori_loop(0, zero_blk_ref.shape[-1], lambda j, c: (fill(j, False), c)[1], 0)

    def issue(t, carry):
        for k in range(TOP_K):
            _dispatch_copy(xf_ref, xs_hbm, sem, t, dest_ref[0, 0, TOP_K * t + k]).start()
        return carry

    def drain(t, carry):
        for k in range(TOP_K):
            _dispatch_copy(xf_ref, xs_hbm, sem, t, dest_ref[0, 0, TOP_K * t + k]).wait()
        return carry

    lax.fori_loop(0, tm, issue, 0)
    lax.fori_loop(0, tm, drain, 0)


def _dispatch(dest, zero_blk, xf, cap):
    t_all = xf.shape[1]
    tm = DISPATCH_TM
    dest3 = dest.reshape(t_all // tm, 1, TOP_K * tm)
    zero_blk3 = zero_blk.reshape(1, 1, -1)
    return pl.pallas_call(
        _dispatch_kernel,
        grid=(t_all // tm,),
        in_specs=[
            pl.BlockSpec((1, 1, TOP_K * tm), lambda i: (i, 0, 0), memory_space=pltpu.SMEM),
            pl.BlockSpec(zero_blk3.shape, lambda i: (0, 0, 0), memory_space=pltpu.SMEM),
            pl.BlockSpec((SLAB, tm, LANES), lambda i: (0, i, 0)),
        ],
        out_specs=pl.BlockSpec(memory_space=pl.ANY),
        out_shape=jax.ShapeDtypeStruct((cap * SLAB, LANES), F32),
        scratch_shapes=[pltpu.VMEM((MOE_BM * SLAB, LANES), F32), pltpu.SemaphoreType.DMA(())],
        compiler_params=_params(("arbitrary",), 24),
        name="moe_dispatch",
    )(dest3, zero_blk3, xf)


def _expert_weight_copies(hbm_refs, stage_refs, sems, e):
    return [pltpu.make_async_copy(hbm.at[e], stage, sems.at[n])
            for n, (hbm, stage) in enumerate(zip(hbm_refs, stage_refs))]


def _experts_kernel(blk_e_ref, first_ref, next_e_ref, nused_ref, x_ref, wg_hbm, wu_hbm, wd_hbm, o_ref,
                    wg_f, wu_f, wd_f, wgu_s, wd_s, sems):
    i = pl.program_id(0)
    hbm_refs = (wg_hbm, wu_hbm, wd_hbm)
    stage_refs = (wg_f, wu_f, wd_f)

    @pl.when(i == 0)
    def _():
        for copy in _expert_weight_copies(hbm_refs, stage_refs, sems, blk_e_ref[0]):
            copy.start()

    @pl.when(i < nused_ref[0])
    def _():
        @pl.when(first_ref[i] == 1)
        def _():
            for copy in _expert_weight_copies(hbm_refs, stage_refs, sems, blk_e_ref[i]):
                copy.wait()
            wgu_s[:, 0:D_EXPERT] = wg_f[...].astype(BF16)
            wgu_s[:, D_EXPERT:2 * D_EXPERT] = wu_f[...].astype(BF16)
            wd_s[...] = wd_f[...].astype(BF16)

            @pl.when(next_e_ref[i] >= 0)
            def _():
                for copy in _expert_weight_copies(hbm_refs, stage_refs, sems, next_e_ref[i]):
                    copy.start()

        x = _load_token_slabs(x_ref, MOE_BM).astype(BF16)
        gu = jnp.dot(x, wgu_s[...], preferred_element_type=F32)
        g = gu[:, 0:D_EXPERT]
        u = gu[:, D_EXPERT:2 * D_EXPERT]
        h = g / (1.0 + jnp.exp(-g)) * u
        _store_token_slabs(o_ref, jnp.dot(h.astype(BF16), wd_s[...], preferred_element_type=F32))

    @pl.when(i >= nused_ref[0])
    def _():
        o_ref[...] = jnp.zeros(o_ref.shape, F32)


def _experts(blk_e, first, next_e, nused, xs, w_gate, w_up, w_down):
    cap = xs.shape[0] // SLAB
    bm = MOE_BM
    rows = lambda i, be, fi, ne, nu: (jnp.minimum(i, nu[0] - 1), 0)
    grid_spec = pltpu.PrefetchScalarGridSpec(
        num_scalar_prefetch=4,
        grid=(cap // bm,),
        in_specs=[
            pl.BlockSpec((bm * SLAB, LANES), rows),
            pl.BlockSpec(memory_space=pl.ANY),
            pl.BlockSpec(memory_space=pl.ANY),
            pl.BlockSpec(memory_space=pl.ANY),
        ],
        out_specs=pl.BlockSpec((bm * SLAB, LANES), lambda i, be, fi, ne, nu: (i, 0)),
        scratch_shapes=[
            pltpu.VMEM((D_MODEL, D_EXPERT), F32),
            pltpu.VMEM((D_MODEL, D_EXPERT), F32),
            pltpu.VMEM((D_EXPERT, D_MODEL), F32),
            pltpu.VMEM((D_MODEL, 2 * D_EXPERT), BF16),
            pltpu.VMEM((D_EXPERT, D_MODEL), BF16),
            pltpu.SemaphoreType.DMA((3,)),
        ],
    )
    return pl.pallas_call(
        _experts_kernel,
        grid_spec=grid_spec,
        out_shape=jax.ShapeDtypeStruct((cap * SLAB, LANES), F32),
        compiler_params=_params(("arbitrary",), 48),
        name="moe_experts",
    )(blk_e, first, next_e, nused, xs, w_gate, w_up, w_down)


def _combine_copy(outs_hbm, g_s, sems, slot, src_row, k, t):
    return pltpu.make_async_copy(outs_hbm.at[pl.ds(src_row * SLAB, SLAB)],
                                 g_s.at[slot, k, :, t, :], sems.at[slot])


def _combine_kernel(dest_ref, dest_next_ref, x1_ref, route_ref, outs_hbm, y_ref, g_s, sems, *, n_tiles):
    i = pl.program_id(0)
    tm = x1_ref.shape[0]
    slot = i % 2

    def gather(table_ref, into, start):
        def body(t, carry):
            for k in range(TOP_K):
                copy = _combine_copy(outs_hbm, g_s, sems, into, table_ref[0, 0, TOP_K * t + k], k, t)
                if start:
                    copy.start()
                else:
                    copy.wait()
            return carry
        lax.fori_loop(0, tm, body, 0)

    @pl.when(i == 0)
    def _():
        gather(dest_ref, 0, True)

    @pl.when(i + 1 < n_tiles)
    def _():
        gather(dest_next_ref, 1 - slot, True)

    gather(dest_ref, slot, False)
    r = route_ref[...]
    w0 = jnp.broadcast_to(r[:, 2:3], (tm, LANES))
    w1 = jnp.broadcast_to(r[:, 3:4], (tm, LANES))
    for c in range(SLAB):
        cs = slice(c * LANES, (c + 1) * LANES)
        y_ref[:, cs] = x1_ref[:, cs] + w0 * g_s[slot, 0, c] + w1 * g_s[slot, 1, c]


def _combine(dest, x1, route, outs, row_off, t):
    tm = OUT_TM
    t_all = x1.shape[0]
    off = row_off // tm
    dest3 = dest.reshape(t_all // tm, 1, TOP_K * tm)
    n = t // tm
    return pl.pallas_call(
        functools.partial(_combine_kernel, n_tiles=n),
        grid=(n,),
        in_specs=[
            pl.BlockSpec((1, 1, TOP_K * tm), lambda i: (i + off, 0, 0), memory_space=pltpu.SMEM),
            pl.BlockSpec((1, 1, TOP_K * tm), lambda i: (jnp.minimum(i + 1, n - 1) + off, 0, 0),
                         memory_space=pltpu.SMEM),
            pl.BlockSpec((tm, D_MODEL), lambda i: (i + off, 0)),
            pl.BlockSpec((tm, ROUTE_LANES), lambda i: (i + off, 0)),
            pl.BlockSpec(memory_space=pl.ANY),
        ],
        out_specs=pl.BlockSpec((tm, D_MODEL), lambda i: (i, 0)),
        out_shape=jax.ShapeDtypeStruct((t, D_MODEL), F32),
        scratch_shapes=[pltpu.VMEM((2, TOP_K, SLAB, tm, LANES), F32), pltpu.SemaphoreType.DMA((2,))],
        compiler_params=_params(("arbitrary",), 32),
        name="moe_combine",
    )(dest3, dest3, x1, route, outs)


def _routing_tables(route, lane_counts, n_blocks):
    bm = MOE_BM
    flat_e = route[:, 0:TOP_K].astype(jnp.int32).reshape(-1)
    rank = route[:, 2 * TOP_K:3 * TOP_K].astype(jnp.int32).reshape(-1)
    counts = lane_counts[0, N_GROUPS:N_GROUPS + N_EXPERTS].astype(jnp.int32)
    padded = (counts + bm - 1) // bm * bm
    pad_end = jnp.cumsum(padded)
    pad_start = pad_end - padded
    dest = (pad_start[flat_e] + rank).astype(jnp.int32)
    nused = (pad_end[-1] // bm).astype(jnp.int32)
    blk = jnp.arange(n_blocks, dtype=jnp.int32)
    blk_e = jnp.searchsorted(pad_end, jnp.minimum(blk, nused - 1) * bm, side='right').astype(jnp.int32)
    blk_e = jnp.minimum(blk_e, N_EXPERTS - 1)
    first = jnp.concatenate([jnp.ones((1,), jnp.int32),
                             (blk_e[1:] != blk_e[:-1]).astype(jnp.int32)])
    experts = jnp.arange(N_EXPERTS, dtype=jnp.int32)
    later_owner = jnp.where((counts[None, :] > 0) & (experts[None, :] > experts[:, None]),
                            experts[None, :], N_EXPERTS)
    next_owner = jnp.min(later_owner, axis=1)
    next_owner = jnp.where(next_owner == N_EXPERTS, -1, next_owner).astype(jnp.int32)
    last_blk = jnp.where(counts > 0, pad_end // bm - 1, -1).astype(jnp.int32)
    tail = nused + jnp.arange(N_EXPERTS, dtype=jnp.int32)
    tail_blk = jnp.where(tail < n_blocks, tail, -1)
    zero_blk = jnp.concatenate([last_blk, tail_blk])
    return dest, zero_blk, blk_e, first, next_owner[blk_e], nused.reshape(1)


def _rope_tables(pos):
    half = HEAD_DIM // 2
    inv = ROPE_THETA ** (-jnp.arange(half, dtype=F32) / half)
    ang = pos.astype(F32)[:, None] * inv[None, :]
    cos = jnp.cos(ang)
    sin = jnp.sin(ang)
    cos_h = jnp.concatenate([cos, cos], axis=-1)
    sin_h = jnp.concatenate([-sin, sin], axis=-1)
    reps = LANES // HEAD_DIM
    return jnp.tile(cos_h, (1, reps)), jnp.tile(sin_h, (1, reps))


def kernel(x_prompt, x_sample, cache_k, cache_v, state_conv, norm_mix_g, w_in, q_norm_g, k_norm_g, lambda_q1, lambda_k1, lambda_q2, lambda_k2, subln_g, conv_w, conv_norm_g, w_out, norm_ffn_g, router_group_w, router_group_b, router_expert_w, router_expert_b, expert_w_gate, expert_w_up, expert_w_down):
    assert w_in.shape[0] == 1, "single-layer step"
    bp, sp, _ = x_prompt.shape
    bs, ss, _ = x_sample.shape
    past = cache_k.shape[2]
    assert bp == 1 and sp % ATT_T == 0
    tp = bp * sp
    ts = bs * ss
    t_all = tp + ts
    lambda_init = 0.8 - 0.6 * math.exp(-0.3 * 0)

    w_in_bf = w_in[0].astype(BF16)
    w_out_bf = w_out[0].astype(BF16)
    w_top, w_bot = w_out_bf[:ATT_WIDTH], w_out_bf[ATT_WIDTH:]
    ng = norm_mix_g[0].reshape(1, D_MODEL)
    qg = jnp.tile(q_norm_g[0], QK_WIDTH // HEAD_DIM).reshape(1, QK_WIDTH)
    kg = jnp.tile(k_norm_g[0], QK_WIDTH // HEAD_DIM).reshape(1, QK_WIDTH)
    head_of = jnp.arange(IN_CH, dtype=jnp.int32) // HEAD_DIM
    gmat = jnp.where(head_of[:, None] == head_of[None, :], 1.0 / HEAD_DIM, 0.0).astype(BF16)
    lamv = jnp.stack([lambda_q1[0], lambda_k1[0], lambda_q2[0], lambda_k2[0]]).astype(F32)
    sg = subln_g[0].reshape(1, V_DIM)
    cw = conv_w[0]
    cng = conv_norm_g[0].reshape(1, CONV_CH)
    cos_p, sin_p = _rope_tables(jnp.arange(sp, dtype=jnp.int32))
    cos_s, sin_s = _rope_tables(jnp.tile(past + jnp.arange(ss, dtype=jnp.int32), bs))

    zero_conv = jnp.zeros((1, CONV_K - 1, CONV_CH), F32)
    qt_p, kf_p, kb_p, vf_p, vt_p, yc_p, tail_p = _inproj(
        x_prompt.reshape(tp, D_MODEL), ng, w_in_bf, qg, kg, gmat, cos_p, sin_p, zero_conv, cw, cng,
        tm=ATT_T, nseq=1, carry=True, qv_transposed=True)
    att_p = _attn_prompt(qt_p, kb_p, vt_p, lamv, subln_g[0].reshape(V_DIM, 1), lambda_init)

    seqs_per_tile = OUT_TM // ss
    q_s, kf_s, kb_s, vf_s, vb_s, yc_s, tail_s = _inproj(
        x_sample.reshape(ts, D_MODEL), ng, w_in_bf, qg, kg, gmat, cos_s, sin_s, state_conv[0], cw, cng,
        tm=OUT_TM, nseq=seqs_per_tile, carry=False, qv_transposed=False)
    kt_cache = jnp.transpose(cache_k[0], (0, 2, 3, 4, 1)).reshape(bs, QK_WIDTH, past)
    att_s = _attn_sample(q_s, kt_cache, cache_v[0].reshape(bs, past * N_HEADS, V_DIM),
                         kb_s, vb_s, lamv, sg, lambda_init)

    rw = jnp.zeros((D_MODEL, ROUTE_LANES), F32)
    rw = rw.at[:, 0:N_GROUPS].set(router_group_w[0]).at[:, N_GROUPS:N_GROUPS + N_EXPERTS].set(router_expert_w[0])
    rb = jnp.zeros((1, ROUTE_LANES), F32)
    rb = rb.at[0, 0:N_GROUPS].set(router_group_b[0]).at[0, N_GROUPS:N_GROUPS + N_EXPERTS].set(router_expert_b[0])
    nf = norm_ffn_g[0].reshape(1, D_MODEL)
    rw_bf = rw.astype(BF16)
    x1, xf, route, lane_counts = _outproj(att_p, yc_p, x_prompt.reshape(tp, D_MODEL), att_s, yc_s,
                                          x_sample.reshape(ts, D_MODEL), w_top, w_bot, nf, rw_bf, rb)

    n = t_all * TOP_K
    n_blocks = n // MOE_BM + N_EXPERTS
    dest, zero_blk, blk_e, first, next_e, nused = _routing_tables(route, lane_counts, n_blocks)
    xs = _dispatch(dest, zero_blk, xf, n_blocks * MOE_BM)
    outs = _experts(blk_e, first, next_e, nused, xs, expert_w_gate[0], expert_w_up[0], expert_w_down[0])
    y_p = _combine(dest, x1, route, outs, 0, tp)
    y_s = _combine(dest, x1, route, outs, tp, ts)

    return (y_p.reshape(bp, sp, D_MODEL),
            y_s.reshape(bs, ss, D_MODEL),
            kf_p.reshape(1, bp, sp, N_HEADS, 2, HEAD_DIM),
            vf_p.reshape(1, bp, sp, N_HEADS, V_DIM),
            tail_p.reshape(1, bp, CONV_K - 1, CONV_CH),
            kf_s.reshape(1, bs, ss, N_HEADS, 2, HEAD_DIM),
            vf_s.reshape(1, bs, ss, N_HEADS, V_DIM),
            tail_s.reshape(1, bs, CONV_K - 1, CONV_CH))
```

```python
import functools
import math

import jax
import jax.numpy as jnp
from jax import lax
from jax.experimental import pallas as pl
from jax.experimental.pallas import tpu as pltpu

F32 = jnp.float32
BF16 = jnp.bfloat16

D_MODEL = 2048
CHUNK = 64
HEAD_DIM = 64
V_DIM = 2 * HEAD_DIM
N_HEADS = 8
QK_WIDTH = N_HEADS * 2 * HEAD_DIM
ATT_WIDTH = N_HEADS * V_DIM
CONV_CH = 1024
CONV_K = 3
ROPE_THETA = 10000.0
N_GROUPS = 8
EXPERTS_PER_GROUP = 8
N_EXPERTS = N_GROUPS * EXPERTS_PER_GROUP
TOP_K = 2
D_EXPERT = D_MODEL // 4
NORM_EPS = 1e-6
SECTION = 1024
N_SECTIONS = 6

LANES = 128
IN_CH = 256
ATT_AHEAD = 2
ATT_ONES = 16
ATT_UNROLL = 4
ATT_CB = 512
ATT_T = 512
OUT_TM = 256
MOE_BM = 256
DISPATCH_TM = 512
ROW_DMA_UNROLL = 8
SLAB = D_MODEL // LANES
ROUTE_LANES = LANES
MASKED = -1e30
Q_SCALE = HEAD_DIM ** -0.5 * math.log2(math.e)
MIB = 1024 * 1024


def _store_token_slabs(ref, x):
    rows = x.shape[0]
    for c in range(SLAB):
        ref[pl.ds(c, rows, stride=SLAB), :] = x[:, c * LANES:(c + 1) * LANES]


def _load_token_slabs(ref, rows):
    return jnp.concatenate([ref[pl.ds(c, rows, stride=SLAB), :] for c in range(SLAB)], axis=1)


def _params(sem, vmem_mib):
    return pltpu.CompilerParams(dimension_semantics=sem, vmem_limit_bytes=vmem_mib * MIB)


def _inproj_kernel(x_ref, ng_ref, w_ref, qg_ref, kg_ref, gmat_ref, cos_ref, sin_ref,
                   prev_ref, cw_ref, cng_ref,
                   q_ref, kf_ref, kb_ref, vf_ref, vb_ref, yc_ref, tail_ref,
                   xn_s, gb_s, c_s, z_s, u_s, *, nseq, carry, qv_transposed):
    i = pl.program_id(0)
    j = pl.program_id(1)
    tm = x_ref.shape[0]
    seq = tm // nseq

    @pl.when(j == 0)
    def _():
        x = x_ref[...]
        inv = lax.rsqrt(jnp.mean(x * x, axis=-1, keepdims=True) + NORM_EPS)
        xn_s[...] = (x * inv * ng_ref[...]).astype(BF16)
        if carry:
            @pl.when(i == 0)
            def _():
                u_s[6:8, :] = prev_ref[0]

    def chunk_dot(c):
        return jnp.dot(xn_s[...], w_ref[:, c * IN_CH:(c + 1) * IN_CH], preferred_element_type=F32)

    def pipelined(epilogue):
        acc = chunk_dot(0)
        for c in range(SECTION // IN_CH):
            nxt = chunk_dot(c + 1) if c + 1 < SECTION // IN_CH else None
            epilogue(acc, slice(c * IN_CH, (c + 1) * IN_CH))
            acc = nxt

    def head_norm_rope(a, g):
        ms = jnp.dot((a * a).astype(BF16), gmat_ref[...], preferred_element_type=F32)
        y = a * lax.rsqrt(ms + NORM_EPS) * g
        reps = IN_CH // LANES
        cos = jnp.concatenate([cos_ref[...]] * reps, axis=1)
        sin = jnp.concatenate([sin_ref[...]] * reps, axis=1)
        lane = lax.broadcasted_iota(jnp.int32, y.shape, 1)
        first = (lane & (HEAD_DIM - 1)) < HEAD_DIM // 2
        partner = jnp.where(first, pltpu.roll(y, IN_CH - HEAD_DIM // 2, 1),
                            pltpu.roll(y, HEAD_DIM // 2, 1))
        return y * cos + partner * sin

    for section in range(N_SECTIONS):

        def epilogue(acc, cols, section=section):
            if section == 0:
                q = head_norm_rope(acc, qg_ref[:, cols]) * Q_SCALE
                if qv_transposed:
                    q_ref[0, cols, :] = q.T.astype(BF16)
                else:
                    q_ref[:, cols] = q.astype(BF16)
            elif section == 1:
                k = head_norm_rope(acc, kg_ref[:, cols])
                kf_ref[:, cols] = k
                kb_ref[:, cols] = k.astype(BF16)
            elif section == 2:
                vf_ref[:, cols] = acc
                if qv_transposed:
                    vb_ref[0, cols, :] = acc.T.astype(BF16)
                else:
                    vb_ref[:, cols] = acc.astype(BF16)
            elif section == 3:
                gb_s[:, cols] = acc
            elif section == 4:
                c_s[:, cols] = acc
            else:
                u = c_s[:, cols] * acc
                w0 = cw_ref[0:1, cols]
                w1 = cw_ref[1:2, cols]
                w2 = cw_ref[2:3, cols]
                for s in range(nseq):
                    rows = slice(s * seq, (s + 1) * seq)
                    if not carry:
                        u_s[6:8, cols] = prev_ref[s, :, cols]
                    u_s[8:8 + seq, cols] = u[rows]
                    conv = (w0 * u_s[6:6 + seq, cols] + w1 * u_s[7:7 + seq, cols]
                            + w2 * u_s[8:8 + seq, cols])
                    z_s[rows, cols] = gb_s[rows, cols] * conv
                    tail = u_s[seq + 6:seq + 8, cols]
                    tail_ref[s, :, cols] = tail
                    if carry:
                        u_s[6:8, cols] = tail

        @pl.when(j == section)
        def _(section=section, epilogue=epilogue):
            pipelined(epilogue)
            if section == N_SECTIONS - 1:
                z = z_s[...]
                inv = lax.rsqrt(jnp.mean(z * z, axis=-1, keepdims=True) + NORM_EPS)
                yc_ref[...] = (z * inv * cng_ref[...]).astype(BF16)


def _inproj(x2d, norm_g, w_in_bf, qg, kg, gmat, cos, sin, conv_prev, conv_w, conv_norm_g,
            *, tm, nseq, carry, qv_transposed):
    t = x2d.shape[0]
    ni = t // tm
    seq = tm // nseq
    row = lambda i, j: (i, 0)
    const = lambda i, j: (0, 0)
    if qv_transposed:
        qv_shape = jax.ShapeDtypeStruct((ni, QK_WIDTH, tm), BF16)
        qv_spec = pl.BlockSpec((1, QK_WIDTH, tm), lambda i, j: (i, 0, 0))
    else:
        qv_shape = jax.ShapeDtypeStruct((t, QK_WIDTH), BF16)
        qv_spec = pl.BlockSpec((tm, QK_WIDTH), row)
    if carry:
        prev_spec = pl.BlockSpec((1, CONV_K - 1, CONV_CH), lambda i, j: (0, 0, 0))
        tail_shape = jax.ShapeDtypeStruct((1, CONV_K - 1, CONV_CH), F32)
        tail_spec = pl.BlockSpec((1, CONV_K - 1, CONV_CH), lambda i, j: (0, 0, 0))
    else:
        prev_spec = pl.BlockSpec((nseq, CONV_K - 1, CONV_CH), lambda i, j: (i, 0, 0))
        tail_shape = jax.ShapeDtypeStruct((ni * nseq, CONV_K - 1, CONV_CH), F32)
        tail_spec = pl.BlockSpec((nseq, CONV_K - 1, CONV_CH), lambda i, j: (i, 0, 0))
    kern = functools.partial(_inproj_kernel, nseq=nseq, carry=carry, qv_transposed=qv_transposed)
    return pl.pallas_call(
        kern,
        grid=(ni, N_SECTIONS),
        in_specs=[
            pl.BlockSpec((tm, D_MODEL), row),
            pl.BlockSpec((1, D_MODEL), const),
            pl.BlockSpec((D_MODEL, SECTION), lambda i, j: (0, j)),
            pl.BlockSpec((1, SECTION), const),
            pl.BlockSpec((1, SECTION), const),
            pl.BlockSpec((IN_CH, IN_CH), const),
            pl.BlockSpec((tm, LANES), row),
            pl.BlockSpec((tm, LANES), row),
            prev_spec,
            pl.BlockSpec((CONV_K, CONV_CH), const),
            pl.BlockSpec((1, CONV_CH), const),
        ],
        out_specs=[
            qv_spec,
            pl.BlockSpec((tm, QK_WIDTH), row),
            pl.BlockSpec((tm, QK_WIDTH), row),
            pl.BlockSpec((tm, ATT_WIDTH), row),
            qv_spec,
            pl.BlockSpec((tm, CONV_CH), row),
            tail_spec,
        ],
        out_shape=[
            qv_shape,
            jax.ShapeDtypeStruct((t, QK_WIDTH), F32),
            jax.ShapeDtypeStruct((t, QK_WIDTH), BF16),
            jax.ShapeDtypeStruct((t, ATT_WIDTH), F32),
            qv_shape,
            jax.ShapeDtypeStruct((t, CONV_CH), BF16),
            tail_shape,
        ],
        scratch_shapes=[
            pltpu.VMEM((tm, D_MODEL), BF16),
            pltpu.VMEM((tm, CONV_CH), F32),
            pltpu.VMEM((tm, CONV_CH), F32),
            pltpu.VMEM((tm, CONV_CH), F32),
            pltpu.VMEM((seq + 8, CONV_CH), F32),
        ],
        compiler_params=_params(("arbitrary", "arbitrary"), 56),
        name="inproj_carry" if carry else "inproj_seqs",
    )(x2d, norm_g, w_in_bf, qg, kg, gmat, cos, sin, conv_prev, conv_w, conv_norm_g)


def _lambda_full(lamv_ref, lambda_init):
    lv = lamv_ref[...]
    a = jnp.sum(lv[0:1] * lv[1:2], axis=-1, keepdims=True)
    b = jnp.sum(lv[2:3] * lv[3:4], axis=-1, keepdims=True)
    return jnp.exp(a) - jnp.exp(b) + lambda_init


def _split_heads(q):
    lane = lax.broadcasted_iota(jnp.int32, q.shape, 1)
    zero = jnp.zeros_like(q)
    return jnp.where(lane < HEAD_DIM, q, zero), jnp.where(lane >= HEAD_DIM, q, zero)


def _diff_finish(acc, l, lam, g, lambda_init, rows):
    a = acc / l
    d = a[0:rows] - lam * a[rows:2 * rows]
    inv = lax.rsqrt(jnp.mean(d * d, axis=-1, keepdims=True) + NORM_EPS)
    return d * inv * g * (1.0 - lambda_init)


def _attn_prompt_kernel(qt_ref, k_ref, vt_ref, lamv_ref, g_ref, o_ref, qq_s, m_s, acc_s, s_s,
                        *, lambda_init):
    i = pl.program_id(1)
    t = ATT_T
    qt = qt_ref[0]
    feat = lax.broadcasted_iota(jnp.int32, qt.shape, 0)
    zero = jnp.zeros_like(qt)
    qq_s[:, 0:t] = jnp.where(feat < HEAD_DIM, qt, zero)
    qq_s[:, t:2 * t] = jnp.where(feat >= HEAD_DIM, qt, zero)
    m_s[...] = jnp.full(m_s.shape, MASKED, F32)
    acc_s[...] = jnp.zeros(acc_s.shape, F32)
    ones_rows = (lax.broadcasted_iota(jnp.int32, (ATT_ONES, t), 0) == 0).astype(BF16)

    nblk = 2 * t // ATT_CB

    def scores(kt, cb):
        k = k_ref[pl.ds(pl.multiple_of(kt * t, t), t), :]
        s = jnp.dot(k, qq_s[:, cb * ATT_CB:(cb + 1) * ATT_CB], preferred_element_type=F32)
        s_s[cb] = s.astype(BF16)

    def step(kt, diagonal):
        vt = jnp.concatenate([vt_ref[kt], ones_rows], axis=0)
        for cb in range(nblk):
            cs = slice(cb * ATT_CB, (cb + 1) * ATT_CB)
            s = s_s[cb]
            if diagonal:
                key = lax.broadcasted_iota(jnp.int32, s.shape, 0)
                qry = lax.broadcasted_iota(jnp.int32, s.shape, 1) + (cb * ATT_CB) % t
                s = s + jnp.where((key // CHUNK) <= (qry // CHUNK), 0.0, MASKED).astype(BF16)
            m_prev = m_s[:, cs]
            m_new = jnp.maximum(m_prev, jnp.max(s, axis=0, keepdims=True).astype(F32))
            alpha = jnp.exp2(m_prev - m_new)
            p = jnp.exp2(s - m_new.astype(BF16))
            m_s[:, cs] = m_new
            ahead = cb + ATT_AHEAD
            if ahead < nblk:
                scores(kt, ahead)
            elif not diagonal:
                scores(kt + 1, ahead - nblk)
            acc_s[:, cs] = alpha * acc_s[:, cs] + jnp.dot(vt, p, preferred_element_type=F32)

    def group_body(g, carry):
        for u in range(ATT_UNROLL):
            step(ATT_UNROLL * g + u, False)
        return carry

    def single_body(kt, carry):
        step(kt, False)
        return carry

    for cb in range(ATT_AHEAD):
        scores(0, cb)
    n_grouped = i // ATT_UNROLL * ATT_UNROLL
    lax.fori_loop(0, i // ATT_UNROLL, group_body, 0)
    lax.fori_loop(n_grouped, i, single_body, 0)
    step(i, True)
    lam = _lambda_full(lamv_ref, lambda_init)
    a = acc_s[0:V_DIM, :] / acc_s[V_DIM:V_DIM + 1, :]
    d = a[:, 0:t] - lam * a[:, t:2 * t]
    inv = lax.rsqrt(jnp.mean(d * d, axis=0, keepdims=True) + NORM_EPS)
    y = d * inv * g_ref[...] * (1.0 - lambda_init)
    o_ref[...] = y.T.astype(BF16)


def _attn_prompt(qt_bf, k_bf, vt_bf, lamv, subln_g_col, lambda_init):
    nq = qt_bf.shape[0]
    t = nq * ATT_T
    kern = functools.partial(_attn_prompt_kernel, lambda_init=lambda_init)
    return pl.pallas_call(
        kern,
        grid=(N_HEADS, nq),
        in_specs=[
            pl.BlockSpec((1, V_DIM, ATT_T), lambda h, i: (i, h, 0)),
            pl.BlockSpec((t, V_DIM), lambda h, i: (0, h)),
            pl.BlockSpec((nq, V_DIM, ATT_T), lambda h, i: (0, h, 0)),
            pl.BlockSpec((4, HEAD_DIM), lambda h, i: (0, 0)),
            pl.BlockSpec((V_DIM, 1), lambda h, i: (0, 0)),
        ],
        out_specs=pl.BlockSpec((ATT_T, V_DIM), lambda h, i: (i, h)),
        out_shape=jax.ShapeDtypeStruct((t, ATT_WIDTH), BF16),
        scratch_shapes=[
            pltpu.VMEM((V_DIM, 2 * ATT_T), BF16),
            pltpu.VMEM((1, 2 * ATT_T), F32),
            pltpu.VMEM((V_DIM + ATT_ONES, 2 * ATT_T), F32),
            pltpu.VMEM((2 * ATT_T // ATT_CB, ATT_T, ATT_CB), BF16),
        ],
        compiler_params=_params(("arbitrary", "arbitrary"), 48),
        name="attn_prompt",
    )(qt_bf, k_bf, vt_bf, lamv, subln_g_col)


def _attn_sample_kernel(q_ref, kc_ref, vc_ref, kn_ref, vn_ref, lamv_ref, g_ref, o_ref,
                        *, lambda_init, past):
    rows = q_ref.shape[0]
    lam = _lambda_full(lamv_ref, lambda_init)
    contract_last = (((1,), (1,)), ((), ()))
    for h in range(N_HEADS):
        hs = slice(h * V_DIM, (h + 1) * V_DIM)
        q1, q2 = _split_heads(q_ref[:, hs])
        qq = jnp.concatenate([q1, q2], axis=0)
        s_c = jnp.dot(qq, kc_ref[0, hs, :].astype(BF16), preferred_element_type=F32)
        s_n = lax.dot_general(qq, kn_ref[:, hs], contract_last, preferred_element_type=F32)
        r = lax.broadcasted_iota(jnp.int32, s_n.shape, 0)
        c = lax.broadcasted_iota(jnp.int32, s_n.shape, 1)
        s_n = jnp.where(((past + c) // CHUNK) <= ((past + r % rows) // CHUNK), s_n, MASKED)
        m = jnp.maximum(jnp.max(s_c, axis=-1, keepdims=True), jnp.max(s_n, axis=-1, keepdims=True))
        p_c = jnp.exp2(s_c - m)
        p_n = jnp.exp2(s_n - m)
        l = jnp.sum(p_c, axis=-1, keepdims=True) + jnp.sum(p_n, axis=-1, keepdims=True)
        v_c = vc_ref[0, pl.ds(h, past, stride=N_HEADS), :].astype(BF16)
        acc = (jnp.dot(p_c.astype(BF16), v_c, preferred_element_type=F32)
               + jnp.dot(p_n.astype(BF16), vn_ref[:, hs], preferred_element_type=F32))
        o_ref[:, hs] = _diff_finish(acc, l, lam, g_ref[...], lambda_init, rows).astype(BF16)


def _attn_sample(q_bf, kt_cache, v_cache, k_bf, v_bf, lamv, subln_g, lambda_init):
    nb, _, past = kt_cache.shape
    rows = q_bf.shape[0] // nb
    kern = functools.partial(_attn_sample_kernel, lambda_init=lambda_init, past=past)
    new_spec = pl.BlockSpec((rows, ATT_WIDTH), lambda b: (b, 0))
    return pl.pallas_call(
        kern,
        grid=(nb,),
        in_specs=[
            new_spec,
            pl.BlockSpec((1, QK_WIDTH, past), lambda b: (b, 0, 0)),
            pl.BlockSpec((1, past * N_HEADS, V_DIM), lambda b: (b, 0, 0)),
            new_spec, new_spec,
            pl.BlockSpec((4, HEAD_DIM), lambda b: (0, 0)),
            pl.BlockSpec((1, V_DIM), lambda b: (0, 0)),
        ],
        out_specs=new_spec,
        out_shape=jax.ShapeDtypeStruct(q_bf.shape, BF16),
        compiler_params=_params(("arbitrary",), 48),
        name="attn_sample",
    )(q_bf, kt_cache, v_cache, k_bf, v_bf, lamv, subln_g)


def _outproj_kernel(att_p_ref, yc_p_ref, x_p_ref, att_s_ref, yc_s_ref, x_s_ref,
                    wt_ref, wb_ref, ng_ref, rw_ref, rb_ref, x1_ref, xf_ref, route_ref, count_ref, count_s,
                    *, n_prompt_tiles):
    i = pl.program_id(0)
    shared = (wt_ref, wb_ref, ng_ref, rw_ref, rb_ref, x1_ref, xf_ref, route_ref, count_ref, count_s)

    @pl.when(i == 0)
    def _():
        count_s[...] = jnp.zeros(count_s.shape, F32)

    @pl.when(i < n_prompt_tiles)
    def _():
        _outproj_tile(att_p_ref, yc_p_ref, x_p_ref, *shared)

    @pl.when(i >= n_prompt_tiles)
    def _():
        _outproj_tile(att_s_ref, yc_s_ref, x_s_ref, *shared)


def _outproj_tile(att_ref, yc_ref, x_ref, wt_ref, wb_ref, ng_ref, rw_ref, rb_ref,
                  x1_ref, xf_ref, route_ref, count_ref, count_s):
    o = (jnp.dot(att_ref[...], wt_ref[...], preferred_element_type=F32)
         + jnp.dot(yc_ref[...], wb_ref[...], preferred_element_type=F32))
    x1 = x_ref[...] + o
    x1_ref[...] = x1
    xf = x1 * lax.rsqrt(jnp.mean(x1 * x1, axis=-1, keepdims=True) + NORM_EPS) * ng_ref[...]
    for c in range(SLAB):
        xf_ref[c] = xf[:, c * LANES:(c + 1) * LANES]
    logits =jnp.dot(xf.astype(BF16), rw_ref[...], preferred_element_type=F32) + rb_ref[...]

    lane = lax.broadcasted_iota(jnp.int32, logits.shape, 1)
    neg = -jnp.inf
    is_group = lane < N_GROUPS
    gl = jnp.where(is_group, logits, neg)
    gmax = jnp.max(gl, axis=-1, keepdims=True)
    grp = jnp.min(jnp.where(gl == gmax, lane, ROUTE_LANES), axis=-1, keepdims=True)
    gsum = jnp.sum(jnp.where(is_group, jnp.exp(gl - gmax), 0.0), axis=-1, keepdims=True)
    g_w = 1.0 / gsum
    e_lane = lane - N_GROUPS
    in_grp = (e_lane >= 0) & (e_lane < N_EXPERTS) & ((e_lane // EXPERTS_PER_GROUP) == grp)
    el = jnp.where(in_grp, logits, neg)
    t1 = jnp.max(el, axis=-1, keepdims=True)
    i1 = jnp.min(jnp.where(el == t1, lane, ROUTE_LANES), axis=-1, keepdims=True)
    el2 = jnp.where(lane == i1, neg, el)
    t2 = jnp.max(el2, axis=-1, keepdims=True)
    i2 = jnp.min(jnp.where(el2 == t2, lane, ROUTE_LANES), axis=-1, keepdims=True)
    r21 = jnp.exp(t2 - t1)
    w0 = g_w / (1.0 + r21)
    w1 = g_w * r21 / (1.0 + r21)
    e0 = (i1 - N_GROUPS).astype(F32)
    e1 = (i2 - N_GROUPS).astype(F32)
    rows = logits.shape[0]
    picked0 = lane == i1
    picked1 = lane == i2
    member = (picked0 | picked1).astype(BF16)
    earlier = (lax.broadcasted_iota(jnp.int32, (rows, rows), 1)
               < lax.broadcasted_iota(jnp.int32, (rows, rows), 0)).astype(BF16)
    before = jnp.dot(earlier, member, preferred_element_type=F32) + count_s[...]
    r0 = jnp.sum(jnp.where(picked0, before, 0.0), axis=-1, keepdims=True)
    r1 = jnp.sum(jnp.where(picked1, before, 0.0), axis=-1, keepdims=True)
    count_s[...] += jnp.sum(member.astype(F32), axis=0, keepdims=True)
    count_ref[...] = count_s[...]
    route_ref[...] = jnp.where(lane == 0, e0, jnp.where(lane == 1, e1,
                               jnp.where(lane == 2, w0, jnp.where(lane == 3, w1,
                               jnp.where(lane == 4, r0, jnp.where(lane == 5, r1, 0.0))))))


def _outproj(att_p, yc_p, x_p, att_s, yc_s, x_s, w_top, w_bot, norm_g, rw, rb):
    tm = OUT_TM
    n_p = x_p.shape[0] // tm
    n_s = x_s.shape[0] // tm
    t_all = x_p.shape[0] + x_s.shape[0]
    prow = lambda i: (jnp.minimum(i, n_p - 1), 0)
    srow = lambda i: (jnp.maximum(i - n_p, 0), 0)
    row = lambda i: (i, 0)
    const = lambda i: (0, 0)
    kern = functools.partial(_outproj_kernel, n_prompt_tiles=n_p)
    return pl.pallas_call(
        kern,
        grid=(n_p + n_s,),
        in_specs=[
            pl.BlockSpec((tm, ATT_WIDTH), prow),
            pl.BlockSpec((tm, CONV_CH), prow),
            pl.BlockSpec((tm, D_MODEL), prow),
            pl.BlockSpec((tm, ATT_WIDTH), srow),
            pl.BlockSpec((tm, CONV_CH), srow),
            pl.BlockSpec((tm, D_MODEL), srow),
            pl.BlockSpec((ATT_WIDTH, D_MODEL), const),
            pl.BlockSpec((CONV_CH, D_MODEL), const),
            pl.BlockSpec((1, D_MODEL), const),
            pl.BlockSpec((D_MODEL, ROUTE_LANES), const),
            pl.BlockSpec((1, ROUTE_LANES), const),
        ],
        out_specs=[
            pl.BlockSpec((tm, D_MODEL), row),
            pl.BlockSpec((SLAB, tm, LANES), lambda i: (0, i, 0)),
            pl.BlockSpec((tm, ROUTE_LANES), row),
            pl.BlockSpec((1, ROUTE_LANES), const),
        ],
        out_shape=[
            jax.ShapeDtypeStruct((t_all, D_MODEL), F32),
            jax.ShapeDtypeStruct((SLAB, t_all, LANES), F32),
            jax.ShapeDtypeStruct((t_all, ROUTE_LANES), F32),
            jax.ShapeDtypeStruct((1, ROUTE_LANES), F32),
        ],
        scratch_shapes=[pltpu.VMEM((1, ROUTE_LANES), F32)],
        compiler_params=_params(("arbitrary",), 52),
        name="outproj_router",
    )(att_p, yc_p, x_p, att_s, yc_s, x_s, w_top, w_bot, norm_g, rw, rb)


def _dispatch_copy(xf_ref, xs_hbm, sem, src_row, dst_row):
    return pltpu.make_async_copy(xf_ref.at[:, src_row, :],
                                 xs_hbm.at[pl.ds(dst_row * SLAB, SLAB)], sem)


def _zero_block_copy(zero_s, xs_hbm, sem, blk):
    span = MOE_BM * SLAB
    return pltpu.make_async_copy(zero_s, xs_hbm.at[pl.ds(blk * span, span)], sem)


def _dispatch_kernel(dest_ref, zero_blk_ref, xf_ref, xs_hbm, zero_s, sem):
    tm = DISPATCH_TM

    @pl.when(pl.program_id(0) == 0)
    def _():
        zero_s[...] = jnp.zeros(zero_s.shape, F32)

        def fill(j, start):
            blk = zero_blk_ref[0, 0, j]

            @pl.when(blk >= 0)
            def _():
                copy = _zero_block_copy(zero_s, xs_hbm, sem, blk)
                if start:
                    copy.start()
                else:
                    copy.wait()

        lax.fori_loop(0, zero_blk_ref.shape[-1], lambda j, c: (fill(j, True), c)[1], 0)
        lax.fori_loop(0, zero_blk_ref.shape[-1], lambda j, c: (fill(j, False), c)[1], 0)

    def issue(t, carry):
        for k in range(TOP_K):
            _dispatch_copy(xf_ref, xs_hbm, sem, t, dest_ref[0, 0, TOP_K * t + k]).start()
        return carry

    def drain(t, carry):
        for k in range(TOP_K):
            _dispatch_copy(xf_ref, xs_hbm, sem, t, dest_ref[0, 0, TOP_K * t + k]).wait()
        return carry

    lax.fori_loop(0, tm, issue, 0, unroll=ROW_DMA_UNROLL)
    lax.fori_loop(0, tm, drain, 0, unroll=ROW_DMA_UNROLL)


def _dispatch(dest, zero_blk, xf, cap):
    t_all = xf.shape[1]
    tm = DISPATCH_TM
    dest3 = dest.reshape(t_all // tm, 1, TOP_K * tm)
    zero_blk3 = zero_blk.reshape(1, 1, -1)
    return pl.pallas_call(
        _dispatch_kernel,
        grid=(t_all // tm,),
        in_specs=[
            pl.BlockSpec((1, 1, TOP_K * tm), lambda i: (i, 0, 0), memory_space=pltpu.SMEM),
            pl.BlockSpec(zero_blk3.shape, lambda i: (0, 0, 0), memory_space=pltpu.SMEM),
            pl.BlockSpec((SLAB, tm, LANES), lambda i: (0, i, 0)),
        ],
        out_specs=pl.BlockSpec(memory_space=pl.ANY),
        out_shape=jax.ShapeDtypeStruct((cap * SLAB, LANES), F32),
        scratch_shapes=[pltpu.VMEM((MOE_BM * SLAB, LANES), F32), pltpu.SemaphoreType.DMA(())],
        compiler_params=_params(("arbitrary",), 24),
        name="moe_dispatch",
    )(dest3, zero_blk3, xf)


def _expert_weight_copies(hbm_refs, stage_refs, sems, e):
    return [pltpu.make_async_copy(hbm.at[e], stage, sems.at[n])
            for n, (hbm, stage) in enumerate(zip(hbm_refs, stage_refs))]


def _experts_kernel(blk_e_ref, first_ref, next_e_ref, nused_ref, x_ref, wg_hbm, wu_hbm, wd_hbm, o_ref,
                    wg_f, wu_f, wd_f, wgu_s, wd_s, sems):
    i = pl.program_id(0)
    hbm_refs = (wg_hbm, wu_hbm, wd_hbm)
    stage_refs = (wg_f, wu_f, wd_f)

    @pl.when(i == 0)
    def _():
        for copy in _expert_weight_copies(hbm_refs, stage_refs, sems, blk_e_ref[0]):
            copy.start()

    @pl.when(i < nused_ref[0])
    def _():
        @pl.when(first_ref[i] == 1)
        def _():
            for copy in _expert_weight_copies(hbm_refs, stage_refs, sems, blk_e_ref[i]):
                copy.wait()
            wgu_s[:, 0:D_EXPERT] = wg_f[...].astype(BF16)
            wgu_s[:, D_EXPERT:2 * D_EXPERT] = wu_f[...].astype(BF16)
            wd_s[...] = wd_f[...].astype(BF16)

            @pl.when(next_e_ref[i] >= 0)
            def _():
                for copy in _expert_weight_copies(hbm_refs, stage_refs, sems, next_e_ref[i]):
                    copy.start()

        x = _load_token_slabs(x_ref, MOE_BM).astype(BF16)
        gu = jnp.dot(x, wgu_s[...], preferred_element_type=F32)
        g = gu[:, 0:D_EXPERT]
        u = gu[:, D_EXPERT:2 * D_EXPERT]
        h = g / (1.0 + jnp.exp(-g)) * u
        _store_token_slabs(o_ref, jnp.dot(h.astype(BF16), wd_s[...], preferred_element_type=F32))

    @pl.when(i >= nused_ref[0])
    def _():
        o_ref[...] = jnp.zeros(o_ref.shape, F32)


def _experts(blk_e, first, next_e, nused, xs, w_gate, w_up, w_down):
    cap = xs.shape[0] // SLAB
    bm = MOE_BM
    rows = lambda i, be, fi, ne, nu: (jnp.minimum(i, nu[0] - 1), 0)
    grid_spec = pltpu.PrefetchScalarGridSpec(
        num_scalar_prefetch=4,
        grid=(cap // bm,),
        in_specs=[
            pl.BlockSpec((bm * SLAB, LANES), rows),
            pl.BlockSpec(memory_space=pl.ANY),
            pl.BlockSpec(memory_space=pl.ANY),
            pl.BlockSpec(memory_space=pl.ANY),
        ],
        out_specs=pl.BlockSpec((bm * SLAB, LANES), lambda i, be, fi, ne, nu: (i, 0)),
        scratch_shapes=[
            pltpu.VMEM((D_MODEL, D_EXPERT), F32),
            pltpu.VMEM((D_MODEL, D_EXPERT), F32),
            pltpu.VMEM((D_EXPERT, D_MODEL), F32),
            pltpu.VMEM((D_MODEL, 2 * D_EXPERT), BF16),
            pltpu.VMEM((D_EXPERT, D_MODEL), BF16),
            pltpu.SemaphoreType.DMA((3,)),
        ],
    )
    return pl.pallas_call(
        _experts_kernel,
        grid_spec=grid_spec,
        out_shape=jax.ShapeDtypeStruct((cap * SLAB, LANES), F32),
        compiler_params=_params(("arbitrary",), 48),
        name="moe_experts",
    )(blk_e, first, next_e, nused, xs, w_gate, w_up, w_down)


def _combine_copy(outs_hbm, g_s, sems, slot, src_row, k, t):
    return pltpu.make_async_copy(outs_hbm.at[pl.ds(src_row * SLAB, SLAB)],
                                 g_s.at[slot, k, :, t, :], sems.at[slot])


def _combine_kernel(dest_ref, dest_next_ref, x1_ref, route_ref, outs_hbm, y_ref, g_s, sems, *, n_tiles):
    i = pl.program_id(0)
    tm = x1_ref.shape[0]
    slot = i % 2

    def gather(table_ref, into, start):
        def body(t, carry):
            for k in range(TOP_K):
                copy = _combine_copy(outs_hbm, g_s, sems, into, table_ref[0, 0, TOP_K * t + k], k, t)
                if start:
                    copy.start()
                else:
                    copy.wait()
            return carry
        lax.fori_loop(0, tm, body, 0, unroll=ROW_DMA_UNROLL)

    @pl.when(i == 0)
    def _():
        gather(dest_ref, 0, True)

    @pl.when(i + 1 < n_tiles)
    def _():
        gather(dest_next_ref, 1 - slot, True)

    gather(dest_ref, slot, False)
    r = route_ref[...]
    w0 = jnp.broadcast_to(r[:, 2:3], (tm, LANES))
    w1 = jnp.broadcast_to(r[:, 3:4], (tm, LANES))
    for c in range(SLAB):
        cs = slice(c * LANES, (c + 1) * LANES)
        y_ref[:, cs] = x1_ref[:, cs] + w0 * g_s[slot, 0, c] + w1 * g_s[slot, 1, c]


def _combine(dest, x1, route, outs, row_off, t):
    tm = OUT_TM
    t_all = x1.shape[0]
    off = row_off // tm
    dest3 = dest.reshape(t_all // tm, 1, TOP_K * tm)
    n = t // tm
    return pl.pallas_call(
        functools.partial(_combine_kernel, n_tiles=n),
        grid=(n,),
        in_specs=[
            pl.BlockSpec((1, 1, TOP_K * tm), lambda i: (i + off, 0, 0), memory_space=pltpu.SMEM),
            pl.BlockSpec((1, 1, TOP_K * tm), lambda i: (jnp.minimum(i + 1, n - 1) + off, 0, 0),
                         memory_space=pltpu.SMEM),
            pl.BlockSpec((tm, D_MODEL), lambda i: (i + off, 0)),
            pl.BlockSpec((tm, ROUTE_LANES), lambda i: (i + off, 0)),
            pl.BlockSpec(memory_space=pl.ANY),
        ],
        out_specs=pl.BlockSpec((tm, D_MODEL), lambda i: (i, 0)),
        out_shape=jax.ShapeDtypeStruct((t, D_MODEL), F32),
        scratch_shapes=[pltpu.VMEM((2, TOP_K, SLAB, tm, LANES), F32), pltpu.SemaphoreType.DMA((2,))],
        compiler_params=_params(("arbitrary",), 32),
        name="moe_combine",
    )(dest3, dest3, x1, route, outs)


def _routing_tables(route, lane_counts, n_blocks):
    bm = MOE_BM
    flat_e = route[:, 0:TOP_K].astype(jnp.int32).reshape(-1)
    rank = route[:, 2 * TOP_K:3 * TOP_K].astype(jnp.int32).reshape(-1)
    counts = lane_counts[0, N_GROUPS:N_GROUPS + N_EXPERTS].astype(jnp.int32)
    padded = (counts + bm - 1) // bm * bm
    pad_end = jnp.cumsum(padded)
    pad_start = pad_end - padded
    experts = jnp.arange(N_EXPERTS, dtype=jnp.int32)
    onehot = (flat_e[:, None] == experts[None, :]).astype(BF16)
    start_blk = jnp.dot(onehot, (pad_start // bm).astype(BF16), preferred_element_type=F32)
    dest = start_blk.astype(jnp.int32) * bm + rank
    nused = (pad_end[-1] // bm).astype(jnp.int32)
    blk = jnp.arange(n_blocks, dtype=jnp.int32)
    blk_row = jnp.minimum(blk, nused - 1) * bm
    blk_e = jnp.sum((pad_end[None, :] <= blk_row[:, None]).astype(jnp.int32), axis=1)
    blk_e = jnp.minimum(blk_e, N_EXPERTS - 1)
    first = jnp.concatenate([jnp.ones((1,), jnp.int32),
                             (blk_e[1:] != blk_e[:-1]).astype(jnp.int32)])
    later_owner = jnp.where((counts[None, :] > 0) & (experts[None, :] > experts[:, None]),
                            experts[None, :], N_EXPERTS)
    next_owner = jnp.min(later_owner, axis=1)
    next_owner = jnp.where(next_owner == N_EXPERTS, -1, next_owner).astype(jnp.int32)
    last_blk = jnp.where(counts > 0, pad_end // bm - 1, -1).astype(jnp.int32)
    tail = nused + jnp.arange(N_EXPERTS, dtype=jnp.int32)
    tail_blk = jnp.where(tail < n_blocks, tail, -1)
    zero_blk = jnp.concatenate([last_blk, tail_blk])
    return dest, zero_blk, blk_e, first, next_owner[blk_e], nused.reshape(1)


def _rope_tables(pos):
    half = HEAD_DIM // 2
    inv = ROPE_THETA ** (-jnp.arange(half, dtype=F32) / half)
    ang = pos.astype(F32)[:, None] * inv[None, :]
    cos = jnp.cos(ang)
    sin = jnp.sin(ang)
    cos_h = jnp.concatenate([cos, cos], axis=-1)
    sin_h = jnp.concatenate([-sin, sin], axis=-1)
    reps = LANES // HEAD_DIM
    return jnp.tile(cos_h, (1, reps)), jnp.tile(sin_h, (1, reps))


def kernel(x_prompt, x_sample, cache_k, cache_v, state_conv, norm_mix_g, w_in, q_norm_g, k_norm_g, lambda_q1, lambda_k1, lambda_q2, lambda_k2, subln_g, conv_w, conv_norm_g, w_out, norm_ffn_g, router_group_w, router_group_b, router_expert_w, router_expert_b, expert_w_gate, expert_w_up, expert_w_down):
    assert w_in.shape[0] == 1, "single-layer step"
    bp, sp, _ = x_prompt.shape
    bs, ss, _ = x_sample.shape
    past = cache_k.shape[2]
    assert bp == 1 and sp % ATT_T == 0
    tp = bp * sp
    ts = bs * ss
    t_all = tp + ts
    lambda_init = 0.8 - 0.6 * math.exp(-0.3 * 0)

    w_in_bf = w_in[0].astype(BF16)
    w_out_bf = w_out[0].astype(BF16)
    w_top, w_bot = w_out_bf[:ATT_WIDTH], w_out_bf[ATT_WIDTH:]
    ng = norm_mix_g[0].reshape(1, D_MODEL)
    qg = jnp.tile(q_norm_g[0], QK_WIDTH // HEAD_DIM).reshape(1, QK_WIDTH)
    kg = jnp.tile(k_norm_g[0], QK_WIDTH // HEAD_DIM).reshape(1, QK_WIDTH)
    head_of = jnp.arange(IN_CH, dtype=jnp.int32) // HEAD_DIM
    gmat = jnp.where(head_of[:, None] == head_of[None, :], 1.0 / HEAD_DIM, 0.0).astype(BF16)
    lamv = jnp.stack([lambda_q1[0], lambda_k1[0], lambda_q2[0], lambda_k2[0]]).astype(F32)
    sg = subln_g[0].reshape(1, V_DIM)
    cw = conv_w[0]
    cng = conv_norm_g[0].reshape(1, CONV_CH)
    cos_p, sin_p = _rope_tables(jnp.arange(sp, dtype=jnp.int32))
    cos_s, sin_s = _rope_tables(jnp.tile(past + jnp.arange(ss, dtype=jnp.int32), bs))

    zero_conv = jnp.zeros((1, CONV_K - 1, CONV_CH), F32)
    qt_p, kf_p, kb_p, vf_p, vt_p, yc_p, tail_p = _inproj(
        x_prompt.reshape(tp, D_MODEL), ng, w_in_bf, qg, kg, gmat, cos_p, sin_p, zero_conv, cw, cng,
        tm=ATT_T, nseq=1, carry=True, qv_transposed=True)
    att_p = _attn_prompt(qt_p, kb_p, vt_p, lamv, subln_g[0].reshape(V_DIM, 1), lambda_init)

    seqs_per_tile = OUT_TM // ss
    q_s, kf_s, kb_s, vf_s, vb_s, yc_s, tail_s = _inproj(
        x_sample.reshape(ts, D_MODEL), ng, w_in_bf, qg, kg, gmat, cos_s, sin_s, state_conv[0], cw, cng,
        tm=OUT_TM, nseq=seqs_per_tile, carry=False, qv_transposed=False)
    kt_cache = jnp.transpose(cache_k[0], (0, 2, 3, 4, 1)).reshape(bs, QK_WIDTH, past)
    att_s = _attn_sample(q_s, kt_cache, cache_v[0].reshape(bs, past * N_HEADS, V_DIM),
                         kb_s, vb_s, lamv, sg, lambda_init)

    rw = jnp.zeros((D_MODEL, ROUTE_LANES), F32)
    rw = rw.at[:, 0:N_GROUPS].set(router_group_w[0]).at[:, N_GROUPS:N_GROUPS + N_EXPERTS].set(router_expert_w[0])
    rb = jnp.zeros((1, ROUTE_LANES), F32)
    rb = rb.at[0, 0:N_GROUPS].set(router_group_b[0]).at[0, N_GROUPS:N_GROUPS + N_EXPERTS].set(router_expert_b[0])
    nf = norm_ffn_g[0].reshape(1, D_MODEL)
    rw_bf = rw.astype(BF16)
    x1, xf, route, lane_counts = _outproj(att_p, yc_p, x_prompt.reshape(tp, D_MODEL), att_s, yc_s,
                                          x_sample.reshape(ts, D_MODEL), w_top, w_bot, nf, rw_bf, rb)

    n = t_all * TOP_K
    n_blocks = n // MOE_BM + N_EXPERTS
    dest, zero_blk, blk_e, first, next_e, nused = _routing_tables(route, lane_counts, n_blocks)
    xs = _dispatch(dest, zero_blk, xf, n_blocks * MOE_BM)
    outs = _experts(blk_e, first, next_e, nused, xs, expert_w_gate[0], expert_w_up[0], expert_w_down[0])
    y_p = _combine(dest, x1, route, outs, 0, tp)
    y_s = _combine(dest, x1, route, outs, tp, ts)

    return (y_p.reshape(bp, sp, D_MODEL),
            y_s.reshape(bs, ss, D_MODEL),
            kf_p.reshape(1, bp, sp, N_HEADS, 2, HEAD_DIM),
            vf_p.reshape(1, bp, sp, N_HEADS, V_DIM),
            tail_p.reshape(1, bp, CONV_K - 1, CONV_CH),
            kf_s.reshape(1, bs, ss, N_HEADS, 2, HEAD_DIM),
            vf_s.reshape(1, bs, ss, N_HEADS, V_DIM),
            tail_s.reshape(1, bs, CONV_K - 1, CONV_CH))
```

```python
import functools
import math

import jax
import jax.numpy as jnp
from jax import lax
from jax.experimental import pallas as pl
from jax.experimental.pallas import tpu as pltpu

F32 = jnp.float32
BF16 = jnp.bfloat16

D_MODEL = 2048
CHUNK = 64
HEAD_DIM = 64
V_DIM = 2 * HEAD_DIM
N_HEADS = 8
QK_WIDTH = N_HEADS * 2 * HEAD_DIM
ATT_WIDTH = N_HEADS * V_DIM
CONV_CH = 1024
CONV_K = 3
ROPE_THETA = 10000.0
N_GROUPS = 8
EXPERTS_PER_GROUP = 8
N_EXPERTS = N_GROUPS * EXPERTS_PER_GROUP
TOP_K = 2
D_EXPERT = D_MODEL // 4
NORM_EPS = 1e-6
SECTION = 1024
N_SECTIONS = 6

LANES = 128
IN_CH = 256
ATT_AHEAD = 2
ATT_ONES = 16
ATT_UNROLL = 4
ATT_CB = 512
ATT_T = 512
OUT_TM = 256
MOE_BM = 256
DISPATCH_TM = 512
ROW_DMA_UNROLL = 8
SLAB = D_MODEL // LANES
ROUTE_LANES = LANES
MASKED = -1e30
Q_SCALE = HEAD_DIM ** -0.5 * math.log2(math.e)
MIB = 1024 * 1024


def _params(sem, vmem_mib):
    return pltpu.CompilerParams(dimension_semantics=sem, vmem_limit_bytes=vmem_mib * MIB)


def _inproj_kernel(x_ref, ng_ref, w_ref, qg_ref, kg_ref, gmat_ref, cos_ref, sin_ref,
                   prev_ref, cw_ref, cng_ref,
                   q_ref, kf_ref, kb_ref, vf_ref, vb_ref, yc_ref, tail_ref,
                   xn_s, gb_s, c_s, z_s, u_s, *, nseq, carry, qv_transposed):
    i = pl.program_id(0)
    j = pl.program_id(1)
    tm = x_ref.shape[0]
    seq = tm // nseq

    @pl.when(j == 0)
    def _():
        x = x_ref[...]
        inv = lax.rsqrt(jnp.mean(x * x, axis=-1, keepdims=True) + NORM_EPS)
        xn_s[...] = (x * inv * ng_ref[...]).astype(BF16)
        if carry:
            @pl.when(i == 0)
            def _():
                u_s[6:8, :] = prev_ref[0]

    def chunk_dot(c):
        return jnp.dot(xn_s[...], w_ref[:, c * IN_CH:(c + 1) * IN_CH], preferred_element_type=F32)

    def pipelined(epilogue):
        acc = chunk_dot(0)
        for c in range(SECTION // IN_CH):
            nxt = chunk_dot(c + 1) if c + 1 < SECTION // IN_CH else None
            epilogue(acc, slice(c * IN_CH, (c + 1) * IN_CH))
            acc = nxt

    def head_norm_rope(a, g):
        ms = jnp.dot((a * a).astype(BF16), gmat_ref[...], preferred_element_type=F32)
        y = a * lax.rsqrt(ms + NORM_EPS) * g
        reps = IN_CH // LANES
        cos = jnp.concatenate([cos_ref[...]] * reps, axis=1)
        sin = jnp.concatenate([sin_ref[...]] * reps, axis=1)
        lane = lax.broadcasted_iota(jnp.int32, y.shape, 1)
        first = (lane & (HEAD_DIM - 1)) < HEAD_DIM // 2
        partner = jnp.where(first, pltpu.roll(y, IN_CH - HEAD_DIM // 2, 1),
                            pltpu.roll(y, HEAD_DIM // 2, 1))
        return y * cos + partner * sin

    for section in range(N_SECTIONS):

        def epilogue(acc, cols, section=section):
            if section == 0:
                q = head_norm_rope(acc, qg_ref[:, cols]) * Q_SCALE
                if qv_transposed:
                    q_ref[0, cols, :] = q.T.astype(BF16)
                else:
                    q_ref[:, cols] = q.astype(BF16)
            elif section == 1:
                k = head_norm_rope(acc, kg_ref[:, cols])
                kf_ref[:, cols] = k
                kb_ref[:, cols] = k.astype(BF16)
            elif section == 2:
                vf_ref[:, cols] = acc
                if qv_transposed:
                    vb_ref[0, cols, :] = acc.T.astype(BF16)
                else:
                    vb_ref[:, cols] = acc.astype(BF16)
            elif section == 3:
                gb_s[:, cols] = acc
            elif section == 4:
                c_s[:, cols] = acc
            else:
                u = c_s[:, cols] * acc
                w0 = cw_ref[0:1, cols]
                w1 = cw_ref[1:2, cols]
                w2 = cw_ref[2:3, cols]
                for s in range(nseq):
                    rows = slice(s * seq, (s + 1) * seq)
                    if not carry:
                        u_s[6:8, cols] = prev_ref[s, :, cols]
                    u_s[8:8 + seq, cols] = u[rows]
                    conv = (w0 * u_s[6:6 + seq, cols] + w1 * u_s[7:7 + seq, cols]
                            + w2 * u_s[8:8 + seq, cols])
                    z_s[rows, cols] = gb_s[rows, cols] * conv
                    tail = u_s[seq + 6:seq + 8, cols]
                    tail_ref[s, :, cols] = tail
                    if carry:
                        u_s[6:8, cols] = tail

        @pl.when(j == section)
        def _(section=section, epilogue=epilogue):
            pipelined(epilogue)
            if section == N_SECTIONS - 1:
                z = z_s[...]
                inv = lax.rsqrt(jnp.mean(z * z, axis=-1, keepdims=True) + NORM_EPS)
                yc_ref[...] = (z * inv * cng_ref[...]).astype(BF16)


def _inproj(x2d, norm_g, w_in_bf, qg, kg, gmat, cos, sin, conv_prev, conv_w, conv_norm_g,
            *, tm, nseq, carry, qv_transposed):
    t = x2d.shape[0]
    ni = t // tm
    seq = tm // nseq
    row = lambda i, j: (i, 0)
    const = lambda i, j: (0, 0)
    if qv_transposed:
        qv_shape = jax.ShapeDtypeStruct((ni, QK_WIDTH, tm), BF16)
        qv_spec = pl.BlockSpec((1, QK_WIDTH, tm), lambda i, j: (i, 0, 0))
    else:
        qv_shape = jax.ShapeDtypeStruct((t, QK_WIDTH), BF16)
        qv_spec = pl.BlockSpec((tm, QK_WIDTH), row)
    if carry:
        prev_spec = pl.BlockSpec((1, CONV_K - 1, CONV_CH), lambda i, j: (0, 0, 0))
        tail_shape = jax.ShapeDtypeStruct((1, CONV_K - 1, CONV_CH), F32)
        tail_spec = pl.BlockSpec((1, CONV_K - 1, CONV_CH), lambda i, j: (0, 0, 0))
    else:
        prev_spec = pl.BlockSpec((nseq, CONV_K - 1, CONV_CH), lambda i, j: (i, 0, 0))
        tail_shape = jax.ShapeDtypeStruct((ni * nseq, CONV_K - 1, CONV_CH), F32)
        tail_spec = pl.BlockSpec((nseq, CONV_K - 1, CONV_CH), lambda i, j: (i, 0, 0))
    kern = functools.partial(_inproj_kernel, nseq=nseq, carry=carry, qv_transposed=qv_transposed)
    return pl.pallas_call(
        kern,
        grid=(ni, N_SECTIONS),
        in_specs=[
            pl.BlockSpec((tm, D_MODEL), row),
            pl.BlockSpec((1, D_MODEL), const),
            pl.BlockSpec((D_MODEL, SECTION), lambda i, j: (0, j)),
            pl.BlockSpec((1, SECTION), const),
            pl.BlockSpec((1, SECTION), const),
            pl.BlockSpec((IN_CH, IN_CH), const),
            pl.BlockSpec((tm, LANES), row),
            pl.BlockSpec((tm, LANES), row),
            prev_spec,
            pl.BlockSpec((CONV_K, CONV_CH), const),
            pl.BlockSpec((1, CONV_CH), const),
        ],
        out_specs=[
            qv_spec,
            pl.BlockSpec((tm, QK_WIDTH), row),
            pl.BlockSpec((tm, QK_WIDTH), row),
            pl.BlockSpec((tm, ATT_WIDTH), row),
            qv_spec,
            pl.BlockSpec((tm, CONV_CH), row),
            tail_spec,
        ],
        out_shape=[
            qv_shape,
            jax.ShapeDtypeStruct((t, QK_WIDTH), F32),
            jax.ShapeDtypeStruct((t, QK_WIDTH), BF16),
            jax.ShapeDtypeStruct((t, ATT_WIDTH), F32),
            qv_shape,
            jax.ShapeDtypeStruct((t, CONV_CH), BF16),
            tail_shape,
        ],
        scratch_shapes=[
            pltpu.VMEM((tm, D_MODEL), BF16),
            pltpu.VMEM((tm, CONV_CH), F32),
            pltpu.VMEM((tm, CONV_CH), F32),
            pltpu.VMEM((tm, CONV_CH), F32),
            pltpu.VMEM((seq + 8, CONV_CH), F32),
        ],
        compiler_params=_params(("arbitrary", "arbitrary"), 56),
        name="inproj_carry" if carry else "inproj_seqs",
    )(x2d, norm_g, w_in_bf, qg, kg, gmat, cos, sin, conv_prev, conv_w, conv_norm_g)


def _lambda_full(lamv_ref, lambda_init):
    lv = lamv_ref[...]
    a = jnp.sum(lv[0:1] * lv[1:2], axis=-1, keepdims=True)
    b = jnp.sum(lv[2:3] * lv[3:4], axis=-1, keepdims=True)
    return jnp.exp(a) - jnp.exp(b) + lambda_init


def _split_heads(q):
    lane = lax.broadcasted_iota(jnp.int32, q.shape, 1)
    zero = jnp.zeros_like(q)
    return jnp.where(lane < HEAD_DIM, q, zero), jnp.where(lane >= HEAD_DIM, q, zero)


def _diff_finish(acc, l, lam, g, lambda_init, rows):
    a = acc / l
    d = a[0:rows] - lam * a[rows:2 * rows]
    inv = lax.rsqrt(jnp.mean(d * d, axis=-1, keepdims=True) + NORM_EPS)
    return d * inv * g * (1.0 - lambda_init)


def _attn_prompt_kernel(qt_ref, k_ref, vt_ref, lamv_ref, g_ref, o_ref, qq_s, m_s, acc_s, s_s,
                        *, lambda_init):
    i = pl.program_id(1)
    t = ATT_T
    qt = qt_ref[0]
    feat = lax.broadcasted_iota(jnp.int32, qt.shape, 0)
    zero = jnp.zeros_like(qt)
    qq_s[:, 0:t] = jnp.where(feat < HEAD_DIM, qt, zero)
    qq_s[:, t:2 * t] = jnp.where(feat >= HEAD_DIM, qt, zero)
    m_s[...] = jnp.full(m_s.shape, MASKED, F32)
    acc_s[...] = jnp.zeros(acc_s.shape, F32)
    ones_rows = (lax.broadcasted_iota(jnp.int32, (ATT_ONES, t), 0) == 0).astype(BF16)

    nblk = 2 * t // ATT_CB

    def scores(kt, cb):
        k = k_ref[pl.ds(pl.multiple_of(kt * t, t), t), :]
        s = jnp.dot(k, qq_s[:, cb * ATT_CB:(cb + 1) * ATT_CB], preferred_element_type=F32)
        s_s[cb] = s.astype(BF16)

    def step(kt, diagonal):
        vt = jnp.concatenate([vt_ref[kt], ones_rows], axis=0)
        for cb in range(nblk):
            cs = slice(cb * ATT_CB, (cb + 1) * ATT_CB)
            s = s_s[cb]
            if diagonal:
                key = lax.broadcasted_iota(jnp.int32, s.shape, 0)
                qry = lax.broadcasted_iota(jnp.int32, s.shape, 1) + (cb * ATT_CB) % t
                s = s + jnp.where((key // CHUNK) <= (qry // CHUNK), 0.0, MASKED).astype(BF16)
            m_prev = m_s[:, cs]
            m_new = jnp.maximum(m_prev, jnp.max(s, axis=0, keepdims=True).astype(F32))
            alpha = jnp.exp2(m_prev - m_new)
            p = jnp.exp2(s - m_new.astype(BF16))
            m_s[:, cs] = m_new
            ahead = cb + ATT_AHEAD
            if ahead < nblk:
                scores(kt, ahead)
            elif not diagonal:
                scores(kt + 1, ahead - nblk)
            acc_s[:, cs] = alpha * acc_s[:, cs] + jnp.dot(vt, p, preferred_element_type=F32)

    def group_body(g, carry):
        for u in range(ATT_UNROLL):
            step(ATT_UNROLL * g + u, False)
        return carry

    def single_body(kt, carry):
        step(kt, False)
        return carry

    for cb in range(ATT_AHEAD):
        scores(0, cb)
    n_grouped = i // ATT_UNROLL * ATT_UNROLL
    lax.fori_loop(0, i // ATT_UNROLL, group_body, 0)
    lax.fori_loop(n_grouped, i, single_body, 0)
    step(i, True)
    lam = _lambda_full(lamv_ref, lambda_init)
    a = acc_s[0:V_DIM, :] / acc_s[V_DIM:V_DIM + 1, :]
    d = a[:, 0:t] - lam * a[:, t:2 * t]
    inv = lax.rsqrt(jnp.mean(d * d, axis=0, keepdims=True) + NORM_EPS)
    y = d * inv * g_ref[...] * (1.0 - lambda_init)
    o_ref[...] = y.T.astype(BF16)


def _attn_prompt(qt_bf, k_bf, vt_bf, lamv, subln_g_col, lambda_init):
    nq = qt_bf.shape[0]
    t = nq * ATT_T
    kern = functools.partial(_attn_prompt_kernel, lambda_init=lambda_init)
    return pl.pallas_call(
        kern,
        grid=(N_HEADS, nq),
        in_specs=[
            pl.BlockSpec((1, V_DIM, ATT_T), lambda h, i: (i, h, 0)),
            pl.BlockSpec((t, V_DIM), lambda h, i: (0, h)),
            pl.BlockSpec((nq, V_DIM, ATT_T), lambda h, i: (0, h, 0)),
            pl.BlockSpec((4, HEAD_DIM), lambda h, i: (0, 0)),
            pl.BlockSpec((V_DIM, 1), lambda h, i: (0, 0)),
        ],
        out_specs=pl.BlockSpec((ATT_T, V_DIM), lambda h, i: (i, h)),
        out_shape=jax.ShapeDtypeStruct((t, ATT_WIDTH), BF16),
        scratch_shapes=[
            pltpu.VMEM((V_DIM, 2 * ATT_T), BF16),
            pltpu.VMEM((1, 2 * ATT_T), F32),
            pltpu.VMEM((V_DIM + ATT_ONES, 2 * ATT_T), F32),
            pltpu.VMEM((2 * ATT_T // ATT_CB, ATT_T, ATT_CB), BF16),
        ],
        compiler_params=_params(("arbitrary", "arbitrary"), 48),
        name="attn_prompt",
    )(qt_bf, k_bf, vt_bf, lamv, subln_g_col)


def _attn_sample_kernel(q_ref, kc_ref, vc_ref, kn_ref, vn_ref, lamv_ref, g_ref, o_ref,
                        *, lambda_init, past):
    rows = q_ref.shape[0]
    lam = _lambda_full(lamv_ref, lambda_init)
    contract_last = (((1,), (1,)), ((), ()))
    for h in range(N_HEADS):
        hs = slice(h * V_DIM, (h + 1) * V_DIM)
        q1, q2 = _split_heads(q_ref[:, hs])
        qq = jnp.concatenate([q1, q2], axis=0)
        s_c = jnp.dot(qq, kc_ref[0, hs, :].astype(BF16), preferred_element_type=F32)
        s_n = lax.dot_general(qq, kn_ref[:, hs], contract_last, preferred_element_type=F32)
        r = lax.broadcasted_iota(jnp.int32, s_n.shape, 0)
        c = lax.broadcasted_iota(jnp.int32, s_n.shape, 1)
        s_n = jnp.where(((past + c) // CHUNK) <= ((past + r % rows) // CHUNK), s_n, MASKED)
        m = jnp.maximum(jnp.max(s_c, axis=-1, keepdims=True), jnp.max(s_n, axis=-1, keepdims=True))
        p_c = jnp.exp2(s_c - m)
        p_n = jnp.exp2(s_n - m)
        l = jnp.sum(p_c, axis=-1, keepdims=True) + jnp.sum(p_n, axis=-1, keepdims=True)
        v_c = vc_ref[0, pl.ds(h, past, stride=N_HEADS), :].astype(BF16)
        acc = (jnp.dot(p_c.astype(BF16), v_c, preferred_element_type=F32)
               + jnp.dot(p_n.astype(BF16), vn_ref[:, hs], preferred_element_type=F32))
        o_ref[:, hs] = _diff_finish(acc, l, lam, g_ref[...], lambda_init, rows).astype(BF16)


def _attn_sample(q_bf, kt_cache, v_cache, k_bf, v_bf, lamv, subln_g, lambda_init):
    nb, _, past = kt_cache.shape
    rows = q_bf.shape[0] // nb
    kern = functools.partial(_attn_sample_kernel, lambda_init=lambda_init, past=past)
    new_spec = pl.BlockSpec((rows, ATT_WIDTH), lambda b: (b, 0))
    return pl.pallas_call(
        kern,
        grid=(nb,),
        in_specs=[
            new_spec,
            pl.BlockSpec((1, QK_WIDTH, past), lambda b: (b, 0, 0)),
            pl.BlockSpec((1, past * N_HEADS, V_DIM), lambda b: (b, 0, 0)),
            new_spec, new_spec,
            pl.BlockSpec((4, HEAD_DIM), lambda b: (0, 0)),
            pl.BlockSpec((1, V_DIM), lambda b: (0, 0)),
        ],
        out_specs=new_spec,
        out_shape=jax.ShapeDtypeStruct(q_bf.shape, BF16),
        compiler_params=_params(("arbitrary",), 48),
        name="attn_sample",
    )(q_bf, kt_cache, v_cache, k_bf, v_bf, lamv, subln_g)


def _outproj_kernel(att_p_ref, yc_p_ref, x_p_ref, att_s_ref, yc_s_ref, x_s_ref,
                    wt_ref, wb_ref, ng_ref, rw_ref, rb_ref, x1_ref, xf_ref, route_ref, count_ref, count_s,
                    *, n_prompt_tiles):
    i = pl.program_id(0)
    shared = (wt_ref, wb_ref, ng_ref, rw_ref, rb_ref, x1_ref, xf_ref, route_ref, count_ref, count_s)

    @pl.when(i == 0)
    def _():
        count_s[...] = jnp.zeros(count_s.shape, F32)

    @pl.when(i < n_prompt_tiles)
    def _():
        _outproj_tile(att_p_ref, yc_p_ref, x_p_ref, *shared)

    @pl.when(i >= n_prompt_tiles)
    def _():
        _outproj_tile(att_s_ref, yc_s_ref, x_s_ref, *shared)


def _outproj_tile(att_ref, yc_ref, x_ref, wt_ref, wb_ref, ng_ref, rw_ref, rb_ref,
                  x1_ref, xf_ref, route_ref, count_ref, count_s):
    o = (jnp.dot(att_ref[...], wt_ref[...], preferred_element_type=F32)
         + jnp.dot(yc_ref[...], wb_ref[...], preferred_element_type=F32))
    x1 = x_ref[...] + o
    x1_ref[...] = x1
    xf = x1 * lax.rsqrt(jnp.mean(x1 * x1, axis=-1, keepdims=True) + NORM_EPS) * ng_ref[...]
    for c in range(SLAB):
        xf_ref[c] = xf[:, c * LANES:(c + 1) * LANES]
    logits =jnp.dot(xf.astype(BF16), rw_ref[...], preferred_element_type=F32) + rb_ref[...]

    lane = lax.broadcasted_iota(jnp.int32, logits.shape, 1)
    neg = -jnp.inf
    is_group = lane < N_GROUPS
    gl = jnp.where(is_group, logits, neg)
    gmax = jnp.max(gl, axis=-1, keepdims=True)
    grp = jnp.min(jnp.where(gl == gmax, lane, ROUTE_LANES), axis=-1, keepdims=True)
    gsum = jnp.sum(jnp.where(is_group, jnp.exp(gl - gmax), 0.0), axis=-1, keepdims=True)
    g_w = 1.0 / gsum
    e_lane = lane - N_GROUPS
    in_grp = (e_lane >= 0) & (e_lane < N_EXPERTS) & ((e_lane // EXPERTS_PER_GROUP) == grp)
    el = jnp.where(in_grp, logits, neg)
    t1 = jnp.max(el, axis=-1, keepdims=True)
    i1 = jnp.min(jnp.where(el == t1, lane, ROUTE_LANES), axis=-1, keepdims=True)
    el2 = jnp.where(lane == i1, neg, el)
    t2 = jnp.max(el2, axis=-1, keepdims=True)
    i2 = jnp.min(jnp.where(el2 == t2, lane, ROUTE_LANES), axis=-1, keepdims=True)
    r21 = jnp.exp(t2 - t1)
    w0 = g_w / (1.0 + r21)
    w1 = g_w * r21 / (1.0 + r21)
    e0 = (i1 - N_GROUPS).astype(F32)
    e1 = (i2 - N_GROUPS).astype(F32)
    rows = logits.shape[0]
    picked0 = lane == i1
    picked1 = lane == i2
    member = (picked0 | picked1).astype(BF16)
    earlier = (lax.broadcasted_iota(jnp.int32, (rows, rows), 1)
               < lax.broadcasted_iota(jnp.int32, (rows, rows), 0)).astype(BF16)
    before = jnp.dot(earlier, member, preferred_element_type=F32) + count_s[...]
    r0 = jnp.sum(jnp.where(picked0, before, 0.0), axis=-1, keepdims=True)
    r1 = jnp.sum(jnp.where(picked1, before, 0.0), axis=-1, keepdims=True)
    count_s[...] += jnp.sum(member.astype(F32), axis=0, keepdims=True)
    count_ref[...] = count_s[...]
    route_ref[...] = jnp.where(lane == 0, e0, jnp.where(lane == 1, e1,
                               jnp.where(lane == 2, w0, jnp.where(lane == 3, w1,
                               jnp.where(lane == 4, r0, jnp.where(lane == 5, r1, 0.0))))))


def _outproj(att_p, yc_p, x_p, att_s, yc_s, x_s, w_top, w_bot, norm_g, rw, rb):
    tm = OUT_TM
    n_p = x_p.shape[0] // tm
    n_s = x_s.shape[0] // tm
    t_all = x_p.shape[0] + x_s.shape[0]
    prow = lambda i: (jnp.minimum(i, n_p - 1), 0)
    srow = lambda i: (jnp.maximum(i - n_p, 0), 0)
    row = lambda i: (i, 0)
    const = lambda i: (0, 0)
    kern = functools.partial(_outproj_kernel, n_prompt_tiles=n_p)
    return pl.pallas_call(
        kern,
        grid=(n_p + n_s,),
        in_specs=[
            pl.BlockSpec((tm, ATT_WIDTH), prow),
            pl.BlockSpec((tm, CONV_CH), prow),
            pl.BlockSpec((tm, D_MODEL), prow),
            pl.BlockSpec((tm, ATT_WIDTH), srow),
            pl.BlockSpec((tm, CONV_CH), srow),
            pl.BlockSpec((tm, D_MODEL), srow),
            pl.BlockSpec((ATT_WIDTH, D_MODEL), const),
            pl.BlockSpec((CONV_CH, D_MODEL), const),
            pl.BlockSpec((1, D_MODEL), const),
            pl.BlockSpec((D_MODEL, ROUTE_LANES), const),
            pl.BlockSpec((1, ROUTE_LANES), const),
        ],
        out_specs=[
            pl.BlockSpec((tm, D_MODEL), row),
            pl.BlockSpec((SLAB, tm, LANES), lambda i: (0, i, 0)),
            pl.BlockSpec((tm, ROUTE_LANES), row),
            pl.BlockSpec((1, ROUTE_LANES), const),
        ],
        out_shape=[
            jax.ShapeDtypeStruct((t_all, D_MODEL), F32),
            jax.ShapeDtypeStruct((SLAB, t_all, LANES), F32),
            jax.ShapeDtypeStruct((t_all, ROUTE_LANES), F32),
            jax.ShapeDtypeStruct((1, ROUTE_LANES), F32),
        ],
        scratch_shapes=[pltpu.VMEM((1, ROUTE_LANES), F32)],
        compiler_params=_params(("arbitrary",), 52),
        name="outproj_router",
    )(att_p, yc_p, x_p, att_s, yc_s, x_s, w_top, w_bot, norm_g, rw, rb)


def _dispatch_copy(xf_ref, xs_hbm, sem, src_row, dst_row):
    return pltpu.make_async_copy(xf_ref.at[:, src_row, :],
                                 xs_hbm.at[pl.ds(dst_row * SLAB, SLAB)], sem)


def _zero_block_copy(zero_s, xs_hbm, sem, blk):
    span = MOE_BM * SLAB
    return pltpu.make_async_copy(zero_s, xs_hbm.at[pl.ds(blk * span, span)], sem)


def _dispatch_kernel(dest_ref, zero_blk_ref, xf_ref, xs_hbm, zero_s, sem):
    tm = DISPATCH_TM

    @pl.when(pl.program_id(0) == 0)
    def _():
        zero_s[...] = jnp.zeros(zero_s.shape, F32)

        def fill(j, start):
            blk = zero_blk_ref[0, 0, j]

            @pl.when(blk >= 0)
            def _():
                copy = _zero_block_copy(zero_s, xs_hbm, sem, blk)
                if start:
                    copy.start()
                else:
                    copy.wait()

        lax.fori_loop(0, zero_blk_ref.shape[-1], lambda j, c: (fill(j, True), c)[1], 0)
        lax.fori_loop(0, zero_blk_ref.shape[-1], lambda j, c: (fill(j, False), c)[1], 0)

    def issue(t, carry):
        for k in range(TOP_K):
            _dispatch_copy(xf_ref, xs_hbm, sem, t, dest_ref[0, 0, TOP_K * t + k]).start()
        return carry

    def drain(t, carry):
        for k in range(TOP_K):
            _dispatch_copy(xf_ref, xs_hbm, sem, t, dest_ref[0, 0, TOP_K * t + k]).wait()
        return carry

    lax.fori_loop(0, tm, issue, 0, unroll=ROW_DMA_UNROLL)
    lax.fori_loop(0, tm, drain, 0, unroll=ROW_DMA_UNROLL)


def _dispatch(dest, zero_blk, xf, cap):
    t_all = xf.shape[1]
    tm = DISPATCH_TM
    dest3 = dest.reshape(t_all // tm, 1, TOP_K * tm)
    zero_blk3 = zero_blk.reshape(1, 1, -1)
    return pl.pallas_call(
        _dispatch_kernel,
        grid=(t_all // tm,),
        in_specs=[
            pl.BlockSpec((1, 1, TOP_K * tm), lambda i: (i, 0, 0), memory_space=pltpu.SMEM),
            pl.BlockSpec(zero_blk3.shape, lambda i: (0, 0, 0), memory_space=pltpu.SMEM),
            pl.BlockSpec((SLAB, tm, LANES), lambda i: (0, i, 0)),
        ],
        out_specs=pl.BlockSpec(memory_space=pl.ANY),
        out_shape=jax.ShapeDtypeStruct((cap * SLAB, LANES), F32),
        scratch_shapes=[pltpu.VMEM((MOE_BM * SLAB, LANES), F32), pltpu.SemaphoreType.DMA(())],
        compiler_params=_params(("arbitrary",), 24),
        name="moe_dispatch",
    )(dest3, zero_blk3, xf)


def _expert_weight_copies(hbm_refs, stage_refs, sems, e):
    return [pltpu.make_async_copy(hbm.at[e], stage, sems.at[n])
            for n, (hbm, stage) in enumerate(zip(hbm_refs, stage_refs))]


def _experts_kernel(blk_e_ref, first_ref, next_e_ref, nused_ref, x_ref, wg_hbm, wu_hbm, wd_hbm, o_ref,
                    wg_f, wu_f, wd_f, wgu_s, wd_s, sems):
    i = pl.program_id(0)
    hbm_refs = (wg_hbm, wu_hbm, wd_hbm)
    stage_refs = (wg_f, wu_f, wd_f)

    @pl.when(i == 0)
    def _():
        for copy in _expert_weight_copies(hbm_refs, stage_refs, sems, blk_e_ref[0]):
            copy.start()

    @pl.when(i < nused_ref[0])
    def _():
        @pl.when(first_ref[i] == 1)
        def _():
            for copy in _expert_weight_copies(hbm_refs, stage_refs, sems, blk_e_ref[i]):
                copy.wait()
            wgu_s[:, 0:D_EXPERT] = wg_f[...].astype(BF16)
            wgu_s[:, D_EXPERT:2 * D_EXPERT] = wu_f[...].astype(BF16)
            wd_s[...] = wd_f[...].astype(BF16)

            @pl.when(next_e_ref[i] >= 0)
            def _():
                for copy in _expert_weight_copies(hbm_refs, stage_refs, sems, next_e_ref[i]):
                    copy.start()

        x = x_ref[...].reshape(MOE_BM, SLAB, LANES).reshape(MOE_BM, D_MODEL).astype(BF16)
        gu = jnp.dot(x, wgu_s[...], preferred_element_type=F32)
        g = gu[:, 0:D_EXPERT]
        u = gu[:, D_EXPERT:2 * D_EXPERT]
        h = g / (1.0 + jnp.exp(-g)) * u
        out = jnp.dot(h.astype(BF16), wd_s[...], preferred_element_type=F32)
        o_ref[...] = out.reshape(MOE_BM, SLAB, LANES).reshape(MOE_BM * SLAB, LANES)

    @pl.when(i >= nused_ref[0])
    def _():
        o_ref[...] = jnp.zeros(o_ref.shape, F32)


def _experts(blk_e, first, next_e, nused, xs, w_gate, w_up, w_down):
    cap = xs.shape[0] // SLAB
    bm = MOE_BM
    rows = lambda i, be, fi, ne, nu: (jnp.minimum(i, nu[0] - 1), 0)
    grid_spec = pltpu.PrefetchScalarGridSpec(
        num_scalar_prefetch=4,
        grid=(cap // bm,),
        in_specs=[
            pl.BlockSpec((bm * SLAB, LANES), rows),
            pl.BlockSpec(memory_space=pl.ANY),
            pl.BlockSpec(memory_space=pl.ANY),
            pl.BlockSpec(memory_space=pl.ANY),
        ],
        out_specs=pl.BlockSpec((bm * SLAB, LANES), lambda i, be, fi, ne, nu: (i, 0)),
        scratch_shapes=[
            pltpu.VMEM((D_MODEL, D_EXPERT), F32),
            pltpu.VMEM((D_MODEL, D_EXPERT), F32),
            pltpu.VMEM((D_EXPERT, D_MODEL), F32),
            pltpu.VMEM((D_MODEL, 2 * D_EXPERT), BF16),
            pltpu.VMEM((D_EXPERT, D_MODEL), BF16),
            pltpu.SemaphoreType.DMA((3,)),
        ],
    )
    return pl.pallas_call(
        _experts_kernel,
        grid_spec=grid_spec,
        out_shape=jax.ShapeDtypeStruct((cap * SLAB, LANES), F32),
        compiler_params=_params(("arbitrary",), 48),
        name="moe_experts",
    )(blk_e, first, next_e, nused, xs, w_gate, w_up, w_down)


def _combine_copy(outs_hbm, g_s, sems, slot, src_row, k, t):
    return pltpu.make_async_copy(outs_hbm.at[pl.ds(src_row * SLAB, SLAB)],
                                 g_s.at[slot, k, :, t, :], sems.at[slot])


def _combine_kernel(dest_ref, dest_next_ref, x1_ref, route_ref, outs_hbm, y_ref, g_s, sems, *, n_tiles):
    i = pl.program_id(0)
    tm = x1_ref.shape[0]
    slot = i % 2

    def gather(table_ref, into, start):
        def body(t, carry):
            for k in range(TOP_K):
                copy = _combine_copy(outs_hbm, g_s, sems, into, table_ref[0, 0, TOP_K * t + k], k, t)
                if start:
                    copy.start()
                else:
                    copy.wait()
            return carry
        lax.fori_loop(0, tm, body, 0, unroll=ROW_DMA_UNROLL)

    @pl.when(i == 0)
    def _():
        gather(dest_ref, 0, True)

    @pl.when(i + 1 < n_tiles)
    def _():
        gather(dest_next_ref, 1 - slot, True)

    gather(dest_ref, slot, False)
    r = route_ref[...]
    w0 = jnp.broadcast_to(r[:, 2:3], (tm, LANES))
    w1 = jnp.broadcast_to(r[:, 3:4], (tm, LANES))
    for c in range(SLAB):
        cs = slice(c * LANES, (c + 1) * LANES)
        y_ref[:, cs] = x1_ref[:, cs] + w0 * g_s[slot, 0, c] + w1 * g_s[slot, 1, c]


def _combine(dest, x1, route, outs, row_off, t):
    tm = OUT_TM
    t_all = x1.shape[0]
    off = row_off // tm
    dest3 = dest.reshape(t_all // tm, 1, TOP_K * tm)
    n = t // tm
    return pl.pallas_call(
        functools.partial(_combine_kernel, n_tiles=n),
        grid=(n,),
        in_specs=[
            pl.BlockSpec((1, 1, TOP_K * tm), lambda i: (i + off, 0, 0), memory_space=pltpu.SMEM),
            pl.BlockSpec((1, 1, TOP_K * tm), lambda i: (jnp.minimum(i + 1, n - 1) + off, 0, 0),
                         memory_space=pltpu.SMEM),
            pl.BlockSpec((tm, D_MODEL), lambda i: (i + off, 0)),
            pl.BlockSpec((tm, ROUTE_LANES), lambda i: (i + off, 0)),
            pl.BlockSpec(memory_space=pl.ANY),
        ],
        out_specs=pl.BlockSpec((tm, D_MODEL), lambda i: (i, 0)),
        out_shape=jax.ShapeDtypeStruct((t, D_MODEL), F32),
        scratch_shapes=[pltpu.VMEM((2, TOP_K, SLAB, tm, LANES), F32), pltpu.SemaphoreType.DMA((2,))],
        compiler_params=_params(("arbitrary",), 32),
        name="moe_combine",
    )(dest3, dest3, x1, route, outs)


def _routing_tables(route, lane_counts, n_blocks):
    bm = MOE_BM
    flat_e = route[:, 0:TOP_K].astype(jnp.int32).reshape(-1)
    rank = route[:, 2 * TOP_K:3 * TOP_K].astype(jnp.int32).reshape(-1)
    counts = lane_counts[0, N_GROUPS:N_GROUPS + N_EXPERTS].astype(jnp.int32)
    padded = (counts + bm - 1) // bm * bm
    pad_end = jnp.cumsum(padded)
    pad_start = pad_end - padded
    experts = jnp.arange(N_EXPERTS, dtype=jnp.int32)
    onehot = (flat_e[:, None] == experts[None, :]).astype(BF16)
    start_blk = jnp.dot(onehot, (pad_start // bm).astype(BF16), preferred_element_type=F32)
    dest = start_blk.astype(jnp.int32) * bm + rank
    nused = (pad_end[-1] // bm).astype(jnp.int32)
    blk = jnp.arange(n_blocks, dtype=jnp.int32)
    blk_row = jnp.minimum(blk, nused - 1) * bm
    blk_e = jnp.sum((pad_end[None, :] <= blk_row[:, None]).astype(jnp.int32), axis=1)
    blk_e = jnp.minimum(blk_e, N_EXPERTS - 1)
    first = jnp.concatenate([jnp.ones((1,), jnp.int32),
                             (blk_e[1:] != blk_e[:-1]).astype(jnp.int32)])
    later_owner = jnp.where((counts[None, :] > 0) & (experts[None, :] > experts[:, None]),
                            experts[None, :], N_EXPERTS)
    next_owner = jnp.min(later_owner, axis=1)
    next_owner = jnp.where(next_owner == N_EXPERTS, -1, next_owner).astype(jnp.int32)
    last_blk = jnp.where(counts > 0, pad_end // bm - 1, -1).astype(jnp.int32)
    tail = nused + jnp.arange(N_EXPERTS, dtype=jnp.int32)
    tail_blk = jnp.where(tail < n_blocks, tail, -1)
    zero_blk = jnp.concatenate([last_blk, tail_blk])
    return dest, zero_blk, blk_e, first, next_owner[blk_e], nused.reshape(1)


def _rope_tables(pos):
    half = HEAD_DIM // 2
    inv = ROPE_THETA ** (-jnp.arange(half, dtype=F32) / half)
    ang = pos.astype(F32)[:, None] * inv[None, :]
    cos = jnp.cos(ang)
    sin = jnp.sin(ang)
    cos_h = jnp.concatenate([cos, cos], axis=-1)
    sin_h = jnp.concatenate([-sin, sin], axis=-1)
    reps = LANES // HEAD_DIM
    return jnp.tile(cos_h, (1, reps)), jnp.tile(sin_h, (1, reps))


def kernel(x_prompt, x_sample, cache_k, cache_v, state_conv, norm_mix_g, w_in, q_norm_g, k_norm_g, lambda_q1, lambda_k1, lambda_q2, lambda_k2, subln_g, conv_w, conv_norm_g, w_out, norm_ffn_g, router_group_w, router_group_b, router_expert_w, router_expert_b, expert_w_gate, expert_w_up, expert_w_down):
    assert w_in.shape[0] == 1, "single-layer step"
    bp, sp, _ = x_prompt.shape
    bs, ss, _ = x_sample.shape
    past = cache_k.shape[2]
    assert bp == 1 and sp % ATT_T == 0
    tp = bp * sp
    ts = bs * ss
    t_all = tp + ts
    lambda_init = 0.8 - 0.6 * math.exp(-0.3 * 0)

    w_in_bf = w_in[0].astype(BF16)
    w_out_bf = w_out[0].astype(BF16)
    w_top, w_bot = w_out_bf[:ATT_WIDTH], w_out_bf[ATT_WIDTH:]
    ng = norm_mix_g[0].reshape(1, D_MODEL)
    qg = jnp.tile(q_norm_g[0], QK_WIDTH // HEAD_DIM).reshape(1, QK_WIDTH)
    kg = jnp.tile(k_norm_g[0], QK_WIDTH // HEAD_DIM).reshape(1, QK_WIDTH)
    head_of = jnp.arange(IN_CH, dtype=jnp.int32) // HEAD_DIM
    gmat = jnp.where(head_of[:, None] == head_of[None, :], 1.0 / HEAD_DIM, 0.0).astype(BF16)
    lamv = jnp.stack([lambda_q1[0], lambda_k1[0], lambda_q2[0], lambda_k2[0]]).astype(F32)
    sg = subln_g[0].reshape(1, V_DIM)
    cw = conv_w[0]
    cng = conv_norm_g[0].reshape(1, CONV_CH)
    cos_p, sin_p = _rope_tables(jnp.arange(sp, dtype=jnp.int32))
    cos_s, sin_s = _rope_tables(jnp.tile(past + jnp.arange(ss, dtype=jnp.int32), bs))

    zero_conv = jnp.zeros((1, CONV_K - 1, CONV_CH), F32)
    qt_p, kf_p, kb_p, vf_p, vt_p, yc_p, tail_p = _inproj(
        x_prompt.reshape(tp, D_MODEL), ng, w_in_bf, qg, kg, gmat, cos_p, sin_p, zero_conv, cw, cng,
        tm=ATT_T, nseq=1, carry=True, qv_transposed=True)
    att_p = _attn_prompt(qt_p, kb_p, vt_p, lamv, subln_g[0].reshape(V_DIM, 1), lambda_init)

    seqs_per_tile = ATT_T // ss
    q_s, kf_s, kb_s, vf_s, vb_s, yc_s, tail_s = _inproj(
        x_sample.reshape(ts, D_MODEL), ng, w_in_bf, qg, kg, gmat, cos_s, sin_s, state_conv[0], cw, cng,
        tm=ATT_T, nseq=seqs_per_tile, carry=False, qv_transposed=False)
    kt_cache = jnp.transpose(cache_k[0], (0, 2, 3, 4, 1)).reshape(bs, QK_WIDTH, past)
    att_s = _attn_sample(q_s, kt_cache, cache_v[0].reshape(bs, past * N_HEADS, V_DIM),
                         kb_s, vb_s, lamv, sg, lambda_init)

    rw = jnp.zeros((D_MODEL, ROUTE_LANES), F32)
    rw = rw.at[:, 0:N_GROUPS].set(router_group_w[0]).at[:, N_GROUPS:N_GROUPS + N_EXPERTS].set(router_expert_w[0])
    rb = jnp.zeros((1, ROUTE_LANES), F32)
    rb = rb.at[0, 0:N_GROUPS].set(router_group_b[0]).at[0, N_GROUPS:N_GROUPS + N_EXPERTS].set(router_expert_b[0])
    nf = norm_ffn_g[0].reshape(1, D_MODEL)
    rw_bf = rw.astype(BF16)
    x1, xf, route, lane_counts = _outproj(att_p, yc_p, x_prompt.reshape(tp, D_MODEL), att_s, yc_s,
                                          x_sample.reshape(ts, D_MODEL), w_top, w_bot, nf, rw_bf, rb)

    n = t_all * TOP_K
    n_blocks = n // MOE_BM + N_EXPERTS
    dest, zero_blk, blk_e, first, next_e, nused = _routing_tables(route, lane_counts, n_blocks)
    xs = _dispatch(dest, zero_blk, xf, n_blocks * MOE_BM)
    outs = _experts(blk_e, first, next_e, nused, xs, expert_w_gate[0], expert_w_up[0], expert_w_down[0])
    y_p = _combine(dest, x1, route, outs, 0, tp)
    y_s = _combine(dest, x1, route, outs, tp, ts)

    return (y_p.reshape(bp, sp, D_MODEL),
            y_s.reshape(bs, ss, D_MODEL),
            kf_p.reshape(1, bp, sp, N_HEADS, 2, HEAD_DIM),
            vf_p.reshape(1, bp, sp, N_HEADS, V_DIM),
            tail_p.reshape(1, bp, CONV_K - 1, CONV_CH),
            kf_s.reshape(1, bs, ss, N_HEADS, 2, HEAD_DIM),
            vf_s.reshape(1, bs, ss, N_HEADS, V_DIM),
            tail_s.reshape(1, bs, CONV_K - 1, CONV_CH))
```

```python
import functools
import math

import jax
import jax.numpy as jnp
from jax import lax
from jax.experimental import pallas as pl
from jax.experimental.pallas import tpu as pltpu

F32 = jnp.float32
BF16 = jnp.bfloat16

D_MODEL = 2048
CHUNK = 64
HEAD_DIM = 64
V_DIM = 2 * HEAD_DIM
N_HEADS = 8
QK_WIDTH = N_HEADS * 2 * HEAD_DIM
ATT_WIDTH = N_HEADS * V_DIM
CONV_CH = 1024
CONV_K = 3
ROPE_THETA = 10000.0
N_GROUPS = 8
EXPERTS_PER_GROUP = 8
N_EXPERTS = N_GROUPS * EXPERTS_PER_GROUP
TOP_K = 2
D_EXPERT = D_MODEL // 4
NORM_EPS = 1e-6
SECTION = 1024
N_SECTIONS = 6

LANES = 128
IN_CH = 256
ATT_AHEAD = 2
ATT_ONES = 16
ATT_UNROLL = 4
ATT_CB = 512
ATT_T = 512
OUT_TM = 256
MOE_BM = 256
DISPATCH_TM = 512
ROW_DMA_UNROLL = 8
PACK_COLS = D_MODEL // 2
SLAB = PACK_COLS // LANES
U32 = jnp.uint32
ROUTE_LANES = LANES
MASKED = -1e30
Q_SCALE = HEAD_DIM ** -0.5 * math.log2(math.e)
MIB = 1024 * 1024


def _pack_bf16_pair(lo, hi):
    lo_bits = lax.bitcast_convert_type(lo.astype(BF16).astype(F32), U32)
    hi_bits = lax.bitcast_convert_type(hi.astype(BF16).astype(F32), U32)
    return hi_bits | (lo_bits >> 16)


def _unpack_bf16_pair(words):
    lo = lax.bitcast_convert_type(words << 16, F32)
    hi = lax.bitcast_convert_type(words & U32(0xFFFF0000), F32)
    return lo, hi


def _params(sem, vmem_mib):
    return pltpu.CompilerParams(dimension_semantics=sem, vmem_limit_bytes=vmem_mib * MIB)


def _inproj_kernel(x_ref, ng_ref, w_ref, qg_ref, kg_ref, gmat_ref, cos_ref, sin_ref,
                   prev_ref, cw_ref, cng_ref,
                   q_ref, kf_ref, kb_ref, vf_ref, vb_ref, yc_ref, tail_ref,
                   xn_s, gb_s, c_s, z_s, u_s, *, nseq, carry, qv_transposed):
    i = pl.program_id(0)
    j = pl.program_id(1)
    tm = x_ref.shape[0]
    seq = tm // nseq

    @pl.when(j == 0)
    def _():
        x = x_ref[...]
        inv = lax.rsqrt(jnp.mean(x * x, axis=-1, keepdims=True) + NORM_EPS)
        xn_s[...] = (x * inv * ng_ref[...]).astype(BF16)
        if carry:
            @pl.when(i == 0)
            def _():
                u_s[6:8, :] = prev_ref[0]

    def chunk_dot(c):
        return jnp.dot(xn_s[...], w_ref[:, c * IN_CH:(c + 1) * IN_CH], preferred_element_type=F32)

    def pipelined(epilogue):
        acc = chunk_dot(0)
        for c in range(SECTION // IN_CH):
            nxt = chunk_dot(c + 1) if c + 1 < SECTION // IN_CH else None
            epilogue(acc, slice(c * IN_CH, (c + 1) * IN_CH))
            acc = nxt

    def head_norm_rope(a, g):
        ms = jnp.dot((a * a).astype(BF16), gmat_ref[...], preferred_element_type=F32)
        y = a * lax.rsqrt(ms + NORM_EPS) * g
        reps = IN_CH // LANES
        cos = jnp.concatenate([cos_ref[...]] * reps, axis=1)
        sin = jnp.concatenate([sin_ref[...]] * reps, axis=1)
        lane = lax.broadcasted_iota(jnp.int32, y.shape, 1)
        first = (lane & (HEAD_DIM - 1)) < HEAD_DIM // 2
        partner = jnp.where(first, pltpu.roll(y, IN_CH - HEAD_DIM // 2, 1),
                            pltpu.roll(y, HEAD_DIM // 2, 1))
        return y * cos + partner * sin

    for section in range(N_SECTIONS):

        def epilogue(acc, cols, section=section):
            if section == 0:
                q = head_norm_rope(acc, qg_ref[:, cols]) * Q_SCALE
                if qv_transposed:
                    q_ref[0, cols, :] = q.T.astype(BF16)
                else:
                    q_ref[:, cols] = q.astype(BF16)
            elif section == 1:
                k = head_norm_rope(acc, kg_ref[:, cols])
                kf_ref[:, cols] = k
                kb_ref[:, cols] = k.astype(BF16)
            elif section == 2:
                vf_ref[:, cols] = acc
                if qv_transposed:
                    vb_ref[0, cols, :] = acc.T.astype(BF16)
                else:
                    vb_ref[:, cols] = acc.astype(BF16)
            elif section == 3:
                gb_s[:, cols] = acc
            elif section == 4:
                c_s[:, cols] = acc
            else:
                u = c_s[:, cols] * acc
                w0 = cw_ref[0:1, cols]
                w1 = cw_ref[1:2, cols]
                w2 = cw_ref[2:3, cols]
                for s in range(nseq):
                    rows = slice(s * seq, (s + 1) * seq)
                    if not carry:
                        u_s[6:8, cols] = prev_ref[s, :, cols]
                    u_s[8:8 + seq, cols] = u[rows]
                    conv = (w0 * u_s[6:6 + seq, cols] + w1 * u_s[7:7 + seq, cols]
                            + w2 * u_s[8:8 + seq, cols])
                    z_s[rows, cols] = gb_s[rows, cols] * conv
                    tail = u_s[seq + 6:seq + 8, cols]
                    tail_ref[s, :, cols] = tail
                    if carry:
                        u_s[6:8, cols] = tail

        @pl.when(j == section)
        def _(section=section, epilogue=epilogue):
            pipelined(epilogue)
            if section == N_SECTIONS - 1:
                z = z_s[...]
                inv = lax.rsqrt(jnp.mean(z * z, axis=-1, keepdims=True) + NORM_EPS)
                yc_ref[...] = (z * inv * cng_ref[...]).astype(BF16)


def _inproj(x2d, norm_g, w_in_bf, qg, kg, gmat, cos, sin, conv_prev, conv_w, conv_norm_g,
            *, tm, nseq, carry, qv_transposed):
    t = x2d.shape[0]
    ni = t // tm
    seq = tm // nseq
    row = lambda i, j: (i, 0)
    const = lambda i, j: (0, 0)
    if qv_transposed:
        qv_shape = jax.ShapeDtypeStruct((ni, QK_WIDTH, tm), BF16)
        qv_spec = pl.BlockSpec((1, QK_WIDTH, tm), lambda i, j: (i, 0, 0))
    else:
        qv_shape = jax.ShapeDtypeStruct((t, QK_WIDTH), BF16)
        qv_spec = pl.BlockSpec((tm, QK_WIDTH), row)
    if carry:
        prev_spec = pl.BlockSpec((1, CONV_K - 1, CONV_CH), lambda i, j: (0, 0, 0))
        tail_shape = jax.ShapeDtypeStruct((1, CONV_K - 1, CONV_CH), F32)
        tail_spec = pl.BlockSpec((1, CONV_K - 1, CONV_CH), lambda i, j: (0, 0, 0))
    else:
        prev_spec = pl.BlockSpec((nseq, CONV_K - 1, CONV_CH), lambda i, j: (i, 0, 0))
        tail_shape = jax.ShapeDtypeStruct((ni * nseq, CONV_K - 1, CONV_CH), F32)
        tail_spec = pl.BlockSpec((nseq, CONV_K - 1, CONV_CH), lambda i, j: (i, 0, 0))
    kern = functools.partial(_inproj_kernel, nseq=nseq, carry=carry, qv_transposed=qv_transposed)
    return pl.pallas_call(
        kern,
        grid=(ni, N_SECTIONS),
        in_specs=[
            pl.BlockSpec((tm, D_MODEL), row),
            pl.BlockSpec((1, D_MODEL), const),
            pl.BlockSpec((D_MODEL, SECTION), lambda i, j: (0, j)),
            pl.BlockSpec((1, SECTION), const),
            pl.BlockSpec((1, SECTION), const),
            pl.BlockSpec((IN_CH, IN_CH), const),
            pl.BlockSpec((tm, LANES), row),
            pl.BlockSpec((tm, LANES), row),
            prev_spec,
            pl.BlockSpec((CONV_K, CONV_CH), const),
            pl.BlockSpec((1, CONV_CH), const),
        ],
        out_specs=[
            qv_spec,
            pl.BlockSpec((tm, QK_WIDTH), row),
            pl.BlockSpec((tm, QK_WIDTH), row),
            pl.BlockSpec((tm, ATT_WIDTH), row),
            qv_spec,
            pl.BlockSpec((tm, CONV_CH), row),
            tail_spec,
        ],
        out_shape=[
            qv_shape,
            jax.ShapeDtypeStruct((t, QK_WIDTH), F32),
            jax.ShapeDtypeStruct((t, QK_WIDTH), BF16),
            jax.ShapeDtypeStruct((t, ATT_WIDTH), F32),
            qv_shape,
            jax.ShapeDtypeStruct((t, CONV_CH), BF16),
            tail_shape,
        ],
        scratch_shapes=[
            pltpu.VMEM((tm, D_MODEL), BF16),
            pltpu.VMEM((tm, CONV_CH), F32),
            pltpu.VMEM((tm, CONV_CH), F32),
            pltpu.VMEM((tm, CONV_CH), F32),
            pltpu.VMEM((seq + 8, CONV_CH), F32),
        ],
        compiler_params=_params(("arbitrary", "arbitrary"), 56),
        name="inproj_carry" if carry else "inproj_seqs",
    )(x2d, norm_g, w_in_bf, qg, kg, gmat, cos, sin, conv_prev, conv_w, conv_norm_g)


def _lambda_full(lamv_ref, lambda_init):
    lv = lamv_ref[...]
    a = jnp.sum(lv[0:1] * lv[1:2], axis=-1, keepdims=True)
    b = jnp.sum(lv[2:3] * lv[3:4], axis=-1, keepdims=True)
    return jnp.exp(a) - jnp.exp(b) + lambda_init


def _split_heads(q):
    lane = lax.broadcasted_iota(jnp.int32, q.shape, 1)
    zero = jnp.zeros_like(q)
    return jnp.where(lane < HEAD_DIM, q, zero), jnp.where(lane >= HEAD_DIM, q, zero)


def _diff_finish(acc, l, lam, g, lambda_init, rows):
    a = acc / l
    d = a[0:rows] - lam * a[rows:2 * rows]
    inv = lax.rsqrt(jnp.mean(d * d, axis=-1, keepdims=True) + NORM_EPS)
    return d * inv * g * (1.0 - lambda_init)


def _attn_prompt_kernel(qt_ref, k_ref, vt_ref, lamv_ref, g_ref, o_ref, qq_s, m_s, acc_s, s_s,
                        *, lambda_init):
    i = pl.program_id(1)
    t = ATT_T
    qt = qt_ref[0]
    feat = lax.broadcasted_iota(jnp.int32, qt.shape, 0)
    zero = jnp.zeros_like(qt)
    qq_s[:, 0:t] = jnp.where(feat < HEAD_DIM, qt, zero)
    qq_s[:, t:2 * t] = jnp.where(feat >= HEAD_DIM, qt, zero)
    m_s[...] = jnp.full(m_s.shape, MASKED, F32)
    acc_s[...] = jnp.zeros(acc_s.shape, F32)
    ones_rows = (lax.broadcasted_iota(jnp.int32, (ATT_ONES, t), 0) == 0).astype(BF16)

    nblk = 2 * t // ATT_CB

    def scores(kt, cb):
        k = k_ref[pl.ds(pl.multiple_of(kt * t, t), t), :]
        s = jnp.dot(k, qq_s[:, cb * ATT_CB:(cb + 1) * ATT_CB], preferred_element_type=F32)
        s_s[cb] = s.astype(BF16)

    def step(kt, diagonal):
        vt = jnp.concatenate([vt_ref[kt], ones_rows], axis=0)
        for cb in range(nblk):
            cs = slice(cb * ATT_CB, (cb + 1) * ATT_CB)
            s = s_s[cb]
            if diagonal:
                key = lax.broadcasted_iota(jnp.int32, s.shape, 0)
                qry = lax.broadcasted_iota(jnp.int32, s.shape, 1) + (cb * ATT_CB) % t
                s = s + jnp.where((key // CHUNK) <= (qry // CHUNK), 0.0, MASKED).astype(BF16)
            m_prev = m_s[:, cs]
            m_new = jnp.maximum(m_prev, jnp.max(s, axis=0, keepdims=True).astype(F32))
            alpha = jnp.exp2(m_prev - m_new)
            p = jnp.exp2(s - m_new.astype(BF16))
            m_s[:, cs] = m_new
            ahead = cb + ATT_AHEAD
            if ahead < nblk:
                scores(kt, ahead)
            elif not diagonal:
                scores(kt + 1, ahead - nblk)
            acc_s[:, cs] = alpha * acc_s[:, cs] + jnp.dot(vt, p, preferred_element_type=F32)

    def group_body(g, carry):
        for u in range(ATT_UNROLL):
            step(ATT_UNROLL * g + u, False)
        return carry

    def single_body(kt, carry):
        step(kt, False)
        return carry

    for cb in range(ATT_AHEAD):
        scores(0, cb)
    n_grouped = i // ATT_UNROLL * ATT_UNROLL
    lax.fori_loop(0, i // ATT_UNROLL, group_body, 0)
    lax.fori_loop(n_grouped, i, single_body, 0)
    step(i, True)
    lam = _lambda_full(lamv_ref, lambda_init)
    a = acc_s[0:V_DIM, :] / acc_s[V_DIM:V_DIM + 1, :]
    d = a[:, 0:t] - lam * a[:, t:2 * t]
    inv = lax.rsqrt(jnp.mean(d * d, axis=0, keepdims=True) + NORM_EPS)
    y = d * inv * g_ref[...] * (1.0 - lambda_init)
    o_ref[...] = y.T.astype(BF16)


def _attn_prompt(qt_bf, k_bf, vt_bf, lamv, subln_g_col, lambda_init):
    nq = qt_bf.shape[0]
    t = nq * ATT_T
    kern = functools.partial(_attn_prompt_kernel, lambda_init=lambda_init)
    return pl.pallas_call(
        kern,
        grid=(N_HEADS, nq),
        in_specs=[
            pl.BlockSpec((1, V_DIM, ATT_T), lambda h, i: (i, h, 0)),
            pl.BlockSpec((t, V_DIM), lambda h, i: (0, h)),
            pl.BlockSpec((nq, V_DIM, ATT_T), lambda h, i: (0, h, 0)),
            pl.BlockSpec((4, HEAD_DIM), lambda h, i: (0, 0)),
            pl.BlockSpec((V_DIM, 1), lambda h, i: (0, 0)),
        ],
        out_specs=pl.BlockSpec((ATT_T, V_DIM), lambda h, i: (i, h)),
        out_shape=jax.ShapeDtypeStruct((t, ATT_WIDTH), BF16),
        scratch_shapes=[
            pltpu.VMEM((V_DIM, 2 * ATT_T), BF16),
            pltpu.VMEM((1, 2 * ATT_T), F32),
            pltpu.VMEM((V_DIM + ATT_ONES, 2 * ATT_T), F32),
            pltpu.VMEM((2 * ATT_T // ATT_CB, ATT_T, ATT_CB), BF16),
        ],
        compiler_params=_params(("arbitrary", "arbitrary"), 48),
        name="attn_prompt",
    )(qt_bf, k_bf, vt_bf, lamv, subln_g_col)


def _attn_sample_kernel(q_ref, kc_ref, vc_ref, kn_ref, vn_ref, lamv_ref, g_ref, o_ref,
                        *, lambda_init, past):
    rows = q_ref.shape[0]
    lam = _lambda_full(lamv_ref, lambda_init)
    contract_last = (((1,), (1,)), ((), ()))
    for h in range(N_HEADS):
        hs = slice(h * V_DIM, (h + 1) * V_DIM)
        q1, q2 = _split_heads(q_ref[:, hs])
        qq = jnp.concatenate([q1, q2], axis=0)
        s_c = jnp.dot(qq, kc_ref[0, hs, :].astype(BF16), preferred_element_type=F32)
        s_n = lax.dot_general(qq, kn_ref[:, hs], contract_last, preferred_element_type=F32)
        r = lax.broadcasted_iota(jnp.int32, s_n.shape, 0)
        c = lax.broadcasted_iota(jnp.int32, s_n.shape, 1)
        s_n = jnp.where(((past + c) // CHUNK) <= ((past + r % rows) // CHUNK), s_n, MASKED)
        m = jnp.maximum(jnp.max(s_c, axis=-1, keepdims=True), jnp.max(s_n, axis=-1, keepdims=True))
        p_c = jnp.exp2(s_c - m)
        p_n = jnp.exp2(s_n - m)
        l = jnp.sum(p_c, axis=-1, keepdims=True) + jnp.sum(p_n, axis=-1, keepdims=True)
        v_c = vc_ref[0, pl.ds(h, past, stride=N_HEADS), :].astype(BF16)
        acc = (jnp.dot(p_c.astype(BF16), v_c, preferred_element_type=F32)
               + jnp.dot(p_n.astype(BF16), vn_ref[:, hs], preferred_element_type=F32))
        o_ref[:, hs] = _diff_finish(acc, l, lam, g_ref[...], lambda_init, rows).astype(BF16)


def _attn_sample(q_bf, kt_cache, v_cache, k_bf, v_bf, lamv, subln_g, lambda_init):
    nb, _, past = kt_cache.shape
    rows = q_bf.shape[0] // nb
    kern = functools.partial(_attn_sample_kernel, lambda_init=lambda_init, past=past)
    new_spec = pl.BlockSpec((rows, ATT_WIDTH), lambda b: (b, 0))
    return pl.pallas_call(
        kern,
        grid=(nb,),
        in_specs=[
            new_spec,
            pl.BlockSpec((1, QK_WIDTH, past), lambda b: (b, 0, 0)),
            pl.BlockSpec((1, past * N_HEADS, V_DIM), lambda b: (b, 0, 0)),
            new_spec, new_spec,
            pl.BlockSpec((4, HEAD_DIM), lambda b: (0, 0)),
            pl.BlockSpec((1, V_DIM), lambda b: (0, 0)),
        ],
        out_specs=new_spec,
        out_shape=jax.ShapeDtypeStruct(q_bf.shape, BF16),
        compiler_params=_params(("arbitrary",), 48),
        name="attn_sample",
    )(q_bf, kt_cache, v_cache, k_bf, v_bf, lamv, subln_g)


def _outproj_kernel(att_p_ref, yc_p_ref, x_p_ref, att_s_ref, yc_s_ref, x_s_ref,
                    wt_ref, wb_ref, ng_ref, rw_ref, rb_ref, x1_ref, xf_ref, route_ref, count_ref, count_s,
                    *, n_prompt_tiles):
    i = pl.program_id(0)
    shared = (wt_ref, wb_ref, ng_ref, rw_ref, rb_ref, x1_ref, xf_ref, route_ref, count_ref, count_s)

    @pl.when(i == 0)
    def _():
        count_s[...] = jnp.zeros(count_s.shape, F32)

    @pl.when(i < n_prompt_tiles)
    def _():
        _outproj_tile(att_p_ref, yc_p_ref, x_p_ref, *shared)

    @pl.when(i >= n_prompt_tiles)
    def _():
        _outproj_tile(att_s_ref, yc_s_ref, x_s_ref, *shared)


def _outproj_tile(att_ref, yc_ref, x_ref, wt_ref, wb_ref, ng_ref, rw_ref, rb_ref,
                  x1_ref, xf_ref, route_ref, count_ref, count_s):
    o = (jnp.dot(att_ref[...], wt_ref[...], preferred_element_type=F32)
         + jnp.dot(yc_ref[...], wb_ref[...], preferred_element_type=F32))
    x1 = x_ref[...] + o
    x1_ref[...] = x1
    xf = x1 * lax.rsqrt(jnp.mean(x1 * x1, axis=-1, keepdims=True) + NORM_EPS) * ng_ref[...]
    for c in range(SLAB):
        lo = slice(c * LANES, (c + 1) * LANES)
        hi = slice(PACK_COLS + c * LANES, PACK_COLS + (c + 1) * LANES)
        xf_ref[c] = _pack_bf16_pair(xf[:, lo], xf[:, hi])
    logits =jnp.dot(xf.astype(BF16), rw_ref[...], preferred_element_type=F32) + rb_ref[...]

    lane = lax.broadcasted_iota(jnp.int32, logits.shape, 1)
    neg = -jnp.inf
    is_group = lane < N_GROUPS
    gl = jnp.where(is_group, logits, neg)
    gmax = jnp.max(gl, axis=-1, keepdims=True)
    grp = jnp.min(jnp.where(gl == gmax, lane, ROUTE_LANES), axis=-1, keepdims=True)
    gsum = jnp.sum(jnp.where(is_group, jnp.exp(gl - gmax), 0.0), axis=-1, keepdims=True)
    g_w = 1.0 / gsum
    e_lane = lane - N_GROUPS
    in_grp = (e_lane >= 0) & (e_lane < N_EXPERTS) & ((e_lane // EXPERTS_PER_GROUP) == grp)
    el = jnp.where(in_grp, logits, neg)
    t1 = jnp.max(el, axis=-1, keepdims=True)
    i1 = jnp.min(jnp.where(el == t1, lane, ROUTE_LANES), axis=-1, keepdims=True)
    el2 = jnp.where(lane == i1, neg, el)
    t2 = jnp.max(el2, axis=-1, keepdims=True)
    i2 = jnp.min(jnp.where(el2 == t2, lane, ROUTE_LANES), axis=-1, keepdims=True)
    r21 = jnp.exp(t2 - t1)
    w0 = g_w / (1.0 + r21)
    w1 = g_w * r21 / (1.0 + r21)
    e0 = (i1 - N_GROUPS).astype(F32)
    e1 = (i2 - N_GROUPS).astype(F32)
    rows = logits.shape[0]
    picked0 = lane == i1
    picked1 = lane == i2
    member = (picked0 | picked1).astype(BF16)
    earlier = (lax.broadcasted_iota(jnp.int32, (rows, rows), 1)
               < lax.broadcasted_iota(jnp.int32, (rows, rows), 0)).astype(BF16)
    before = jnp.dot(earlier, member, preferred_element_type=F32) + count_s[...]
    r0 = jnp.sum(jnp.where(picked0, before, 0.0), axis=-1, keepdims=True)
    r1 = jnp.sum(jnp.where(picked1, before, 0.0), axis=-1, keepdims=True)
    count_s[...] += jnp.sum(member.astype(F32), axis=0, keepdims=True)
    count_ref[...] = count_s[...]
    route_ref[...] = jnp.where(lane == 0, e0, jnp.where(lane == 1, e1,
                               jnp.where(lane == 2, w0, jnp.where(lane == 3, w1,
                               jnp.where(lane == 4, r0, jnp.where(lane == 5, r1, 0.0))))))


def _outproj(att_p, yc_p, x_p, att_s, yc_s, x_s, w_top, w_bot, norm_g, rw, rb):
    tm = OUT_TM
    n_p = x_p.shape[0] // tm
    n_s = x_s.shape[0] // tm
    t_all = x_p.shape[0] + x_s.shape[0]
    prow = lambda i: (jnp.minimum(i, n_p - 1), 0)
    srow = lambda i: (jnp.maximum(i - n_p, 0), 0)
    row = lambda i: (i, 0)
    const = lambda i: (0, 0)
    kern = functools.partial(_outproj_kernel, n_prompt_tiles=n_p)
    return pl.pallas_call(
        kern,
        grid=(n_p + n_s,),
        in_specs=[
            pl.BlockSpec((tm, ATT_WIDTH), prow),
            pl.BlockSpec((tm, CONV_CH), prow),
            pl.BlockSpec((tm, D_MODEL), prow),
            pl.BlockSpec((tm, ATT_WIDTH), srow),
            pl.BlockSpec((tm, CONV_CH), srow),
            pl.BlockSpec((tm, D_MODEL), srow),
            pl.BlockSpec((ATT_WIDTH, D_MODEL), const),
            pl.BlockSpec((CONV_CH, D_MODEL), const),
            pl.BlockSpec((1, D_MODEL), const),
            pl.BlockSpec((D_MODEL, ROUTE_LANES), const),
            pl.BlockSpec((1, ROUTE_LANES), const),
        ],
        out_specs=[
            pl.BlockSpec((tm, D_MODEL), row),
            pl.BlockSpec((SLAB, tm, LANES), lambda i: (0, i, 0)),
            pl.BlockSpec((tm, ROUTE_LANES), row),
            pl.BlockSpec((1, ROUTE_LANES), const),
        ],
        out_shape=[
            jax.ShapeDtypeStruct((t_all, D_MODEL), F32),
            jax.ShapeDtypeStruct((SLAB, t_all, LANES), U32),
            jax.ShapeDtypeStruct((t_all, ROUTE_LANES), F32),
            jax.ShapeDtypeStruct((1, ROUTE_LANES), F32),
        ],
        scratch_shapes=[pltpu.VMEM((1, ROUTE_LANES), F32)],
        compiler_params=_params(("arbitrary",), 52),
        name="outproj_router",
    )(att_p, yc_p, x_p, att_s, yc_s, x_s, w_top, w_bot, norm_g, rw, rb)


def _dispatch_copy(xf_ref, xs_hbm, sem, src_row, dst_row):
    return pltpu.make_async_copy(xf_ref.at[:, src_row, :],
                                 xs_hbm.at[pl.ds(dst_row * SLAB, SLAB)], sem)


def _zero_block_copy(zero_s, xs_hbm, sem, blk):
    span = MOE_BM * SLAB
    return pltpu.make_async_copy(zero_s, xs_hbm.at[pl.ds(blk * span, span)], sem)


def _dispatch_kernel(dest_ref, zero_blk_ref, xf_ref, xs_hbm, zero_s, sem):
    tm = DISPATCH_TM

    @pl.when(pl.program_id(0) == 0)
    def _():
        zero_s[...] = jnp.zeros(zero_s.shape, U32)

        def fill(j, start):
            blk = zero_blk_ref[0, 0, j]

            @pl.when(blk >= 0)
            def _():
                copy = _zero_block_copy(zero_s, xs_hbm, sem, blk)
                if start:
                    copy.start()
                else:
                    copy.wait()

        lax.fori_loop(0, zero_blk_ref.shape[-1], lambda j, c: (fill(j, True), c)[1], 0)
        lax.fori_loop(0, zero_blk_ref.shape[-1], lambda j, c: (fill(j, False), c)[1], 0)

    def issue(t, carry):
        for k in range(TOP_K):
            _dispatch_copy(xf_ref, xs_hbm, sem, t, dest_ref[0, 0, TOP_K * t + k]).start()
        return carry

    def drain(t, carry):
        for k in range(TOP_K):
            _dispatch_copy(xf_ref, xs_hbm, sem, t, dest_ref[0, 0, TOP_K * t + k]).wait()
        return carry

    lax.fori_loop(0, tm, issue, 0, unroll=ROW_DMA_UNROLL)
    lax.fori_loop(0, tm, drain, 0, unroll=ROW_DMA_UNROLL)


def _dispatch(dest, zero_blk, xf, cap):
    t_all = xf.shape[1]
    tm = DISPATCH_TM
    dest3 = dest.reshape(t_all // tm, 1, TOP_K * tm)
    zero_blk3 = zero_blk.reshape(1, 1, -1)
    return pl.pallas_call(
        _dispatch_kernel,
        grid=(t_all // tm,),
        in_specs=[
            pl.BlockSpec((1, 1, TOP_K * tm), lambda i: (i, 0, 0), memory_space=pltpu.SMEM),
            pl.BlockSpec(zero_blk3.shape, lambda i: (0, 0, 0), memory_space=pltpu.SMEM),
            pl.BlockSpec((SLAB, tm, LANES), lambda i: (0, i, 0)),
        ],
        out_specs=pl.BlockSpec(memory_space=pl.ANY),
        out_shape=jax.ShapeDtypeStruct((cap * SLAB, LANES), U32),
        scratch_shapes=[pltpu.VMEM((MOE_BM * SLAB, LANES), U32), pltpu.SemaphoreType.DMA(())],
        compiler_params=_params(("arbitrary",), 24),
        name="moe_dispatch",
    )(dest3, zero_blk3, xf)


def _expert_weight_copies(hbm_refs, stage_refs, sems, e):
    return [pltpu.make_async_copy(hbm.at[e], stage, sems.at[n])
            for n, (hbm, stage) in enumerate(zip(hbm_refs, stage_refs))]


def _experts_kernel(blk_e_ref, first_ref, next_e_ref, nused_ref, x_ref, wg_hbm, wu_hbm, wd_hbm, o_ref,
                    wg_f, wu_f, wd_f, wgu_s, wd_s, sems):
    i = pl.program_id(0)
    hbm_refs = (wg_hbm, wu_hbm, wd_hbm)
    stage_refs = (wg_f, wu_f, wd_f)

    @pl.when(i == 0)
    def _():
        for copy in _expert_weight_copies(hbm_refs, stage_refs, sems, blk_e_ref[0]):
            copy.start()

    @pl.when(i < nused_ref[0])
    def _():
        @pl.when(first_ref[i] == 1)
        def _():
            for copy in _expert_weight_copies(hbm_refs, stage_refs, sems, blk_e_ref[i]):
                copy.wait()
            wgu_s[:, 0:D_EXPERT] = wg_f[...].astype(BF16)
            wgu_s[:, D_EXPERT:2 * D_EXPERT] = wu_f[...].astype(BF16)
            wd_s[...] = wd_f[...].astype(BF16)

            @pl.when(next_e_ref[i] >= 0)
            def _():
                for copy in _expert_weight_copies(hbm_refs, stage_refs, sems, next_e_ref[i]):
                    copy.start()

        x_lo, x_hi = _unpack_bf16_pair(x_ref[...].reshape(MOE_BM, SLAB, LANES).reshape(MOE_BM, PACK_COLS))
        x = jnp.concatenate([x_lo, x_hi], axis=1).astype(BF16)
        gu = jnp.dot(x, wgu_s[...], preferred_element_type=F32)
        g = gu[:, 0:D_EXPERT]
        u = gu[:, D_EXPERT:2 * D_EXPERT]
        h = g / (1.0 + jnp.exp(-g)) * u
        out = jnp.dot(h.astype(BF16), wd_s[...], preferred_element_type=F32)
        packed = _pack_bf16_pair(out[:, 0:PACK_COLS], out[:, PACK_COLS:D_MODEL])
        o_ref[...] = packed.reshape(MOE_BM, SLAB, LANES).reshape(MOE_BM * SLAB, LANES)

    @pl.when(i >= nused_ref[0])
    def _():
        o_ref[...] = jnp.zeros(o_ref.shape, U32)


def _experts(blk_e, first, next_e, nused, xs, w_gate, w_up, w_down):
    cap = xs.shape[0] // SLAB
    bm = MOE_BM
    rows = lambda i, be, fi, ne, nu: (jnp.minimum(i, nu[0] - 1), 0)
    grid_spec = pltpu.PrefetchScalarGridSpec(
        num_scalar_prefetch=4,
        grid=(cap // bm,),
        in_specs=[
            pl.BlockSpec((bm * SLAB, LANES), rows),
            pl.BlockSpec(memory_space=pl.ANY),
            pl.BlockSpec(memory_space=pl.ANY),
            pl.BlockSpec(memory_space=pl.ANY),
        ],
        out_specs=pl.BlockSpec((bm * SLAB, LANES), lambda i, be, fi, ne, nu: (i, 0)),
        scratch_shapes=[
            pltpu.VMEM((D_MODEL, D_EXPERT), F32),
            pltpu.VMEM((D_MODEL, D_EXPERT), F32),
            pltpu.VMEM((D_EXPERT, D_MODEL), F32),
            pltpu.VMEM((D_MODEL, 2 * D_EXPERT), BF16),
            pltpu.VMEM((D_EXPERT, D_MODEL), BF16),
            pltpu.SemaphoreType.DMA((3,)),
        ],
    )
    return pl.pallas_call(
        _experts_kernel,
        grid_spec=grid_spec,
        out_shape=jax.ShapeDtypeStruct((cap * SLAB, LANES), U32),
        compiler_params=_params(("arbitrary",), 48),
        name="moe_experts",
    )(blk_e, first, next_e, nused, xs, w_gate, w_up, w_down)


def _combine_copy(outs_hbm, g_s, sems, slot, src_row, k, t):
    return pltpu.make_async_copy(outs_hbm.at[pl.ds(src_row * SLAB, SLAB)],
                                 g_s.at[slot, k, :, t, :], sems.at[slot])


def _combine_kernel(dest_ref, dest_next_ref, x1_ref, route_ref, outs_hbm, y_ref, g_s, sems, *, n_tiles):
    i = pl.program_id(0)
    tm = x1_ref.shape[0]
    slot = i % 2

    def gather(table_ref, into, start):
        def body(t, carry):
            for k in range(TOP_K):
                copy = _combine_copy(outs_hbm, g_s, sems, into, table_ref[0, 0, TOP_K * t + k], k, t)
                if start:
                    copy.start()
                else:
                    copy.wait()
            return carry
        lax.fori_loop(0, tm, body, 0, unroll=ROW_DMA_UNROLL)

    @pl.when(i == 0)
    def _():
        gather(dest_ref, 0, True)

    @pl.when(i + 1 < n_tiles)
    def _():
        gather(dest_next_ref, 1 - slot, True)

    gather(dest_ref, slot, False)
    r = route_ref[...]
    w0 = jnp.broadcast_to(r[:, 2:3], (tm, LANES))
    w1 = jnp.broadcast_to(r[:, 3:4], (tm, LANES))
    for c in range(SLAB):
        lo = slice(c * LANES, (c + 1) * LANES)
        hi = slice(PACK_COLS + c * LANES, PACK_COLS + (c + 1) * LANES)
        lo0, hi0 = _unpack_bf16_pair(g_s[slot, 0, c])
        lo1, hi1 = _unpack_bf16_pair(g_s[slot, 1, c])
        y_ref[:, lo] = x1_ref[:, lo] + w0 * lo0 + w1 * lo1
        y_ref[:, hi] = x1_ref[:, hi] + w0 * hi0 + w1 * hi1


def _combine(dest, x1, route, outs, row_off, t):
    tm = OUT_TM
    t_all = x1.shape[0]
    off = row_off // tm
    dest3 = dest.reshape(t_all // tm, 1, TOP_K * tm)
    n = t // tm
    return pl.pallas_call(
        functools.partial(_combine_kernel, n_tiles=n),
        grid=(n,),
        in_specs=[
            pl.BlockSpec((1, 1, TOP_K * tm), lambda i: (i + off, 0, 0), memory_space=pltpu.SMEM),
            pl.BlockSpec((1, 1, TOP_K * tm), lambda i: (jnp.minimum(i + 1, n - 1) + off, 0, 0),
                         memory_space=pltpu.SMEM),
            pl.BlockSpec((tm, D_MODEL), lambda i: (i + off, 0)),
            pl.BlockSpec((tm, ROUTE_LANES), lambda i: (i + off, 0)),
            pl.BlockSpec(memory_space=pl.ANY),
        ],
        out_specs=pl.BlockSpec((tm, D_MODEL), lambda i: (i, 0)),
        out_shape=jax.ShapeDtypeStruct((t, D_MODEL), F32),
        scratch_shapes=[pltpu.VMEM((2, TOP_K, SLAB, tm, LANES), U32), pltpu.SemaphoreType.DMA((2,))],
        compiler_params=_params(("arbitrary",), 32),
        name="moe_combine",
    )(dest3, dest3, x1, route, outs)


def _routing_tables(route, lane_counts, n_blocks):
    bm = MOE_BM
    flat_e = route[:, 0:TOP_K].astype(jnp.int32).reshape(-1)
    rank = route[:, 2 * TOP_K:3 * TOP_K].astype(jnp.int32).reshape(-1)
    counts = lane_counts[0, N_GROUPS:N_GROUPS + N_EXPERTS].astype(jnp.int32)
    padded = (counts + bm - 1) // bm * bm
    pad_end = jnp.cumsum(padded)
    pad_start = pad_end - padded
    experts = jnp.arange(N_EXPERTS, dtype=jnp.int32)
    onehot = (flat_e[:, None] == experts[None, :]).astype(BF16)
    start_blk = jnp.dot(onehot, (pad_start // bm).astype(BF16), preferred_element_type=F32)
    dest = start_blk.astype(jnp.int32) * bm + rank
    nused = (pad_end[-1] // bm).astype(jnp.int32)
    blk = jnp.arange(n_blocks, dtype=jnp.int32)
    blk_row = jnp.minimum(blk, nused - 1) * bm
    blk_e = jnp.sum((pad_end[None, :] <= blk_row[:, None]).astype(jnp.int32), axis=1)
    blk_e = jnp.minimum(blk_e, N_EXPERTS - 1)
    first = jnp.concatenate([jnp.ones((1,), jnp.int32),
                             (blk_e[1:] != blk_e[:-1]).astype(jnp.int32)])
    later_owner = jnp.where((counts[None, :] > 0) & (experts[None, :] > experts[:, None]),
                            experts[None, :], N_EXPERTS)
    next_owner = jnp.min(later_owner, axis=1)
    next_owner = jnp.where(next_owner == N_EXPERTS, -1, next_owner).astype(jnp.int32)
    last_blk = jnp.where(counts > 0, pad_end // bm - 1, -1).astype(jnp.int32)
    tail = nused + jnp.arange(N_EXPERTS, dtype=jnp.int32)
    tail_blk = jnp.where(tail < n_blocks, tail, -1)
    zero_blk = jnp.concatenate([last_blk, tail_blk])
    return dest, zero_blk, blk_e, first, next_owner[blk_e], nused.reshape(1)


def _rope_tables(pos):
    half = HEAD_DIM // 2
    inv = ROPE_THETA ** (-jnp.arange(half, dtype=F32) / half)
    ang = pos.astype(F32)[:, None] * inv[None, :]
    cos = jnp.cos(ang)
    sin = jnp.sin(ang)
    cos_h = jnp.concatenate([cos, cos], axis=-1)
    sin_h = jnp.concatenate([-sin, sin], axis=-1)
    reps = LANES // HEAD_DIM
    return jnp.tile(cos_h, (1, reps)), jnp.tile(sin_h, (1, reps))


def kernel(x_prompt, x_sample, cache_k, cache_v, state_conv, norm_mix_g, w_in, q_norm_g, k_norm_g, lambda_q1, lambda_k1, lambda_q2, lambda_k2, subln_g, conv_w, conv_norm_g, w_out, norm_ffn_g, router_group_w, router_group_b, router_expert_w, router_expert_b, expert_w_gate, expert_w_up, expert_w_down):
    assert w_in.shape[0] == 1, "single-layer step"
    bp, sp, _ = x_prompt.shape
    bs, ss, _ = x_sample.shape
    past = cache_k.shape[2]
    assert bp == 1 and sp % ATT_T == 0
    tp = bp * sp
    ts = bs * ss
    t_all = tp + ts
    lambda_init = 0.8 - 0.6 * math.exp(-0.3 * 0)

    w_in_bf = w_in[0].astype(BF16)
    w_out_bf = w_out[0].astype(BF16)
    w_top, w_bot = w_out_bf[:ATT_WIDTH], w_out_bf[ATT_WIDTH:]
    ng = norm_mix_g[0].reshape(1, D_MODEL)
    qg = jnp.tile(q_norm_g[0], QK_WIDTH // HEAD_DIM).reshape(1, QK_WIDTH)
    kg = jnp.tile(k_norm_g[0], QK_WIDTH // HEAD_DIM).reshape(1, QK_WIDTH)
    head_of = jnp.arange(IN_CH, dtype=jnp.int32) // HEAD_DIM
    gmat = jnp.where(head_of[:, None] == head_of[None, :], 1.0 / HEAD_DIM, 0.0).astype(BF16)
    lamv = jnp.stack([lambda_q1[0], lambda_k1[0], lambda_q2[0], lambda_k2[0]]).astype(F32)
    sg = subln_g[0].reshape(1, V_DIM)
    cw = conv_w[0]
    cng = conv_norm_g[0].reshape(1, CONV_CH)
    cos_p, sin_p = _rope_tables(jnp.arange(sp, dtype=jnp.int32))
    cos_s, sin_s = _rope_tables(jnp.tile(past + jnp.arange(ss, dtype=jnp.int32), bs))

    zero_conv = jnp.zeros((1, CONV_K - 1, CONV_CH), F32)
    qt_p, kf_p, kb_p, vf_p, vt_p, yc_p, tail_p = _inproj(
        x_prompt.reshape(tp, D_MODEL), ng, w_in_bf, qg, kg, gmat, cos_p, sin_p, zero_conv, cw, cng,
        tm=ATT_T, nseq=1, carry=True, qv_transposed=True)
    att_p = _attn_prompt(qt_p, kb_p, vt_p, lamv, subln_g[0].reshape(V_DIM, 1), lambda_init)

    seqs_per_tile = ATT_T // ss
    q_s, kf_s, kb_s, vf_s, vb_s, yc_s, tail_s = _inproj(
        x_sample.reshape(ts, D_MODEL), ng, w_in_bf, qg, kg, gmat, cos_s, sin_s, state_conv[0], cw, cng,
        tm=ATT_T, nseq=seqs_per_tile, carry=False, qv_transposed=False)
    kt_cache = jnp.transpose(cache_k[0], (0, 2, 3, 4, 1)).reshape(bs, QK_WIDTH, past)
    att_s = _attn_sample(q_s, kt_cache, cache_v[0].reshape(bs, past * N_HEADS, V_DIM),
                         kb_s, vb_s, lamv, sg, lambda_init)

    rw = jnp.zeros((D_MODEL, ROUTE_LANES), F32)
    rw = rw.at[:, 0:N_GROUPS].set(router_group_w[0]).at[:, N_GROUPS:N_GROUPS + N_EXPERTS].set(router_expert_w[0])
    rb = jnp.zeros((1, ROUTE_LANES), F32)
    rb = rb.at[0, 0:N_GROUPS].set(router_group_b[0]).at[0, N_GROUPS:N_GROUPS + N_EXPERTS].set(router_expert_b[0])
    nf = norm_ffn_g[0].reshape(1, D_MODEL)
    rw_bf = rw.astype(BF16)
    x1, xf, route, lane_counts = _outproj(att_p, yc_p, x_prompt.reshape(tp, D_MODEL), att_s, yc_s,
                                          x_sample.reshape(ts, D_MODEL), w_top, w_bot, nf, rw_bf, rb)

    n = t_all * TOP_K
    n_blocks = n // MOE_BM + N_EXPERTS
    dest, zero_blk, blk_e, first, next_e, nused = _routing_tables(route, lane_counts, n_blocks)
    xs = _dispatch(dest, zero_blk, xf, n_blocks * MOE_BM)
    outs = _experts(blk_e, first, next_e, nused, xs, expert_w_gate[0], expert_w_up[0], expert_w_down[0])
    y_p = _combine(dest, x1, route, outs, 0, tp)
    y_s = _combine(dest, x1, route, outs, tp, ts)

    return (y_p.reshape(bp, sp, D_MODEL),
            y_s.reshape(bs, ss, D_MODEL),
            kf_p.reshape(1, bp, sp, N_HEADS, 2, HEAD_DIM),
            vf_p.reshape(1, bp, sp, N_HEADS, V_DIM),
            tail_p.reshape(1, bp, CONV_K - 1, CONV_CH),
            kf_s.reshape(1, bs, ss, N_HEADS, 2, HEAD_DIM),
            vf_s.reshape(1, bs, ss, N_HEADS, V_DIM),
            tail_s.reshape(1, bs, CONV_K - 1, CONV_CH))
```

```python
import functools
import math

import jax
import jax.numpy as jnp
from jax import lax
from jax.experimental import pallas as pl
from jax.experimental.pallas import tpu as pltpu

F32 = jnp.float32
BF16 = jnp.bfloat16

D_MODEL = 2048
CHUNK = 64
HEAD_DIM = 64
V_DIM = 2 * HEAD_DIM
N_HEADS = 8
QK_WIDTH = N_HEADS * 2 * HEAD_DIM
ATT_WIDTH = N_HEADS * V_DIM
CONV_CH = 1024
CONV_K = 3
ROPE_THETA = 10000.0
N_GROUPS = 8
EXPERTS_PER_GROUP = 8
N_EXPERTS = N_GROUPS * EXPERTS_PER_GROUP
TOP_K = 2
D_EXPERT = D_MODEL // 4
NORM_EPS = 1e-6
SECTION = 1024
N_SECTIONS = 6

LANES = 128
IN_CH = 256
ATT_AHEAD = 2
ATT_ONES = 16
ATT_QPS = 4
ATT_UNROLL = 4
ATT_CB = 512
ATT_T = 512
OUT_TM = 256
MOE_BM = 256
DISPATCH_TM = 512
ROW_DMA_UNROLL = 8
PACK_COLS = D_MODEL // 2
SLAB = PACK_COLS // LANES
U32 = jnp.uint32
ROUTE_LANES = LANES
MASKED = -1e30
Q_SCALE = HEAD_DIM ** -0.5 * math.log2(math.e)
MIB = 1024 * 1024


def _pack_bf16_pair(lo, hi):
    lo_bits = lax.bitcast_convert_type(lo.astype(BF16).astype(F32), U32)
    hi_bits = lax.bitcast_convert_type(hi.astype(BF16).astype(F32), U32)
    return hi_bits | (lo_bits >> 16)


def _unpack_bf16_pair(words):
    lo = lax.bitcast_convert_type(words << 16, F32)
    hi = lax.bitcast_convert_type(words & U32(0xFFFF0000), F32)
    return lo, hi


def _params(sem, vmem_mib):
    return pltpu.CompilerParams(dimension_semantics=sem, vmem_limit_bytes=vmem_mib * MIB)


def _inproj_kernel(x_ref, ng_ref, w_ref, qg_ref, kg_ref, gmat_ref, cos_ref, sin_ref,
                   prev_ref, cw_ref, cng_ref,
                   q_ref, kf_ref, kb_ref, vf_ref, vb_ref, yc_ref, tail_ref,
                   xn_s, gb_s, c_s, z_s, u_s, *, nseq, carry, qv_transposed):
    i = pl.program_id(0)
    j = pl.program_id(1)
    tm = x_ref.shape[0]
    seq = tm // nseq

    @pl.when(j == 0)
    def _():
        x = x_ref[...]
        inv = lax.rsqrt(jnp.mean(x * x, axis=-1, keepdims=True) + NORM_EPS)
        xn_s[...] = (x * inv * ng_ref[...]).astype(BF16)
        if carry:
            @pl.when(i == 0)
            def _():
                u_s[6:8, :] = prev_ref[0]

    def chunk_dot(c):
        return jnp.dot(xn_s[...], w_ref[:, c * IN_CH:(c + 1) * IN_CH], preferred_element_type=F32)

    def pipelined(epilogue):
        acc = chunk_dot(0)
        for c in range(SECTION // IN_CH):
            nxt = chunk_dot(c + 1) if c + 1 < SECTION // IN_CH else None
            epilogue(acc, slice(c * IN_CH, (c + 1) * IN_CH))
            acc = nxt

    def head_norm_rope(a, g):
        ms = jnp.dot((a * a).astype(BF16), gmat_ref[...], preferred_element_type=F32)
        y = a * lax.rsqrt(ms + NORM_EPS) * g
        reps = IN_CH // LANES
        cos = jnp.concatenate([cos_ref[...]] * reps, axis=1)
        sin = jnp.concatenate([sin_ref[...]] * reps, axis=1)
        lane = lax.broadcasted_iota(jnp.int32, y.shape, 1)
        first = (lane & (HEAD_DIM - 1)) < HEAD_DIM // 2
        partner = jnp.where(first, pltpu.roll(y, IN_CH - HEAD_DIM // 2, 1),
                            pltpu.roll(y, HEAD_DIM // 2, 1))
        return y * cos + partner * sin

    for section in range(N_SECTIONS):

        def epilogue(acc, cols, section=section):
            if section == 0:
                q = head_norm_rope(acc, qg_ref[:, cols]) * Q_SCALE
                if qv_transposed:
                    q_ref[0, cols, :] = q.T.astype(BF16)
                else:
                    q_ref[:, cols] = q.astype(BF16)
            elif section == 1:
                k = head_norm_rope(acc, kg_ref[:, cols])
                kf_ref[:, cols] = k
                kb_ref[:, cols] = k.astype(BF16)
            elif section == 2:
                vf_ref[:, cols] = acc
                if qv_transposed:
                    vb_ref[0, cols, :] = acc.T.astype(BF16)
                else:
                    vb_ref[:, cols] = acc.astype(BF16)
            elif section == 3:
                gb_s[:, cols] = acc
            elif section == 4:
                c_s[:, cols] = acc
            else:
                u = c_s[:, cols] * acc
                w0 = cw_ref[0:1, cols]
                w1 = cw_ref[1:2, cols]
                w2 = cw_ref[2:3, cols]
                for s in range(nseq):
                    rows = slice(s * seq, (s + 1) * seq)
                    if not carry:
                        u_s[6:8, cols] = prev_ref[s, :, cols]
                    u_s[8:8 + seq, cols] = u[rows]
                    conv = (w0 * u_s[6:6 + seq, cols] + w1 * u_s[7:7 + seq, cols]
                            + w2 * u_s[8:8 + seq, cols])
                    z_s[rows, cols] = gb_s[rows, cols] * conv
                    tail = u_s[seq + 6:seq + 8, cols]
                    tail_ref[s, :, cols] = tail
                    if carry:
                        u_s[6:8, cols] = tail

        @pl.when(j == section)
        def _(section=section, epilogue=epilogue):
            pipelined(epilogue)
            if section == N_SECTIONS - 1:
                z = z_s[...]
                inv = lax.rsqrt(jnp.mean(z * z, axis=-1, keepdims=True) + NORM_EPS)
                yc_ref[...] = (z * inv * cng_ref[...]).astype(BF16)


def _inproj(x2d, norm_g, w_in_bf, qg, kg, gmat, cos, sin, conv_prev, conv_w, conv_norm_g,
            *, tm, nseq, carry, qv_transposed):
    t = x2d.shape[0]
    ni = t // tm
    seq = tm // nseq
    row = lambda i, j: (i, 0)
    const = lambda i, j: (0, 0)
    if qv_transposed:
        qv_shape = jax.ShapeDtypeStruct((ni, QK_WIDTH, tm), BF16)
        qv_spec = pl.BlockSpec((1, QK_WIDTH, tm), lambda i, j: (i, 0, 0))
    else:
        qv_shape = jax.ShapeDtypeStruct((t, QK_WIDTH), BF16)
        qv_spec = pl.BlockSpec((tm, QK_WIDTH), row)
    if carry:
        prev_spec = pl.BlockSpec((1, CONV_K - 1, CONV_CH), lambda i, j: (0, 0, 0))
        tail_shape = jax.ShapeDtypeStruct((1, CONV_K - 1, CONV_CH), F32)
        tail_spec = pl.BlockSpec((1, CONV_K - 1, CONV_CH), lambda i, j: (0, 0, 0))
    else:
        prev_spec = pl.BlockSpec((nseq, CONV_K - 1, CONV_CH), lambda i, j: (i, 0, 0))
        tail_shape = jax.ShapeDtypeStruct((ni * nseq, CONV_K - 1, CONV_CH), F32)
        tail_spec = pl.BlockSpec((nseq, CONV_K - 1, CONV_CH), lambda i, j: (i, 0, 0))
    kern = functools.partial(_inproj_kernel, nseq=nseq, carry=carry, qv_transposed=qv_transposed)
    return pl.pallas_call(
        kern,
        grid=(ni, N_SECTIONS),
        in_specs=[
            pl.BlockSpec((tm, D_MODEL), row),
            pl.BlockSpec((1, D_MODEL), const),
            pl.BlockSpec((D_MODEL, SECTION), lambda i, j: (0, j)),
            pl.BlockSpec((1, SECTION), const),
            pl.BlockSpec((1, SECTION), const),
            pl.BlockSpec((IN_CH, IN_CH), const),
            pl.BlockSpec((tm, LANES), row),
            pl.BlockSpec((tm, LANES), row),
            prev_spec,
            pl.BlockSpec((CONV_K, CONV_CH), const),
            pl.BlockSpec((1, CONV_CH), const),
        ],
        out_specs=[
            qv_spec,
            pl.BlockSpec((tm, QK_WIDTH), row),
            pl.BlockSpec((tm, QK_WIDTH), row),
            pl.BlockSpec((tm, ATT_WIDTH), row),
            qv_spec,
            pl.BlockSpec((tm, CONV_CH), row),
            tail_spec,
        ],
        out_shape=[
            qv_shape,
            jax.ShapeDtypeStruct((t, QK_WIDTH), F32),
            jax.ShapeDtypeStruct((t, QK_WIDTH), BF16),
            jax.ShapeDtypeStruct((t, ATT_WIDTH), F32),
            qv_shape,
            jax.ShapeDtypeStruct((t, CONV_CH), BF16),
            tail_shape,
        ],
        scratch_shapes=[
            pltpu.VMEM((tm, D_MODEL), BF16),
            pltpu.VMEM((tm, CONV_CH), F32),
            pltpu.VMEM((tm, CONV_CH), F32),
            pltpu.VMEM((tm, CONV_CH), F32),
            pltpu.VMEM((seq + 8, CONV_CH), F32),
        ],
        compiler_params=_params(("arbitrary", "arbitrary"), 56),
        name="inproj_carry" if carry else "inproj_seqs",
    )(x2d, norm_g, w_in_bf, qg, kg, gmat, cos, sin, conv_prev, conv_w, conv_norm_g)


def _lambda_full(lamv_ref, lambda_init):
    lv = lamv_ref[...]
    a = jnp.sum(lv[0:1] * lv[1:2], axis=-1, keepdims=True)
    b = jnp.sum(lv[2:3] * lv[3:4], axis=-1, keepdims=True)
    return jnp.exp(a) - jnp.exp(b) + lambda_init


def _split_heads(q):
    lane = lax.broadcasted_iota(jnp.int32, q.shape, 1)
    zero = jnp.zeros_like(q)
    return jnp.where(lane < HEAD_DIM, q, zero), jnp.where(lane >= HEAD_DIM, q, zero)


def _diff_finish(acc, l, lam, g, lambda_init, rows):
    a = acc / l
    d = a[0:rows] - lam * a[rows:2 * rows]
    inv = lax.rsqrt(jnp.mean(d * d, axis=-1, keepdims=True) + NORM_EPS)
    return d * inv * g * (1.0 - lambda_init)


def _attn_prompt_kernel(qt_ref, k_ref, vt_ref, lamv_ref, g_ref, o_ref, qq_s, m_s, acc_s, s_s,
                        *, lambda_init):
    for sub in range(ATT_QPS):
        _attn_prompt_tile(ATT_QPS * pl.program_id(1) + sub, qt_ref.at[sub], k_ref, vt_ref, lamv_ref, g_ref,
                          o_ref.at[pl.ds(sub * ATT_T, ATT_T)], qq_s.at[sub], m_s.at[sub], acc_s.at[sub],
                          s_s.at[sub], lambda_init=lambda_init)


def _attn_prompt_tile(i, qt_ref, k_ref, vt_ref, lamv_ref, g_ref, o_ref, qq_s, m_s, acc_s, s_s,
                      *, lambda_init):
    t = ATT_T
    qt = qt_ref[...]
    feat = lax.broadcasted_iota(jnp.int32, qt.shape, 0)
    zero = jnp.zeros_like(qt)
    qq_s[:, 0:t] = jnp.where(feat < HEAD_DIM, qt, zero)
    qq_s[:, t:2 * t] = jnp.where(feat >= HEAD_DIM, qt, zero)
    m_s[...] = jnp.full(m_s.shape, MASKED, F32)
    acc_s[...] = jnp.zeros(acc_s.shape, F32)
    ones_rows = (lax.broadcasted_iota(jnp.int32, (ATT_ONES, t), 0) == 0).astype(BF16)

    nblk = 2 * t // ATT_CB

    def scores(kt, cb):
        k = k_ref[pl.ds(pl.multiple_of(kt * t, t), t), :]
        s = jnp.dot(k, qq_s[:, cb * ATT_CB:(cb + 1) * ATT_CB], preferred_element_type=F32)
        s_s[cb] = s.astype(BF16)

    def step(kt, diagonal):
        vt = jnp.concatenate([vt_ref[kt], ones_rows], axis=0)
        for cb in range(nblk):
            cs = slice(cb * ATT_CB, (cb + 1) * ATT_CB)
            s = s_s[cb]
            if diagonal:
                key = lax.broadcasted_iota(jnp.int32, s.shape, 0)
                qry = lax.broadcasted_iota(jnp.int32, s.shape, 1) + (cb * ATT_CB) % t
                s = s + jnp.where((key // CHUNK) <= (qry // CHUNK), 0.0, MASKED).astype(BF16)
            m_prev = m_s[:, cs]
            m_new = jnp.maximum(m_prev, jnp.max(s, axis=0, keepdims=True).astype(F32))
            alpha = jnp.exp2(m_prev - m_new)
            p = jnp.exp2(s - m_new.astype(BF16))
            m_s[:, cs] = m_new
            ahead = cb + ATT_AHEAD
            if ahead < nblk:
                scores(kt, ahead)
            elif not diagonal:
                scores(kt + 1, ahead - nblk)
            acc_s[:, cs] = alpha * acc_s[:, cs] + jnp.dot(vt, p, preferred_element_type=F32)

    def group_body(g, carry):
        for u in range(ATT_UNROLL):
            step(ATT_UNROLL * g + u, False)
        return carry

    def single_body(kt, carry):
        step(kt, False)
        return carry

    for cb in range(ATT_AHEAD):
        scores(0, cb)
    n_grouped = i // ATT_UNROLL * ATT_UNROLL
    lax.fori_loop(0, i // ATT_UNROLL, group_body, 0)
    lax.fori_loop(n_grouped, i, single_body, 0)
    step(i, True)
    lam = _lambda_full(lamv_ref, lambda_init)
    a = acc_s[0:V_DIM, :] / acc_s[V_DIM:V_DIM + 1, :]
    d = a[:, 0:t] - lam * a[:, t:2 * t]
    inv = lax.rsqrt(jnp.mean(d * d, axis=0, keepdims=True) + NORM_EPS)
    y = d * inv * g_ref[...] * (1.0 - lambda_init)
    o_ref[...] = y.T.astype(BF16)


def _attn_prompt(qt_bf, k_bf, vt_bf, lamv, subln_g_col, lambda_init):
    nq = qt_bf.shape[0]
    t = nq * ATT_T
    kern = functools.partial(_attn_prompt_kernel, lambda_init=lambda_init)
    return pl.pallas_call(
        kern,
        grid=(N_HEADS, nq // ATT_QPS),
        in_specs=[
            pl.BlockSpec((ATT_QPS, V_DIM, ATT_T), lambda h, i: (i, h, 0)),
            pl.BlockSpec((t, V_DIM), lambda h, i: (0, h)),
            pl.BlockSpec((nq, V_DIM, ATT_T), lambda h, i: (0, h, 0)),
            pl.BlockSpec((4, HEAD_DIM), lambda h, i: (0, 0)),
            pl.BlockSpec((V_DIM, 1), lambda h, i: (0, 0)),
        ],
        out_specs=pl.BlockSpec((ATT_QPS * ATT_T, V_DIM), lambda h, i: (i, h)),
        out_shape=jax.ShapeDtypeStruct((t, ATT_WIDTH), BF16),
        scratch_shapes=[
            pltpu.VMEM((ATT_QPS, V_DIM, 2 * ATT_T), BF16),
            pltpu.VMEM((ATT_QPS, 1, 2 * ATT_T), F32),
            pltpu.VMEM((ATT_QPS, V_DIM + ATT_ONES, 2 * ATT_T), F32),
            pltpu.VMEM((ATT_QPS, 2 * ATT_T // ATT_CB, ATT_T, ATT_CB), BF16),
        ],
        compiler_params=_params(("arbitrary", "arbitrary"), 48),
        name="attn_prompt",
    )(qt_bf, k_bf, vt_bf, lamv, subln_g_col)


def _attn_sample_kernel(q_ref, kc_ref, vc_ref, kn_ref, vn_ref, lamv_ref, g_ref, o_ref,
                        *, lambda_init, past):
    rows = q_ref.shape[0]
    lam = _lambda_full(lamv_ref, lambda_init)
    contract_last = (((1,), (1,)), ((), ()))
    for h in range(N_HEADS):
        hs = slice(h * V_DIM, (h + 1) * V_DIM)
        q1, q2 = _split_heads(q_ref[:, hs])
        qq = jnp.concatenate([q1, q2], axis=0)
        s_c = jnp.dot(qq, kc_ref[0, hs, :].astype(BF16), preferred_element_type=F32)
        s_n = lax.dot_general(qq, kn_ref[:, hs], contract_last, preferred_element_type=F32)
        r = lax.broadcasted_iota(jnp.int32, s_n.shape, 0)
        c = lax.broadcasted_iota(jnp.int32, s_n.shape, 1)
        s_n = jnp.where(((past + c) // CHUNK) <= ((past + r % rows) // CHUNK), s_n, MASKED)
        m = jnp.maximum(jnp.max(s_c, axis=-1, keepdims=True), jnp.max(s_n, axis=-1, keepdims=True))
        p_c = jnp.exp2(s_c - m)
        p_n = jnp.exp2(s_n - m)
        l = jnp.sum(p_c, axis=-1, keepdims=True) + jnp.sum(p_n, axis=-1, keepdims=True)
        v_c = vc_ref[0, pl.ds(h, past, stride=N_HEADS), :].astype(BF16)
        acc = (jnp.dot(p_c.astype(BF16), v_c, preferred_element_type=F32)
               + jnp.dot(p_n.astype(BF16), vn_ref[:, hs], preferred_element_type=F32))
        o_ref[:, hs] = _diff_finish(acc, l, lam, g_ref[...], lambda_init, rows).astype(BF16)


def _attn_sample(q_bf, kt_cache, v_cache, k_bf, v_bf, lamv, subln_g, lambda_init):
    nb, _, past = kt_cache.shape
    rows = q_bf.shape[0] // nb
    kern = functools.partial(_attn_sample_kernel, lambda_init=lambda_init, past=past)
    new_spec = pl.BlockSpec((rows, ATT_WIDTH), lambda b: (b, 0))
    return pl.pallas_call(
        kern,
        grid=(nb,),
        in_specs=[
            new_spec,
            pl.BlockSpec((1, QK_WIDTH, past), lambda b: (b, 0, 0)),
            pl.BlockSpec((1, past * N_HEADS, V_DIM), lambda b: (b, 0, 0)),
            new_spec, new_spec,
            pl.BlockSpec((4, HEAD_DIM), lambda b: (0, 0)),
            pl.BlockSpec((1, V_DIM), lambda b: (0, 0)),
        ],
        out_specs=new_spec,
        out_shape=jax.ShapeDtypeStruct(q_bf.shape, BF16),
        compiler_params=_params(("arbitrary",), 48),
        name="attn_sample",
    )(q_bf, kt_cache, v_cache, k_bf, v_bf, lamv, subln_g)


def _outproj_kernel(att_p_ref, yc_p_ref, x_p_ref, att_s_ref, yc_s_ref, x_s_ref,
                    wt_ref, wb_ref, ng_ref, rw_ref, rb_ref, x1_ref, xf_ref, route_ref, count_ref, count_s,
                    *, n_prompt_tiles):
    i = pl.program_id(0)
    shared = (wt_ref, wb_ref, ng_ref, rw_ref, rb_ref, x1_ref, xf_ref, route_ref, count_ref, count_s)

    @pl.when(i == 0)
    def _():
        count_s[...] = jnp.zeros(count_s.shape, F32)

    @pl.when(i < n_prompt_tiles)
    def _():
        _outproj_tile(att_p_ref, yc_p_ref, x_p_ref, *shared)

    @pl.when(i >= n_prompt_tiles)
    def _():
        _outproj_tile(att_s_ref, yc_s_ref, x_s_ref, *shared)


def _outproj_tile(att_ref, yc_ref, x_ref, wt_ref, wb_ref, ng_ref, rw_ref, rb_ref,
                  x1_ref, xf_ref, route_ref, count_ref, count_s):
    o = (jnp.dot(att_ref[...], wt_ref[...], preferred_element_type=F32)
         + jnp.dot(yc_ref[...], wb_ref[...], preferred_element_type=F32))
    x1 = x_ref[...] + o
    x1_ref[...] = x1
    xf = x1 * lax.rsqrt(jnp.mean(x1 * x1, axis=-1, keepdims=True) + NORM_EPS) * ng_ref[...]
    for c in range(SLAB):
        lo = slice(c * LANES, (c + 1) * LANES)
        hi = slice(PACK_COLS + c * LANES, PACK_COLS + (c + 1) * LANES)
        xf_ref[c] = _pack_bf16_pair(xf[:, lo], xf[:, hi])
    logits =jnp.dot(xf.astype(BF16), rw_ref[...], preferred_element_type=F32) + rb_ref[...]

    lane = lax.broadcasted_iota(jnp.int32, logits.shape, 1)
    neg = -jnp.inf
    is_group = lane < N_GROUPS
    gl = jnp.where(is_group, logits, neg)
    gmax = jnp.max(gl, axis=-1, keepdims=True)
    grp = jnp.min(jnp.where(gl == gmax, lane, ROUTE_LANES), axis=-1, keepdims=True)
    gsum = jnp.sum(jnp.where(is_group, jnp.exp(gl - gmax), 0.0), axis=-1, keepdims=True)
    g_w = 1.0 / gsum
    e_lane = lane - N_GROUPS
    in_grp = (e_lane >= 0) & (e_lane < N_EXPERTS) & ((e_lane // EXPERTS_PER_GROUP) == grp)
    el = jnp.where(in_grp, logits, neg)
    t1 = jnp.max(el, axis=-1, keepdims=True)
    i1 = jnp.min(jnp.where(el == t1, lane, ROUTE_LANES), axis=-1, keepdims=True)
    el2 = jnp.where(lane == i1, neg, el)
    t2 = jnp.max(el2, axis=-1, keepdims=True)
    i2 = jnp.min(jnp.where(el2 == t2, lane, ROUTE_LANES), axis=-1, keepdims=True)
    r21 = jnp.exp(t2 - t1)
    w0 = g_w / (1.0 + r21)
    w1 = g_w * r21 / (1.0 + r21)
    e0 = (i1 - N_GROUPS).astype(F32)
    e1 = (i2 - N_GROUPS).astype(F32)
    rows = logits.shape[0]
    picked0 = lane == i1
    picked1 = lane == i2
    member = (picked0 | picked1).astype(BF16)
    earlier = (lax.broadcasted_iota(jnp.int32, (rows, rows), 1)
               < lax.broadcasted_iota(jnp.int32, (rows, rows), 0)).astype(BF16)
    before = jnp.dot(earlier, member, preferred_element_type=F32) + count_s[...]
    r0 = jnp.sum(jnp.where(picked0, before, 0.0), axis=-1, keepdims=True)
    r1 = jnp.sum(jnp.where(picked1, before, 0.0), axis=-1, keepdims=True)
    count_s[...] += jnp.sum(member.astype(F32), axis=0, keepdims=True)
    count_ref[...] = count_s[...]
    route_ref[...] = jnp.where(lane == 0, e0, jnp.where(lane == 1, e1,
                               jnp.where(lane == 2, w0, jnp.where(lane == 3, w1,
                               jnp.where(lane == 4, r0, jnp.where(lane == 5, r1, 0.0))))))


def _outproj(att_p, yc_p, x_p, att_s, yc_s, x_s, w_top, w_bot, norm_g, rw, rb):
    tm = OUT_TM
    n_p = x_p.shape[0] // tm
    n_s = x_s.shape[0] // tm
    t_all = x_p.shape[0] + x_s.shape[0]
    prow = lambda i: (jnp.minimum(i, n_p - 1), 0)
    srow = lambda i: (jnp.maximum(i - n_p, 0), 0)
    row = lambda i: (i, 0)
    const = lambda i: (0, 0)
    kern = functools.partial(_outproj_kernel, n_prompt_tiles=n_p)
    return pl.pallas_call(
        kern,
        grid=(n_p + n_s,),
        in_specs=[
            pl.BlockSpec((tm, ATT_WIDTH), prow),
            pl.BlockSpec((tm, CONV_CH), prow),
            pl.BlockSpec((tm, D_MODEL), prow),
            pl.BlockSpec((tm, ATT_WIDTH), srow),
            pl.BlockSpec((tm, CONV_CH), srow),
            pl.BlockSpec((tm, D_MODEL), srow),
            pl.BlockSpec((ATT_WIDTH, D_MODEL), const),
            pl.BlockSpec((CONV_CH, D_MODEL), const),
            pl.BlockSpec((1, D_MODEL), const),
            pl.BlockSpec((D_MODEL, ROUTE_LANES), const),
            pl.BlockSpec((1, ROUTE_LANES), const),
        ],
        out_specs=[
            pl.BlockSpec((tm, D_MODEL), row),
            pl.BlockSpec((SLAB, tm, LANES), lambda i: (0, i, 0)),
            pl.BlockSpec((tm, ROUTE_LANES), row),
            pl.BlockSpec((1, ROUTE_LANES), const),
        ],
        out_shape=[
            jax.ShapeDtypeStruct((t_all, D_MODEL), F32),
            jax.ShapeDtypeStruct((SLAB, t_all, LANES), U32),
            jax.ShapeDtypeStruct((t_all, ROUTE_LANES), F32),
            jax.ShapeDtypeStruct((1, ROUTE_LANES), F32),
        ],
        scratch_shapes=[pltpu.VMEM((1, ROUTE_LANES), F32)],
        compiler_params=_params(("arbitrary",), 52),
        name="outproj_router",
    )(att_p, yc_p, x_p, att_s, yc_s, x_s, w_top, w_bot, norm_g, rw, rb)


def _dispatch_copy(xf_ref, xs_hbm, sem, src_row, dst_row):
    return pltpu.make_async_copy(xf_ref.at[:, src_row, :],
                                 xs_hbm.at[pl.ds(dst_row * SLAB, SLAB)], sem)


def _zero_block_copy(zero_s, xs_hbm, sem, blk):
    span = MOE_BM * SLAB
    return pltpu.make_async_copy(zero_s, xs_hbm.at[pl.ds(blk * span, span)], sem)


def _dispatch_kernel(dest_ref, zero_blk_ref, xf_ref, xs_hbm, zero_s, sem):
    tm = DISPATCH_TM

    @pl.when(pl.program_id(0) == 0)
    def _():
        zero_s[...] = jnp.zeros(zero_s.shape, U32)

        def fill(j, start):
            blk = zero_blk_ref[0, 0, j]

            @pl.when(blk >= 0)
            def _():
                copy = _zero_block_copy(zero_s, xs_hbm, sem, blk)
                if start:
                    copy.start()
                else:
                    copy.wait()

        lax.fori_loop(0, zero_blk_ref.shape[-1], lambda j, c: (fill(j, True), c)[1], 0)
        lax.fori_loop(0, zero_blk_ref.shape[-1], lambda j, c: (fill(j, False), c)[1], 0)

    def issue(t, carry):
        for k in range(TOP_K):
            _dispatch_copy(xf_ref, xs_hbm, sem, t, dest_ref[0, 0, TOP_K * t + k]).start()
        return carry

    def drain(t, carry):
        for k in range(TOP_K):
            _dispatch_copy(xf_ref, xs_hbm, sem, t, dest_ref[0, 0, TOP_K * t + k]).wait()
        return carry

    lax.fori_loop(0, tm, issue, 0, unroll=ROW_DMA_UNROLL)
    lax.fori_loop(0, tm, drain, 0, unroll=ROW_DMA_UNROLL)


def _dispatch(dest, zero_blk, xf, cap):
    t_all = xf.shape[1]
    tm = DISPATCH_TM
    dest3 = dest.reshape(t_all // tm, 1, TOP_K * tm)
    zero_blk3 = zero_blk.reshape(1, 1, -1)
    return pl.pallas_call(
        _dispatch_kernel,
        grid=(t_all // tm,),
        in_specs=[
            pl.BlockSpec((1, 1, TOP_K * tm), lambda i: (i, 0, 0), memory_space=pltpu.SMEM),
            pl.BlockSpec(zero_blk3.shape, lambda i: (0, 0, 0), memory_space=pltpu.SMEM),
            pl.BlockSpec((SLAB, tm, LANES), lambda i: (0, i, 0)),
        ],
        out_specs=pl.BlockSpec(memory_space=pl.ANY),
        out_shape=jax.ShapeDtypeStruct((cap * SLAB, LANES), U32),
        scratch_shapes=[pltpu.VMEM((MOE_BM * SLAB, LANES), U32), pltpu.SemaphoreType.DMA(())],
        compiler_params=_params(("arbitrary",), 24),
        name="moe_dispatch",
    )(dest3, zero_blk3, xf)


def _expert_weight_copies(hbm_refs, stage_refs, sems, e):
    return [pltpu.make_async_copy(hbm.at[e], stage, sems.at[n])
            for n, (hbm, stage) in enumerate(zip(hbm_refs, stage_refs))]


def _experts_kernel(blk_e_ref, first_ref, next_e_ref, nused_ref, x_ref, wg_hbm, wu_hbm, wd_hbm, o_ref,
                    wg_f, wu_f, wd_f, wgu_s, wd_s, sems):
    i = pl.program_id(0)
    hbm_refs = (wg_hbm, wu_hbm, wd_hbm)
    stage_refs = (wg_f, wu_f, wd_f)

    @pl.when(i == 0)
    def _():
        for copy in _expert_weight_copies(hbm_refs, stage_refs, sems, blk_e_ref[0]):
            copy.start()

    @pl.when(i < nused_ref[0])
    def _():
        @pl.when(first_ref[i] == 1)
        def _():
            gate_copy, up_copy, down_copy = _expert_weight_copies(hbm_refs, stage_refs, sems, blk_e_ref[i])
            gate_copy.wait()
            wgu_s[:, 0:D_EXPERT] = wg_f[...].astype(BF16)
            up_copy.wait()
            wgu_s[:, D_EXPERT:2 * D_EXPERT] = wu_f[...].astype(BF16)
            down_copy.wait()
            wd_s[...] = wd_f[...].astype(BF16)

            @pl.when(next_e_ref[i] >= 0)
            def _():
                for copy in _expert_weight_copies(hbm_refs, stage_refs, sems, next_e_ref[i]):
                    copy.start()

        x_lo, x_hi = _unpack_bf16_pair(x_ref[...].reshape(MOE_BM, SLAB, LANES).reshape(MOE_BM, PACK_COLS))
        x = jnp.concatenate([x_lo, x_hi], axis=1).astype(BF16)
        gu = jnp.dot(x, wgu_s[...], preferred_element_type=F32)
        g = gu[:, 0:D_EXPERT]
        u = gu[:, D_EXPERT:2 * D_EXPERT]
        h = g / (1.0 + jnp.exp(-g)) * u
        out = jnp.dot(h.astype(BF16), wd_s[...], preferred_element_type=F32)
        packed = _pack_bf16_pair(out[:, 0:PACK_COLS], out[:, PACK_COLS:D_MODEL])
        o_ref[...] = packed.reshape(MOE_BM, SLAB, LANES).reshape(MOE_BM * SLAB, LANES)

    @pl.when(i >= nused_ref[0])
    def _():
        o_ref[...] = jnp.zeros(o_ref.shape, U32)


def _experts(blk_e, first, next_e, nused, xs, w_gate, w_up, w_down):
    cap = xs.shape[0] // SLAB
    bm = MOE_BM
    rows = lambda i, be, fi, ne, nu: (jnp.minimum(i, nu[0] - 1), 0)
    grid_spec = pltpu.PrefetchScalarGridSpec(
        num_scalar_prefetch=4,
        grid=(cap // bm,),
        in_specs=[
            pl.BlockSpec((bm * SLAB, LANES), rows),
            pl.BlockSpec(memory_space=pl.ANY),
            pl.BlockSpec(memory_space=pl.ANY),
            pl.BlockSpec(memory_space=pl.ANY),
        ],
        out_specs=pl.BlockSpec((bm * SLAB, LANES), lambda i, be, fi, ne, nu: (i, 0)),
        scratch_shapes=[
            pltpu.VMEM((D_MODEL, D_EXPERT), F32),
            pltpu.VMEM((D_MODEL, D_EXPERT), F32),
            pltpu.VMEM((D_EXPERT, D_MODEL), F32),
            pltpu.VMEM((D_MODEL, 2 * D_EXPERT), BF16),
            pltpu.VMEM((D_EXPERT, D_MODEL), BF16),
            pltpu.SemaphoreType.DMA((3,)),
        ],
    )
    return pl.pallas_call(
        _experts_kernel,
        grid_spec=grid_spec,
        out_shape=jax.ShapeDtypeStruct((cap * SLAB, LANES), U32),
        compiler_params=_params(("arbitrary",), 48),
        name="moe_experts",
    )(blk_e, first, next_e, nused, xs, w_gate, w_up, w_down)


def _combine_copy(outs_hbm, g_s, sems, slot, src_row, k, t):
    return pltpu.make_async_copy(outs_hbm.at[pl.ds(src_row * SLAB, SLAB)],
                                 g_s.at[slot, k, :, t, :], sems.at[slot])


def _combine_kernel(dest_ref, dest_next_ref, x1_ref, route_ref, outs_hbm, y_ref, g_s, sems, *, n_tiles):
    i = pl.program_id(0)
    tm = x1_ref.shape[0]
    slot = i % 2

    def gather(table_ref, into, start):
        def body(t, carry):
            for k in range(TOP_K):
                copy = _combine_copy(outs_hbm, g_s, sems, into, table_ref[0, 0, TOP_K * t + k], k, t)
                if start:
                    copy.start()
                else:
                    copy.wait()
            return carry
        lax.fori_loop(0, tm, body, 0, unroll=ROW_DMA_UNROLL)

    @pl.when(i == 0)
    def _():
        gather(dest_ref, 0, True)

    @pl.when(i + 1 < n_tiles)
    def _():
        gather(dest_next_ref, 1 - slot, True)

    gather(dest_ref, slot, False)
    r = route_ref[...]
    w0 = jnp.broadcast_to(r[:, 2:3], (tm, LANES))
    w1 = jnp.broadcast_to(r[:, 3:4], (tm, LANES))
    for c in range(SLAB):
        lo = slice(c * LANES, (c + 1) * LANES)
        hi = slice(PACK_COLS + c * LANES, PACK_COLS + (c + 1) * LANES)
        lo0, hi0 = _unpack_bf16_pair(g_s[slot, 0, c])
        lo1, hi1 = _unpack_bf16_pair(g_s[slot, 1, c])
        y_ref[:, lo] = x1_ref[:, lo] + w0 * lo0 + w1 * lo1
        y_ref[:, hi] = x1_ref[:, hi] + w0 * hi0 + w1 * hi1


def _combine(dest, x1, route, outs, row_off, t):
    tm = OUT_TM
    t_all = x1.shape[0]
    off = row_off // tm
    dest3 = dest.reshape(t_all // tm, 1, TOP_K * tm)
    n = t // tm
    return pl.pallas_call(
        functools.partial(_combine_kernel, n_tiles=n),
        grid=(n,),
        in_specs=[
            pl.BlockSpec((1, 1, TOP_K * tm), lambda i: (i + off, 0, 0), memory_space=pltpu.SMEM),
            pl.BlockSpec((1, 1, TOP_K * tm), lambda i: (jnp.minimum(i + 1, n - 1) + off, 0, 0),
                         memory_space=pltpu.SMEM),
            pl.BlockSpec((tm, D_MODEL), lambda i: (i + off, 0)),
            pl.BlockSpec((tm, ROUTE_LANES), lambda i: (i + off, 0)),
            pl.BlockSpec(memory_space=pl.ANY),
        ],
        out_specs=pl.BlockSpec((tm, D_MODEL), lambda i: (i, 0)),
        out_shape=jax.ShapeDtypeStruct((t, D_MODEL), F32),
        scratch_shapes=[pltpu.VMEM((2, TOP_K, SLAB, tm, LANES), U32), pltpu.SemaphoreType.DMA((2,))],
        compiler_params=_params(("arbitrary",), 32),
        name="moe_combine",
    )(dest3, dest3, x1, route, outs)


def _routing_tables(route, lane_counts, n_blocks):
    bm = MOE_BM
    flat_e = route[:, 0:TOP_K].astype(jnp.int32).reshape(-1)
    rank = route[:, 2 * TOP_K:3 * TOP_K].astype(jnp.int32).reshape(-1)
    counts = lane_counts[0, N_GROUPS:N_GROUPS + N_EXPERTS].astype(jnp.int32)
    padded = (counts + bm - 1) // bm * bm
    pad_end = jnp.cumsum(padded)
    pad_start = pad_end - padded
    experts = jnp.arange(N_EXPERTS, dtype=jnp.int32)
    onehot = (flat_e[:, None] == experts[None, :]).astype(BF16)
    start_blk = jnp.dot(onehot, (pad_start // bm).astype(BF16), preferred_element_type=F32)
    dest = start_blk.astype(jnp.int32) * bm + rank
    nused = (pad_end[-1] // bm).astype(jnp.int32)
    blk = jnp.arange(n_blocks, dtype=jnp.int32)
    blk_row = jnp.minimum(blk, nused - 1) * bm
    blk_e = jnp.sum((pad_end[None, :] <= blk_row[:, None]).astype(jnp.int32), axis=1)
    blk_e = jnp.minimum(blk_e, N_EXPERTS - 1)
    first = jnp.concatenate([jnp.ones((1,), jnp.int32),
                             (blk_e[1:] != blk_e[:-1]).astype(jnp.int32)])
    later_owner = jnp.where((counts[None, :] > 0) & (experts[None, :] > experts[:, None]),
                            experts[None, :], N_EXPERTS)
    next_owner = jnp.min(later_owner, axis=1)
    next_owner = jnp.where(next_owner == N_EXPERTS, -1, next_owner).astype(jnp.int32)
    last_blk = jnp.where(counts > 0, pad_end // bm - 1, -1).astype(jnp.int32)
    tail = nused + jnp.arange(N_EXPERTS, dtype=jnp.int32)
    tail_blk = jnp.where(tail < n_blocks, tail, -1)
    zero_blk = jnp.concatenate([last_blk, tail_blk])
    return dest, zero_blk, blk_e, first, next_owner[blk_e], nused.reshape(1)


def _rope_tables(pos):
    half = HEAD_DIM // 2
    inv = ROPE_THETA ** (-jnp.arange(half, dtype=F32) / half)
    ang = pos.astype(F32)[:, None] * inv[None, :]
    cos = jnp.cos(ang)
    sin = jnp.sin(ang)
    cos_h = jnp.concatenate([cos, cos], axis=-1)
    sin_h = jnp.concatenate([-sin, sin], axis=-1)
    reps = LANES // HEAD_DIM
    return jnp.tile(cos_h, (1, reps)), jnp.tile(sin_h, (1, reps))


def kernel(x_prompt, x_sample, cache_k, cache_v, state_conv, norm_mix_g, w_in, q_norm_g, k_norm_g, lambda_q1, lambda_k1, lambda_q2, lambda_k2, subln_g, conv_w, conv_norm_g, w_out, norm_ffn_g, router_group_w, router_group_b, router_expert_w, router_expert_b, expert_w_gate, expert_w_up, expert_w_down):
    assert w_in.shape[0] == 1, "single-layer step"
    bp, sp, _ = x_prompt.shape
    bs, ss, _ = x_sample.shape
    past = cache_k.shape[2]
    assert bp == 1 and sp % ATT_T == 0
    tp = bp * sp
    ts = bs * ss
    t_all = tp + ts
    lambda_init = 0.8 - 0.6 * math.exp(-0.3 * 0)

    w_in_bf = w_in[0].astype(BF16)
    w_out_bf = w_out[0].astype(BF16)
    w_top, w_bot = w_out_bf[:ATT_WIDTH], w_out_bf[ATT_WIDTH:]
    ng = norm_mix_g[0].reshape(1, D_MODEL)
    qg = jnp.tile(q_norm_g[0], QK_WIDTH // HEAD_DIM).reshape(1, QK_WIDTH)
    kg = jnp.tile(k_norm_g[0], QK_WIDTH // HEAD_DIM).reshape(1, QK_WIDTH)
    head_of = jnp.arange(IN_CH, dtype=jnp.int32) // HEAD_DIM
    gmat = jnp.where(head_of[:, None] == head_of[None, :], 1.0 / HEAD_DIM, 0.0).astype(BF16)
    lamv = jnp.stack([lambda_q1[0], lambda_k1[0], lambda_q2[0], lambda_k2[0]]).astype(F32)
    sg = subln_g[0].reshape(1, V_DIM)
    cw = conv_w[0]
    cng = conv_norm_g[0].reshape(1, CONV_CH)
    cos_p, sin_p = _rope_tables(jnp.arange(sp, dtype=jnp.int32))
    cos_s, sin_s = _rope_tables(jnp.tile(past + jnp.arange(ss, dtype=jnp.int32), bs))

    zero_conv = jnp.zeros((1, CONV_K - 1, CONV_CH), F32)
    qt_p, kf_p, kb_p, vf_p, vt_p, yc_p, tail_p = _inproj(
        x_prompt.reshape(tp, D_MODEL), ng, w_in_bf, qg, kg, gmat, cos_p, sin_p, zero_conv, cw, cng,
        tm=ATT_T, nseq=1, carry=True, qv_transposed=True)
    att_p = _attn_prompt(qt_p, kb_p, vt_p, lamv, subln_g[0].reshape(V_DIM, 1), lambda_init)

    seqs_per_tile = ATT_T // ss
    q_s, kf_s, kb_s, vf_s, vb_s, yc_s, tail_s = _inproj(
        x_sample.reshape(ts, D_MODEL), ng, w_in_bf, qg, kg, gmat, cos_s, sin_s, state_conv[0], cw, cng,
        tm=ATT_T, nseq=seqs_per_tile, carry=False, qv_transposed=False)
    kt_cache = jnp.transpose(cache_k[0], (0, 2, 3, 4, 1)).reshape(bs, QK_WIDTH, past)
    att_s = _attn_sample(q_s, kt_cache, cache_v[0].reshape(bs, past * N_HEADS, V_DIM),
                         kb_s, vb_s, lamv, sg, lambda_init)

    rw = jnp.zeros((D_MODEL, ROUTE_LANES), F32)
    rw = rw.at[:, 0:N_GROUPS].set(router_group_w[0]).at[:, N_GROUPS:N_GROUPS + N_EXPERTS].set(router_expert_w[0])
    rb = jnp.zeros((1, ROUTE_LANES), F32)
    rb = rb.at[0, 0:N_GROUPS].set(router_group_b[0]).at[0, N_GROUPS:N_GROUPS + N_EXPERTS].set(router_expert_b[0])
    nf = norm_ffn_g[0].reshape(1, D_MODEL)
    rw_bf = rw.astype(BF16)
    x1, xf, route, lane_counts = _outproj(att_p, yc_p, x_prompt.reshape(tp, D_MODEL), att_s, yc_s,
                                          x_sample.reshape(ts, D_MODEL), w_top, w_bot, nf, rw_bf, rb)

    n = t_all * TOP_K
    n_blocks = n // MOE_BM + N_EXPERTS
    dest, zero_blk, blk_e, first, next_e, nused = _routing_tables(route, lane_counts, n_blocks)
    xs = _dispatch(dest, zero_blk, xf, n_blocks * MOE_BM)
    outs = _experts(blk_e, first, next_e, nused, xs, expert_w_gate[0], expert_w_up[0], expert_w_down[0])
    y_p = _combine(dest, x1, route, outs, 0, tp)
    y_s = _combine(dest, x1, route, outs, tp, ts)

    return (y_p.reshape(bp, sp, D_MODEL),
            y_s.reshape(bs, ss, D_MODEL),
            kf_p.reshape(1, bp, sp, N_HEADS, 2, HEAD_DIM),
            vf_p.reshape(1, bp, sp, N_HEADS, V_DIM),
            tail_p.reshape(1, bp, CONV_K - 1, CONV_CH),
            kf_s.reshape(1, bs, ss, N_HEADS, 2, HEAD_DIM),
            vf_s.reshape(1, bs, ss, N_HEADS, V_DIM),
            tail_s.reshape(1, bs, CONV_K - 1, CONV_CH))
```

```python
import functools
import math

import jax
import jax.numpy as jnp
from jax import lax
from jax.experimental import pallas as pl
from jax.experimental.pallas import tpu as pltpu

F32 = jnp.float32
BF16 = jnp.bfloat16

D_MODEL = 2048
CHUNK = 64
HEAD_DIM = 64
V_DIM = 2 * HEAD_DIM
N_HEADS = 8
QK_WIDTH = N_HEADS * 2 * HEAD_DIM
ATT_WIDTH = N_HEADS * V_DIM
CONV_CH = 1024
CONV_K = 3
ROPE_THETA = 10000.0
N_GROUPS = 8
EXPERTS_PER_GROUP = 8
N_EXPERTS = N_GROUPS * EXPERTS_PER_GROUP
TOP_K = 2
D_EXPERT = D_MODEL // 4
NORM_EPS = 1e-6
SECTION = 1024
N_SECTIONS = 6

LANES = 128
IN_SPS = 2
IN_CH = 256
ATT_AHEAD = 2
ATT_ONES = 16
ATT_QPS = 4
ATT_UNROLL = 4
ATT_CB = 512
ATT_T = 512
OUT_TM = 256
MOE_BM = 256
DISPATCH_TM = 512
ROW_DMA_UNROLL = 8
PACK_COLS = D_MODEL // 2
SLAB = PACK_COLS // LANES
U32 = jnp.uint32
ROUTE_LANES = LANES
MASKED = -1e30
Q_SCALE = HEAD_DIM ** -0.5 * math.log2(math.e)
MIB = 1024 * 1024


def _pack_bf16_pair(lo, hi):
    lo_bits = lax.bitcast_convert_type(lo.astype(BF16).astype(F32), U32)
    hi_bits = lax.bitcast_convert_type(hi.astype(BF16).astype(F32), U32)
    return hi_bits | (lo_bits >> 16)


def _unpack_bf16_pair(words):
    lo = lax.bitcast_convert_type(words << 16, F32)
    hi = lax.bitcast_convert_type(words & U32(0xFFFF0000), F32)
    return lo, hi


def _params(sem, vmem_mib):
    return pltpu.CompilerParams(dimension_semantics=sem, vmem_limit_bytes=vmem_mib * MIB)


def _inproj_kernel(x_ref, ng_ref, w_ref, qg_ref, kg_ref, gmat_ref, cos_ref, sin_ref,
                   prev_ref, cw_ref, cng_ref,
                   q_ref, kf_ref, kb_ref, vf_ref, vb_ref, yc_ref, tail_ref,
                   xn_s, gb_s, c_s, z_s, u_s, *, nseq, carry, qv_transposed):
    i = pl.program_id(0)
    j = pl.program_id(1)
    tm = x_ref.shape[0]
    seq = tm // nseq

    @pl.when(j == 0)
    def _():
        x = x_ref[...]
        inv = lax.rsqrt(jnp.mean(x * x, axis=-1, keepdims=True) + NORM_EPS)
        xn_s[...] = (x * inv * ng_ref[...]).astype(BF16)
        if carry:
            @pl.when(i == 0)
            def _():
                u_s[6:8, :] = prev_ref[0]

    def chunk_dot(col):
        return jnp.dot(xn_s[...], w_ref[:, col:col + IN_CH], preferred_element_type=F32)

    def pipelined(stages):
        acc = chunk_dot(stages[0][0])
        for n, (_, epilogue, cols) in enumerate(stages):
            nxt = chunk_dot(stages[n + 1][0]) if n + 1 < len(stages) else None
            epilogue(acc, cols)
            acc = nxt

    def head_norm_rope(a, g):
        ms = jnp.dot((a * a).astype(BF16), gmat_ref[...], preferred_element_type=F32)
        y = a * lax.rsqrt(ms + NORM_EPS) * g
        reps = IN_CH // LANES
        cos = jnp.concatenate([cos_ref[...]] * reps, axis=1)
        sin = jnp.concatenate([sin_ref[...]] * reps, axis=1)
        lane = lax.broadcasted_iota(jnp.int32, y.shape, 1)
        first = (lane & (HEAD_DIM - 1)) < HEAD_DIM // 2
        partner = jnp.where(first, pltpu.roll(y, IN_CH - HEAD_DIM // 2, 1),
                            pltpu.roll(y, HEAD_DIM // 2, 1))
        return y * cos + partner * sin

    epilogues = []
    for section in range(N_SECTIONS):

        def epilogue(acc, cols, section=section):
            if section == 0:
                q = head_norm_rope(acc, qg_ref[:, cols]) * Q_SCALE
                if qv_transposed:
                    q_ref[0, cols, :] = q.T.astype(BF16)
                else:
                    q_ref[:, cols] = q.astype(BF16)
            elif section == 1:
                k = head_norm_rope(acc, kg_ref[:, cols])
                kf_ref[:, cols] = k
                kb_ref[:, cols] = k.astype(BF16)
            elif section == 2:
                vf_ref[:, cols] = acc
                if qv_transposed:
                    vb_ref[0, cols, :] = acc.T.astype(BF16)
                else:
                    vb_ref[:, cols] = acc.astype(BF16)
            elif section == 3:
                gb_s[:, cols] = acc
            elif section == 4:
                c_s[:, cols] = acc
            else:
                u = c_s[:, cols] * acc
                w0 = cw_ref[0:1, cols]
                w1 = cw_ref[1:2, cols]
                w2 = cw_ref[2:3, cols]
                for s in range(nseq):
                    rows = slice(s * seq, (s + 1) * seq)
                    if not carry:
                        u_s[6:8, cols] = prev_ref[s, :, cols]
                    u_s[8:8 + seq, cols] = u[rows]
                    conv = (w0 * u_s[6:6 + seq, cols] + w1 * u_s[7:7 + seq, cols]
                            + w2 * u_s[8:8 + seq, cols])
                    z_s[rows, cols] = gb_s[rows, cols] * conv
                    tail = u_s[seq + 6:seq + 8, cols]
                    tail_ref[s, :, cols] = tail
                    if carry:
                        u_s[6:8, cols] = tail

        epilogues.append(epilogue)

    n_steps = N_SECTIONS // IN_SPS
    for step in range(n_steps):
        stages = [(sub * SECTION + c * IN_CH, epilogues[step * IN_SPS + sub],
                   slice(c * IN_CH, (c + 1) * IN_CH))
                  for sub in range(IN_SPS) for c in range(SECTION // IN_CH)]

        @pl.when(j == step)
        def _(step=step, stages=stages):
            pipelined(stages)
            if step == n_steps - 1:
                z = z_s[...]
                inv = lax.rsqrt(jnp.mean(z * z, axis=-1, keepdims=True) + NORM_EPS)
                yc_ref[...] = (z * inv * cng_ref[...]).astype(BF16)


def _inproj(x2d, norm_g, w_in_bf, qg, kg, gmat, cos, sin, conv_prev, conv_w, conv_norm_g,
            *, tm, nseq, carry, qv_transposed):
    t = x2d.shape[0]
    ni = t // tm
    seq = tm // nseq
    row = lambda i, j: (i, 0)
    const = lambda i, j: (0, 0)
    if qv_transposed:
        qv_shape = jax.ShapeDtypeStruct((ni, QK_WIDTH, tm), BF16)
        qv_spec = pl.BlockSpec((1, QK_WIDTH, tm), lambda i, j: (i, 0, 0))
    else:
        qv_shape = jax.ShapeDtypeStruct((t, QK_WIDTH), BF16)
        qv_spec = pl.BlockSpec((tm, QK_WIDTH), row)
    if carry:
        prev_spec = pl.BlockSpec((1, CONV_K - 1, CONV_CH), lambda i, j: (0, 0, 0))
        tail_shape = jax.ShapeDtypeStruct((1, CONV_K - 1, CONV_CH), F32)
        tail_spec = pl.BlockSpec((1, CONV_K - 1, CONV_CH), lambda i, j: (0, 0, 0))
    else:
        prev_spec = pl.BlockSpec((nseq, CONV_K - 1, CONV_CH), lambda i, j: (i, 0, 0))
        tail_shape = jax.ShapeDtypeStruct((ni * nseq, CONV_K - 1, CONV_CH), F32)
        tail_spec = pl.BlockSpec((nseq, CONV_K - 1, CONV_CH), lambda i, j: (i, 0, 0))
    kern = functools.partial(_inproj_kernel, nseq=nseq, carry=carry, qv_transposed=qv_transposed)
    return pl.pallas_call(
        kern,
        grid=(ni, N_SECTIONS // IN_SPS),
        in_specs=[
            pl.BlockSpec((tm, D_MODEL), row),
            pl.BlockSpec((1, D_MODEL), const),
            pl.BlockSpec((D_MODEL, IN_SPS * SECTION), lambda i, j: (0, j)),
            pl.BlockSpec((1, SECTION), const),
            pl.BlockSpec((1, SECTION), const),
            pl.BlockSpec((IN_CH, IN_CH), const),
            pl.BlockSpec((tm, LANES), row),
            pl.BlockSpec((tm, LANES), row),
            prev_spec,
            pl.BlockSpec((CONV_K, CONV_CH), const),
            pl.BlockSpec((1, CONV_CH), const),
        ],
        out_specs=[
            qv_spec,
            pl.BlockSpec((tm, QK_WIDTH), row),
            pl.BlockSpec((tm, QK_WIDTH), row),
            pl.BlockSpec((tm, ATT_WIDTH), row),
            qv_spec,
            pl.BlockSpec((tm, CONV_CH), row),
            tail_spec,
        ],
        out_shape=[
            qv_shape,
            jax.ShapeDtypeStruct((t, QK_WIDTH), F32),
            jax.ShapeDtypeStruct((t, QK_WIDTH), BF16),
            jax.ShapeDtypeStruct((t, ATT_WIDTH), F32),
            qv_shape,
            jax.ShapeDtypeStruct((t, CONV_CH), BF16),
            tail_shape,
        ],
        scratch_shapes=[
            pltpu.VMEM((tm, D_MODEL), BF16),
            pltpu.VMEM((tm, CONV_CH), F32),
            pltpu.VMEM((tm, CONV_CH), F32),
            pltpu.VMEM((tm, CONV_CH), F32),
            pltpu.VMEM((seq + 8, CONV_CH), F32),
        ],
        compiler_params=_params(("arbitrary", "arbitrary"), 58),
        name="inproj_carry" if carry else "inproj_seqs",
    )(x2d, norm_g, w_in_bf, qg, kg, gmat, cos, sin, conv_prev, conv_w, conv_norm_g)


def _lambda_full(lamv_ref, lambda_init):
    lv = lamv_ref[...]
    a = jnp.sum(lv[0:1] * lv[1:2], axis=-1, keepdims=True)
    b = jnp.sum(lv[2:3] * lv[3:4], axis=-1, keepdims=True)
    return jnp.exp(a) - jnp.exp(b) + lambda_init


def _split_heads(q):
    lane = lax.broadcasted_iota(jnp.int32, q.shape, 1)
    zero = jnp.zeros_like(q)
    return jnp.where(lane < HEAD_DIM, q, zero), jnp.where(lane >= HEAD_DIM, q, zero)


def _diff_finish(acc, l, lam, g, lambda_init, rows):
    a = acc / l
    d = a[0:rows] - lam * a[rows:2 * rows]
    inv = lax.rsqrt(jnp.mean(d * d, axis=-1, keepdims=True) + NORM_EPS)
    return d * inv * g * (1.0 - lambda_init)


def _attn_prompt_kernel(qt_ref, k_ref, vt_ref, lamv_ref, g_ref, o_ref, qq_s, m_s, acc_s, s_s,
                        *, lambda_init):
    for sub in range(ATT_QPS):
        _attn_prompt_tile(ATT_QPS * pl.program_id(1) + sub, qt_ref.at[sub], k_ref, vt_ref, lamv_ref, g_ref,
                          o_ref.at[pl.ds(sub * ATT_T, ATT_T)], qq_s.at[sub], m_s.at[sub], acc_s.at[sub],
                          s_s.at[sub], lambda_init=lambda_init)


def _attn_prompt_tile(i, qt_ref, k_ref, vt_ref, lamv_ref, g_ref, o_ref, qq_s, m_s, acc_s, s_s,
                      *, lambda_init):
    t = ATT_T
    qt = qt_ref[...]
    feat = lax.broadcasted_iota(jnp.int32, qt.shape, 0)
    zero = jnp.zeros_like(qt)
    qq_s[:, 0:t] = jnp.where(feat < HEAD_DIM, qt, zero)
    qq_s[:, t:2 * t] = jnp.where(feat >= HEAD_DIM, qt, zero)
    m_s[...] = jnp.full(m_s.shape, MASKED, F32)
    acc_s[...] = jnp.zeros(acc_s.shape, F32)
    ones_rows = (lax.broadcasted_iota(jnp.int32, (ATT_ONES, t), 0) == 0).astype(BF16)

    nblk = 2 * t // ATT_CB

    def scores(kt, cb):
        k = k_ref[pl.ds(pl.multiple_of(kt * t, t), t), :]
        s = jnp.dot(k, qq_s[:, cb * ATT_CB:(cb + 1) * ATT_CB], preferred_element_type=F32)
        s_s[cb] = s.astype(BF16)

    def step(kt, diagonal):
        vt = jnp.concatenate([vt_ref[kt], ones_rows], axis=0)
        for cb in range(nblk):
            cs = slice(cb * ATT_CB, (cb + 1) * ATT_CB)
            s = s_s[cb]
            if diagonal:
                key = lax.broadcasted_iota(jnp.int32, s.shape, 0)
                qry = lax.broadcasted_iota(jnp.int32, s.shape, 1) + (cb * ATT_CB) % t
                s = s + jnp.where((key // CHUNK) <= (qry // CHUNK), 0.0, MASKED).astype(BF16)
            m_prev = m_s[:, cs]
            m_new = jnp.maximum(m_prev, jnp.max(s, axis=0, keepdims=True).astype(F32))
            alpha = jnp.exp2(m_prev - m_new)
            p = jnp.exp2(s - m_new.astype(BF16))
            m_s[:, cs] = m_new
            ahead = cb + ATT_AHEAD
            if ahead < nblk:
                scores(kt, ahead)
            elif not diagonal:
                scores(kt + 1, ahead - nblk)
            acc_s[:, cs] = alpha * acc_s[:, cs] + jnp.dot(vt, p, preferred_element_type=F32)

    def group_body(g, carry):
        for u in range(ATT_UNROLL):
            step(ATT_UNROLL * g + u, False)
        return carry

    def single_body(kt, carry):
        step(kt, False)
        return carry

    for cb in range(ATT_AHEAD):
        scores(0, cb)
    n_grouped = i // ATT_UNROLL * ATT_UNROLL
    lax.fori_loop(0, i // ATT_UNROLL, group_body, 0)
    lax.fori_loop(n_grouped, i, single_body, 0)
    step(i, True)
    lam = _lambda_full(lamv_ref, lambda_init)
    a = acc_s[0:V_DIM, :] / acc_s[V_DIM:V_DIM + 1, :]
    d = a[:, 0:t] - lam * a[:, t:2 * t]
    inv = lax.rsqrt(jnp.mean(d * d, axis=0, keepdims=True) + NORM_EPS)
    y = d * inv * g_ref[...] * (1.0 - lambda_init)
    o_ref[...] = y.T.astype(BF16)


def _attn_prompt(qt_bf, k_bf, vt_bf, lamv, subln_g_col, lambda_init):
    nq = qt_bf.shape[0]
    t = nq * ATT_T
    kern = functools.partial(_attn_prompt_kernel, lambda_init=lambda_init)
    return pl.pallas_call(
        kern,
        grid=(N_HEADS, nq // ATT_QPS),
        in_specs=[
            pl.BlockSpec((ATT_QPS, V_DIM, ATT_T), lambda h, i: (i, h, 0)),
            pl.BlockSpec((t, V_DIM), lambda h, i: (0, h)),
            pl.BlockSpec((nq, V_DIM, ATT_T), lambda h, i: (0, h, 0)),
            pl.BlockSpec((4, HEAD_DIM), lambda h, i: (0, 0)),
            pl.BlockSpec((V_DIM, 1), lambda h, i: (0, 0)),
        ],
        out_specs=pl.BlockSpec((ATT_QPS * ATT_T, V_DIM), lambda h, i: (i, h)),
        out_shape=jax.ShapeDtypeStruct((t, ATT_WIDTH), BF16),
        scratch_shapes=[
            pltpu.VMEM((ATT_QPS, V_DIM, 2 * ATT_T), BF16),
            pltpu.VMEM((ATT_QPS, 1, 2 * ATT_T), F32),
            pltpu.VMEM((ATT_QPS, V_DIM + ATT_ONES, 2 * ATT_T), F32),
            pltpu.VMEM((ATT_QPS, 2 * ATT_T // ATT_CB, ATT_T, ATT_CB), BF16),
        ],
        compiler_params=_params(("arbitrary", "arbitrary"), 48),
        name="attn_prompt",
    )(qt_bf, k_bf, vt_bf, lamv, subln_g_col)


def _attn_sample_kernel(q_ref, kc_ref, vc_ref, kn_ref, vn_ref, lamv_ref, g_ref, o_ref,
                        *, lambda_init, past):
    rows = q_ref.shape[0]
    lam = _lambda_full(lamv_ref, lambda_init)
    contract_last = (((1,), (1,)), ((), ()))
    for h in range(N_HEADS):
        hs = slice(h * V_DIM, (h + 1) * V_DIM)
        q1, q2 = _split_heads(q_ref[:, hs])
        qq = jnp.concatenate([q1, q2], axis=0)
        s_c = jnp.dot(qq, kc_ref[0, hs, :].astype(BF16), preferred_element_type=F32)
        s_n = lax.dot_general(qq, kn_ref[:, hs], contract_last, preferred_element_type=F32)
        r = lax.broadcasted_iota(jnp.int32, s_n.shape, 0)
        c = lax.broadcasted_iota(jnp.int32, s_n.shape, 1)
        s_n = jnp.where(((past + c) // CHUNK) <= ((past + r % rows) // CHUNK), s_n, MASKED)
        m = jnp.maximum(jnp.max(s_c, axis=-1, keepdims=True), jnp.max(s_n, axis=-1, keepdims=True))
        p_c = jnp.exp2(s_c - m)
        p_n = jnp.exp2(s_n - m)
        l = jnp.sum(p_c, axis=-1, keepdims=True) + jnp.sum(p_n, axis=-1, keepdims=True)
        v_c = vc_ref[0, pl.ds(h, past, stride=N_HEADS), :].astype(BF16)
        acc = (jnp.dot(p_c.astype(BF16), v_c, preferred_element_type=F32)
               + jnp.dot(p_n.astype(BF16), vn_ref[:, hs], preferred_element_type=F32))
        o_ref[:, hs] = _diff_finish(acc, l, lam, g_ref[...], lambda_init, rows).astype(BF16)


def _attn_sample(q_bf, kt_cache, v_cache, k_bf, v_bf, lamv, subln_g, lambda_init):
    nb, _, past = kt_cache.shape
    rows = q_bf.shape[0] // nb
    kern = functools.partial(_attn_sample_kernel, lambda_init=lambda_init, past=past)
    new_spec = pl.BlockSpec((rows, ATT_WIDTH), lambda b: (b, 0))
    return pl.pallas_call(
        kern,
        grid=(nb,),
        in_specs=[
            new_spec,
            pl.BlockSpec((1, QK_WIDTH, past), lambda b: (b, 0, 0)),
            pl.BlockSpec((1, past * N_HEADS, V_DIM), lambda b: (b, 0, 0)),
            new_spec, new_spec,
            pl.BlockSpec((4, HEAD_DIM), lambda b: (0, 0)),
            pl.BlockSpec((1, V_DIM), lambda b: (0, 0)),
        ],
        out_specs=new_spec,
        out_shape=jax.ShapeDtypeStruct(q_bf.shape, BF16),
        compiler_params=_params(("arbitrary",), 48),
        name="attn_sample",
    )(q_bf, kt_cache, v_cache, k_bf, v_bf, lamv, subln_g)


def _outproj_kernel(att_p_ref, yc_p_ref, x_p_ref, att_s_ref, yc_s_ref, x_s_ref,
                    wt_ref, wb_ref, ng_ref, rw_ref, rb_ref, x1_ref, xf_ref, route_ref, count_ref, count_s,
                    *, n_prompt_tiles):
    i = pl.program_id(0)
    shared = (wt_ref, wb_ref, ng_ref, rw_ref, rb_ref, x1_ref, xf_ref, route_ref, count_ref, count_s)

    @pl.when(i == 0)
    def _():
        count_s[...] = jnp.zeros(count_s.shape, F32)

    @pl.when(i < n_prompt_tiles)
    def _():
        _outproj_tile(att_p_ref, yc_p_ref, x_p_ref, *shared)

    @pl.when(i >= n_prompt_tiles)
    def _():
        _outproj_tile(att_s_ref, yc_s_ref, x_s_ref, *shared)


def _outproj_tile(att_ref, yc_ref, x_ref, wt_ref, wb_ref, ng_ref, rw_ref, rb_ref,
                  x1_ref, xf_ref, route_ref, count_ref, count_s):
    o = (jnp.dot(att_ref[...], wt_ref[...], preferred_element_type=F32)
         + jnp.dot(yc_ref[...], wb_ref[...], preferred_element_type=F32))
    x1 = x_ref[...] + o
    x1_ref[...] = x1
    xf = x1 * lax.rsqrt(jnp.mean(x1 * x1, axis=-1, keepdims=True) + NORM_EPS) * ng_ref[...]
    for c in range(SLAB):
        lo = slice(c * LANES, (c + 1) * LANES)
        hi = slice(PACK_COLS + c * LANES, PACK_COLS + (c + 1) * LANES)
        xf_ref[c] = _pack_bf16_pair(xf[:, lo], xf[:, hi])
    logits =jnp.dot(xf.astype(BF16), rw_ref[...], preferred_element_type=F32) + rb_ref[...]

    lane = lax.broadcasted_iota(jnp.int32, logits.shape, 1)
    neg = -jnp.inf
    is_group = lane < N_GROUPS
    gl = jnp.where(is_group, logits, neg)
    gmax = jnp.max(gl, axis=-1, keepdims=True)
    grp = jnp.min(jnp.where(gl == gmax, lane, ROUTE_LANES), axis=-1, keepdims=True)
    gsum = jnp.sum(jnp.where(is_group, jnp.exp(gl - gmax), 0.0), axis=-1, keepdims=True)
    g_w = 1.0 / gsum
    e_lane = lane - N_GROUPS
    in_grp = (e_lane >= 0) & (e_lane < N_EXPERTS) & ((e_lane // EXPERTS_PER_GROUP) == grp)
    el = jnp.where(in_grp, logits, neg)
    t1 = jnp.max(el, axis=-1, keepdims=True)
    i1 = jnp.min(jnp.where(el == t1, lane, ROUTE_LANES), axis=-1, keepdims=True)
    el2 = jnp.where(lane == i1, neg, el)
    t2 = jnp.max(el2, axis=-1, keepdims=True)
    i2 = jnp.min(jnp.where(el2 == t2, lane, ROUTE_LANES), axis=-1, keepdims=True)
    r21 = jnp.exp(t2 - t1)
    w0 = g_w / (1.0 + r21)
    w1 = g_w * r21 / (1.0 + r21)
    e0 = (i1 - N_GROUPS).astype(F32)
    e1 = (i2 - N_GROUPS).astype(F32)
    rows = logits.shape[0]
    picked0 = lane == i1
    picked1 = lane == i2
    member = (picked0 | picked1).astype(BF16)
    earlier = (lax.broadcasted_iota(jnp.int32, (rows, rows), 1)
               < lax.broadcasted_iota(jnp.int32, (rows, rows), 0)).astype(BF16)
    before = jnp.dot(earlier, member, preferred_element_type=F32) + count_s[...]
    r0 = jnp.sum(jnp.where(picked0, before, 0.0), axis=-1, keepdims=True)
    r1 = jnp.sum(jnp.where(picked1, before, 0.0), axis=-1, keepdims=True)
    count_s[...] += jnp.sum(member.astype(F32), axis=0, keepdims=True)
    count_ref[...] = count_s[...]
    route_ref[...] = jnp.where(lane == 0, e0, jnp.where(lane == 1, e1,
                               jnp.where(lane == 2, w0, jnp.where(lane == 3, w1,
                               jnp.where(lane == 4, r0, jnp.where(lane == 5, r1, 0.0))))))


def _outproj(att_p, yc_p, x_p, att_s, yc_s, x_s, w_top, w_bot, norm_g, rw, rb):
    tm = OUT_TM
    n_p = x_p.shape[0] // tm
    n_s = x_s.shape[0] // tm
    t_all = x_p.shape[0] + x_s.shape[0]
    prow = lambda i: (jnp.minimum(i, n_p - 1), 0)
    srow = lambda i: (jnp.maximum(i - n_p, 0), 0)
    row = lambda i: (i, 0)
    const = lambda i: (0, 0)
    kern = functools.partial(_outproj_kernel, n_prompt_tiles=n_p)
    return pl.pallas_call(
        kern,
        grid=(n_p + n_s,),
        in_specs=[
            pl.BlockSpec((tm, ATT_WIDTH), prow),
            pl.BlockSpec((tm, CONV_CH), prow),
            pl.BlockSpec((tm, D_MODEL), prow),
            pl.BlockSpec((tm, ATT_WIDTH), srow),
            pl.BlockSpec((tm, CONV_CH), srow),
            pl.BlockSpec((tm, D_MODEL), srow),
            pl.BlockSpec((ATT_WIDTH, D_MODEL), const),
            pl.BlockSpec((CONV_CH, D_MODEL), const),
            pl.BlockSpec((1, D_MODEL), const),
            pl.BlockSpec((D_MODEL, ROUTE_LANES), const),
            pl.BlockSpec((1, ROUTE_LANES), const),
        ],
        out_specs=[
            pl.BlockSpec((tm, D_MODEL), row),
            pl.BlockSpec((SLAB, tm, LANES), lambda i: (0, i, 0)),
            pl.BlockSpec((tm, ROUTE_LANES), row),
            pl.BlockSpec((1, ROUTE_LANES), const),
        ],
        out_shape=[
            jax.ShapeDtypeStruct((t_all, D_MODEL), F32),
            jax.ShapeDtypeStruct((SLAB, t_all, LANES), U32),
            jax.ShapeDtypeStruct((t_all, ROUTE_LANES), F32),
            jax.ShapeDtypeStruct((1, ROUTE_LANES), F32),
        ],
        scratch_shapes=[pltpu.VMEM((1, ROUTE_LANES), F32)],
        compiler_params=_params(("arbitrary",), 52),
        name="outproj_router",
    )(att_p, yc_p, x_p, att_s, yc_s, x_s, w_top, w_bot, norm_g, rw, rb)


def _dispatch_copy(xf_ref, xs_hbm, sem, src_row, dst_row):
    return pltpu.make_async_copy(xf_ref.at[:, src_row, :],
                                 xs_hbm.at[pl.ds(dst_row * SLAB, SLAB)], sem)


def _zero_block_copy(zero_s, xs_hbm, sem, blk):
    span = MOE_BM * SLAB
    return pltpu.make_async_copy(zero_s, xs_hbm.at[pl.ds(blk * span, span)], sem)


def _dispatch_kernel(dest_ref, zero_blk_ref, xf_ref, xs_hbm, zero_s, sem):
    tm = DISPATCH_TM

    @pl.when(pl.program_id(0) == 0)
    def _():
        zero_s[...] = jnp.zeros(zero_s.shape, U32)

        def fill(j, start):
            blk = zero_blk_ref[0, 0, j]

            @pl.when(blk >= 0)
            def _():
                copy = _zero_block_copy(zero_s, xs_hbm, sem, blk)
                if start:
                    copy.start()
                else:
                    copy.wait()

        lax.fori_loop(0, zero_blk_ref.shape[-1], lambda j, c: (fill(j, True), c)[1], 0)
        lax.fori_loop(0, zero_blk_ref.shape[-1], lambda j, c: (fill(j, False), c)[1], 0)

    def issue(t, carry):
        for k in range(TOP_K):
            _dispatch_copy(xf_ref, xs_hbm, sem, t, dest_ref[0, 0, TOP_K * t + k]).start()
        return carry

    def drain(t, carry):
        for k in range(TOP_K):
            _dispatch_copy(xf_ref, xs_hbm, sem, t, dest_ref[0, 0, TOP_K * t + k]).wait()
        return carry

    lax.fori_loop(0, tm, issue, 0, unroll=ROW_DMA_UNROLL)
    lax.fori_loop(0, tm, drain, 0, unroll=ROW_DMA_UNROLL)


def _dispatch(dest, zero_blk, xf, cap):
    t_all = xf.shape[1]
    tm = DISPATCH_TM
    dest3 = dest.reshape(t_all // tm, 1, TOP_K * tm)
    zero_blk3 = zero_blk.reshape(1, 1, -1)
    return pl.pallas_call(
        _dispatch_kernel,
        grid=(t_all // tm,),
        in_specs=[
            pl.BlockSpec((1, 1, TOP_K * tm), lambda i: (i, 0, 0), memory_space=pltpu.SMEM),
            pl.BlockSpec(zero_blk3.shape, lambda i: (0, 0, 0), memory_space=pltpu.SMEM),
            pl.BlockSpec((SLAB, tm, LANES), lambda i: (0, i, 0)),
        ],
        out_specs=pl.BlockSpec(memory_space=pl.ANY),
        out_shape=jax.ShapeDtypeStruct((cap * SLAB, LANES), U32),
        scratch_shapes=[pltpu.VMEM((MOE_BM * SLAB, LANES), U32), pltpu.SemaphoreType.DMA(())],
        compiler_params=_params(("arbitrary",), 24),
        name="moe_dispatch",
    )(dest3, zero_blk3, xf)


def _expert_weight_copies(hbm_refs, stage_refs, sems, e):
    return [pltpu.make_async_copy(hbm.at[e], stage, sems.at[n])
            for n, (hbm, stage) in enumerate(zip(hbm_refs, stage_refs))]


def _experts_kernel(blk_e_ref, first_ref, next_e_ref, nused_ref, x_ref, wg_hbm, wu_hbm, wd_hbm, o_ref,
                    wg_f, wu_f, wd_f, wgu_s, wd_s, sems):
    i = pl.program_id(0)
    hbm_refs = (wg_hbm, wu_hbm, wd_hbm)
    stage_refs = (wg_f, wu_f, wd_f)

    @pl.when(i == 0)
    def _():
        for copy in _expert_weight_copies(hbm_refs, stage_refs, sems, blk_e_ref[0]):
            copy.start()

    @pl.when(i < nused_ref[0])
    def _():
        @pl.when(first_ref[i] == 1)
        def _():
            gate_copy, up_copy, down_copy = _expert_weight_copies(hbm_refs, stage_refs, sems, blk_e_ref[i])
            gate_copy.wait()
            wgu_s[:, 0:D_EXPERT] = wg_f[...].astype(BF16)
            up_copy.wait()
            wgu_s[:, D_EXPERT:2 * D_EXPERT] = wu_f[...].astype(BF16)
            down_copy.wait()
            wd_s[...] = wd_f[...].astype(BF16)

            @pl.when(next_e_ref[i] >= 0)
            def _():
                for copy in _expert_weight_copies(hbm_refs, stage_refs, sems, next_e_ref[i]):
                    copy.start()

        x_lo, x_hi = _unpack_bf16_pair(x_ref[...].reshape(MOE_BM, SLAB, LANES).reshape(MOE_BM, PACK_COLS))
        x = jnp.concatenate([x_lo, x_hi], axis=1).astype(BF16)
        gu = jnp.dot(x, wgu_s[...], preferred_element_type=F32)
        g = gu[:, 0:D_EXPERT]
        u = gu[:, D_EXPERT:2 * D_EXPERT]
        h = g / (1.0 + jnp.exp(-g)) * u
        out = jnp.dot(h.astype(BF16), wd_s[...], preferred_element_type=F32)
        packed = _pack_bf16_pair(out[:, 0:PACK_COLS], out[:, PACK_COLS:D_MODEL])
        o_ref[...] = packed.reshape(MOE_BM, SLAB, LANES).reshape(MOE_BM * SLAB, LANES)

    @pl.when(i >= nused_ref[0])
    def _():
        o_ref[...] = jnp.zeros(o_ref.shape, U32)


def _experts(blk_e, first, next_e, nused, xs, w_gate, w_up, w_down):
    cap = xs.shape[0] // SLAB
    bm = MOE_BM
    rows = lambda i, be, fi, ne, nu: (jnp.minimum(i, nu[0] - 1), 0)
    grid_spec = pltpu.PrefetchScalarGridSpec(
        num_scalar_prefetch=4,
        grid=(cap // bm,),
        in_specs=[
            pl.BlockSpec((bm * SLAB, LANES), rows),
            pl.BlockSpec(memory_space=pl.ANY),
            pl.BlockSpec(memory_space=pl.ANY),
            pl.BlockSpec(memory_space=pl.ANY),
        ],
        out_specs=pl.BlockSpec((bm * SLAB, LANES), lambda i, be, fi, ne, nu: (i, 0)),
        scratch_shapes=[
            pltpu.VMEM((D_MODEL, D_EXPERT), F32),
            pltpu.VMEM((D_MODEL, D_EXPERT), F32),
            pltpu.VMEM((D_EXPERT, D_MODEL), F32),
            pltpu.VMEM((D_MODEL, 2 * D_EXPERT), BF16),
            pltpu.VMEM((D_EXPERT, D_MODEL), BF16),
            pltpu.SemaphoreType.DMA((3,)),
        ],
    )
    return pl.pallas_call(
        _experts_kernel,
        grid_spec=grid_spec,
        out_shape=jax.ShapeDtypeStruct((cap * SLAB, LANES), U32),
        compiler_params=_params(("arbitrary",), 48),
        name="moe_experts",
    )(blk_e, first, next_e, nused, xs, w_gate, w_up, w_down)


def _combine_copy(outs_hbm, g_s, sems, slot, src_row, k, t):
    return pltpu.make_async_copy(outs_hbm.at[pl.ds(src_row * SLAB, SLAB)],
                                 g_s.at[slot, k, :, t, :], sems.at[slot])


def _combine_kernel(dest_ref, dest_next_ref, x1_ref, route_ref, outs_hbm, y_ref, g_s, sems, *, n_tiles):
    i = pl.program_id(0)
    tm = x1_ref.shape[0]
    slot = i % 2

    def gather(table_ref, into, start):
        def body(t, carry):
            for k in range(TOP_K):
                copy = _combine_copy(outs_hbm, g_s, sems, into, table_ref[0, 0, TOP_K * t + k], k, t)
                if start:
                    copy.start()
                else:
                    copy.wait()
            return carry
        lax.fori_loop(0, tm, body, 0, unroll=ROW_DMA_UNROLL)

    @pl.when(i == 0)
    def _():
        gather(dest_ref, 0, True)

    @pl.when(i + 1 < n_tiles)
    def _():
        gather(dest_next_ref, 1 - slot, True)

    gather(dest_ref, slot, False)
    r = route_ref[...]
    w0 = jnp.broadcast_to(r[:, 2:3], (tm, LANES))
    w1 = jnp.broadcast_to(r[:, 3:4], (tm, LANES))
    for c in range(SLAB):
        lo = slice(c * LANES, (c + 1) * LANES)
        hi = slice(PACK_COLS + c * LANES, PACK_COLS + (c + 1) * LANES)
        lo0, hi0 = _unpack_bf16_pair(g_s[slot, 0, c])
        lo1, hi1 = _unpack_bf16_pair(g_s[slot, 1, c])
        y_ref[:, lo] = x1_ref[:, lo] + w0 * lo0 + w1 * lo1
        y_ref[:, hi] = x1_ref[:, hi] + w0 * hi0 + w1 * hi1


def _combine(dest, x1, route, outs, row_off, t):
    tm = OUT_TM
    t_all = x1.shape[0]
    off = row_off // tm
    dest3 = dest.reshape(t_all // tm, 1, TOP_K * tm)
    n = t // tm
    return pl.pallas_call(
        functools.partial(_combine_kernel, n_tiles=n),
        grid=(n,),
        in_specs=[
            pl.BlockSpec((1, 1, TOP_K * tm), lambda i: (i + off, 0, 0), memory_space=pltpu.SMEM),
            pl.BlockSpec((1, 1, TOP_K * tm), lambda i: (jnp.minimum(i + 1, n - 1) + off, 0, 0),
                         memory_space=pltpu.SMEM),
            pl.BlockSpec((tm, D_MODEL), lambda i: (i + off, 0)),
            pl.BlockSpec((tm, ROUTE_LANES), lambda i: (i + off, 0)),
            pl.BlockSpec(memory_space=pl.ANY),
        ],
        out_specs=pl.BlockSpec((tm, D_MODEL), lambda i: (i, 0)),
        out_shape=jax.ShapeDtypeStruct((t, D_MODEL), F32),
        scratch_shapes=[pltpu.VMEM((2, TOP_K, SLAB, tm, LANES), U32), pltpu.SemaphoreType.DMA((2,))],
        compiler_params=_params(("arbitrary",), 32),
        name="moe_combine",
    )(dest3, dest3, x1, route, outs)


def _routing_tables(route, lane_counts, n_blocks):
    bm = MOE_BM
    flat_e = route[:, 0:TOP_K].astype(jnp.int32).reshape(-1)
    rank = route[:, 2 * TOP_K:3 * TOP_K].astype(jnp.int32).reshape(-1)
    counts = lane_counts[0, N_GROUPS:N_GROUPS + N_EXPERTS].astype(jnp.int32)
    padded = (counts + bm - 1) // bm * bm
    pad_end = jnp.cumsum(padded)
    pad_start = pad_end - padded
    experts = jnp.arange(N_EXPERTS, dtype=jnp.int32)
    onehot = (flat_e[:, None] == experts[None, :]).astype(BF16)
    start_blk = jnp.dot(onehot, (pad_start // bm).astype(BF16), preferred_element_type=F32)
    dest = start_blk.astype(jnp.int32) * bm + rank
    nused = (pad_end[-1] // bm).astype(jnp.int32)
    blk = jnp.arange(n_blocks, dtype=jnp.int32)
    blk_row = jnp.minimum(blk, nused - 1) * bm
    blk_e = jnp.sum((pad_end[None, :] <= blk_row[:, None]).astype(jnp.int32), axis=1)
    blk_e = jnp.minimum(blk_e, N_EXPERTS - 1)
    first = jnp.concatenate([jnp.ones((1,), jnp.int32),
                             (blk_e[1:] != blk_e[:-1]).astype(jnp.int32)])
    later_owner = jnp.where((counts[None, :] > 0) & (experts[None, :] > experts[:, None]),
                            experts[None, :], N_EXPERTS)
    next_owner = jnp.min(later_owner, axis=1)
    next_owner = jnp.where(next_owner == N_EXPERTS, -1, next_owner).astype(jnp.int32)
    last_blk = jnp.where(counts > 0, pad_end // bm - 1, -1).astype(jnp.int32)
    tail = nused + jnp.arange(N_EXPERTS, dtype=jnp.int32)
    tail_blk = jnp.where(tail < n_blocks, tail, -1)
    zero_blk = jnp.concatenate([last_blk, tail_blk])
    return dest, zero_blk, blk_e, first, next_owner[blk_e], nused.reshape(1)


def _rope_tables(pos):
    half = HEAD_DIM // 2
    inv = ROPE_THETA ** (-jnp.arange(half, dtype=F32) / half)
    ang = pos.astype(F32)[:, None] * inv[None, :]
    cos = jnp.cos(ang)
    sin = jnp.sin(ang)
    cos_h = jnp.concatenate([cos, cos], axis=-1)
    sin_h = jnp.concatenate([-sin, sin], axis=-1)
    reps = LANES // HEAD_DIM
    return jnp.tile(cos_h, (1, reps)), jnp.tile(sin_h, (1, reps))


def kernel(x_prompt, x_sample, cache_k, cache_v, state_conv, norm_mix_g, w_in, q_norm_g, k_norm_g, lambda_q1, lambda_k1, lambda_q2, lambda_k2, subln_g, conv_w, conv_norm_g, w_out, norm_ffn_g, router_group_w, router_group_b, router_expert_w, router_expert_b, expert_w_gate, expert_w_up, expert_w_down):
    assert w_in.shape[0] == 1, "single-layer step"
    bp, sp, _ = x_prompt.shape
    bs, ss, _ = x_sample.shape
    past = cache_k.shape[2]
    assert bp == 1 and sp % ATT_T == 0
    tp = bp * sp
    ts = bs * ss
    t_all = tp + ts
    lambda_init = 0.8 - 0.6 * math.exp(-0.3 * 0)

    w_in_bf = w_in[0].astype(BF16)
    w_out_bf = w_out[0].astype(BF16)
    w_top, w_bot = w_out_bf[:ATT_WIDTH], w_out_bf[ATT_WIDTH:]
    ng = norm_mix_g[0].reshape(1, D_MODEL)
    qg = jnp.tile(q_norm_g[0], QK_WIDTH // HEAD_DIM).reshape(1, QK_WIDTH)
    kg = jnp.tile(k_norm_g[0], QK_WIDTH // HEAD_DIM).reshape(1, QK_WIDTH)
    head_of = jnp.arange(IN_CH, dtype=jnp.int32) // HEAD_DIM
    gmat = jnp.where(head_of[:, None] == head_of[None, :], 1.0 / HEAD_DIM, 0.0).astype(BF16)
    lamv = jnp.stack([lambda_q1[0], lambda_k1[0], lambda_q2[0], lambda_k2[0]]).astype(F32)
    sg = subln_g[0].reshape(1, V_DIM)
    cw = conv_w[0]
    cng = conv_norm_g[0].reshape(1, CONV_CH)
    cos_p, sin_p = _rope_tables(jnp.arange(sp, dtype=jnp.int32))
    cos_s, sin_s = _rope_tables(jnp.tile(past + jnp.arange(ss, dtype=jnp.int32), bs))

    zero_conv = jnp.zeros((1, CONV_K - 1, CONV_CH), F32)
    qt_p, kf_p, kb_p, vf_p, vt_p, yc_p, tail_p = _inproj(
        x_prompt.reshape(tp, D_MODEL), ng, w_in_bf, qg, kg, gmat, cos_p, sin_p, zero_conv, cw, cng,
        tm=ATT_T, nseq=1, carry=True, qv_transposed=True)
    att_p = _attn_prompt(qt_p, kb_p, vt_p, lamv, subln_g[0].reshape(V_DIM, 1), lambda_init)

    seqs_per_tile = ATT_T // ss
    q_s, kf_s, kb_s, vf_s, vb_s, yc_s, tail_s = _inproj(
        x_sample.reshape(ts, D_MODEL), ng, w_in_bf, qg, kg, gmat, cos_s, sin_s, state_conv[0], cw, cng,
        tm=ATT_T, nseq=seqs_per_tile, carry=False, qv_transposed=False)
    kt_cache = jnp.transpose(cache_k[0], (0, 2, 3, 4, 1)).reshape(bs, QK_WIDTH, past)
    att_s = _attn_sample(q_s, kt_cache, cache_v[0].reshape(bs, past * N_HEADS, V_DIM),
                         kb_s, vb_s, lamv, sg, lambda_init)

    rw = jnp.zeros((D_MODEL, ROUTE_LANES), F32)
    rw = rw.at[:, 0:N_GROUPS].set(router_group_w[0]).at[:, N_GROUPS:N_GROUPS + N_EXPERTS].set(router_expert_w[0])
    rb = jnp.zeros((1, ROUTE_LANES), F32)
    rb = rb.at[0, 0:N_GROUPS].set(router_group_b[0]).at[0, N_GROUPS:N_GROUPS + N_EXPERTS].set(router_expert_b[0])
    nf = norm_ffn_g[0].reshape(1, D_MODEL)
    rw_bf = rw.astype(BF16)
    x1, xf, route, lane_counts = _outproj(att_p, yc_p, x_prompt.reshape(tp, D_MODEL), att_s, yc_s,
                                          x_sample.reshape(ts, D_MODEL), w_top, w_bot, nf, rw_bf, rb)

    n = t_all * TOP_K
    n_blocks = n // MOE_BM + N_EXPERTS
    dest, zero_blk, blk_e, first, next_e, nused = _routing_tables(route, lane_counts, n_blocks)
    xs = _dispatch(dest, zero_blk, xf, n_blocks * MOE_BM)
    outs = _experts(blk_e, first, next_e, nused, xs, expert_w_gate[0], expert_w_up[0], expert_w_down[0])
    y_p = _combine(dest, x1, route, outs, 0, tp)
    y_s = _combine(dest, x1, route, outs, tp, ts)

    return (y_p.reshape(bp, sp, D_MODEL),
            y_s.reshape(bs, ss, D_MODEL),
            kf_p.reshape(1, bp, sp, N_HEADS, 2, HEAD_DIM),
            vf_p.reshape(1, bp, sp, N_HEADS, V_DIM),
            tail_p.reshape(1, bp, CONV_K - 1, CONV_CH),
            kf_s.reshape(1, bs, ss, N_HEADS, 2, HEAD_DIM),
            vf_s.reshape(1, bs, ss, N_HEADS, V_DIM),
            tail_s.reshape(1, bs, CONV_K - 1, CONV_CH))
```

```python
import functools
import math

import jax
import jax.numpy as jnp
from jax import lax
from jax.experimental import pallas as pl
from jax.experimental.pallas import tpu as pltpu

F32 = jnp.float32
BF16 = jnp.bfloat16

D_MODEL = 2048
CHUNK = 64
HEAD_DIM = 64
V_DIM = 2 * HEAD_DIM
N_HEADS = 8
QK_WIDTH = N_HEADS * 2 * HEAD_DIM
ATT_WIDTH = N_HEADS * V_DIM
CONV_CH = 1024
CONV_K = 3
ROPE_THETA = 10000.0
N_GROUPS = 8
EXPERTS_PER_GROUP = 8
N_EXPERTS = N_GROUPS * EXPERTS_PER_GROUP
TOP_K = 2
D_EXPERT = D_MODEL // 4
NORM_EPS = 1e-6
SECTION = 1024
N_SECTIONS = 6

LANES = 128
IN_SPS = 2
IN_CH = 256
ATT_AHEAD = 2
ATT_ONES = 16
ATT_QPS = 8
ATT_UNROLL = 4
ATT_CB = 512
ATT_T = 512
OUT_TM = 256
COMBINE_TM = 512
MOE_BM = 256
DISPATCH_TM = 512
ROW_DMA_UNROLL = 8
PACK_COLS = D_MODEL // 2
SLAB = PACK_COLS // LANES
U32 = jnp.uint32
ROUTE_LANES = LANES
MASKED = -1e30
Q_SCALE = HEAD_DIM ** -0.5 * math.log2(math.e)
MIB = 1024 * 1024


def _pack_bf16_pair(lo, hi):
    lo_bits = lax.bitcast_convert_type(lo.astype(BF16).astype(F32), U32)
    hi_bits = lax.bitcast_convert_type(hi.astype(BF16).astype(F32), U32)
    return hi_bits | (lo_bits >> 16)


def _unpack_bf16_pair(words):
    lo = lax.bitcast_convert_type(words << 16, F32)
    hi = lax.bitcast_convert_type(words & U32(0xFFFF0000), F32)
    return lo, hi


def _params(sem, vmem_mib):
    return pltpu.CompilerParams(dimension_semantics=sem, vmem_limit_bytes=vmem_mib * MIB)


def _inproj_kernel(x_ref, ng_ref, w_ref, qg_ref, kg_ref, gmat_ref, cos_ref, sin_ref,
                   prev_ref, cw_ref, cng_ref,
                   q_ref, kf_ref, kb_ref, vf_ref, vb_ref, yc_ref, tail_ref,
                   xn_s, gb_s, c_s, z_s, u_s, *, nseq, carry, qv_transposed):
    i = pl.program_id(0)
    j = pl.program_id(1)
    tm = x_ref.shape[0]
    seq = tm // nseq

    @pl.when(j == 0)
    def _():
        x = x_ref[...]
        inv = lax.rsqrt(jnp.mean(x * x, axis=-1, keepdims=True) + NORM_EPS)
        xn_s[...] = (x * inv * ng_ref[...]).astype(BF16)
        if carry:
            @pl.when(i == 0)
            def _():
                u_s[6:8, :] = prev_ref[0]

    def chunk_dot(col):
        return jnp.dot(xn_s[...], w_ref[:, col:col + IN_CH], preferred_element_type=F32)

    def pipelined(stages):
        acc = chunk_dot(stages[0][0])
        for n, (_, epilogue, cols) in enumerate(stages):
            nxt = chunk_dot(stages[n + 1][0]) if n + 1 < len(stages) else None
            epilogue(acc, cols)
            acc = nxt

    def head_norm_rope(a, g):
        ms = jnp.dot((a * a).astype(BF16), gmat_ref[...], preferred_element_type=F32)
        y = a * lax.rsqrt(ms + NORM_EPS) * g
        reps = IN_CH // LANES
        cos = jnp.concatenate([cos_ref[...]] * reps, axis=1)
        sin = jnp.concatenate([sin_ref[...]] * reps, axis=1)
        lane = lax.broadcasted_iota(jnp.int32, y.shape, 1)
        first = (lane & (HEAD_DIM - 1)) < HEAD_DIM // 2
        partner = jnp.where(first, pltpu.roll(y, IN_CH - HEAD_DIM // 2, 1),
                            pltpu.roll(y, HEAD_DIM // 2, 1))
        return y * cos + partner * sin

    epilogues = []
    for section in range(N_SECTIONS):

        def epilogue(acc, cols, section=section):
            if section == 0:
                q = head_norm_rope(acc, qg_ref[:, cols]) * Q_SCALE
                if qv_transposed:
                    q_ref[0, cols, :] = q.T.astype(BF16)
                else:
                    q_ref[:, cols] = q.astype(BF16)
            elif section == 1:
                k = head_norm_rope(acc, kg_ref[:, cols])
                kf_ref[:, cols] = k
                kb_ref[:, cols] = k.astype(BF16)
            elif section == 2:
                vf_ref[:, cols] = acc
                if qv_transposed:
                    vb_ref[0, cols, :] = acc.T.astype(BF16)
                else:
                    vb_ref[:, cols] = acc.astype(BF16)
            elif section == 3:
                gb_s[:, cols] = acc
            elif section == 4:
                c_s[:, cols] = acc
            else:
                u = c_s[:, cols] * acc
                w0 = cw_ref[0:1, cols]
                w1 = cw_ref[1:2, cols]
                w2 = cw_ref[2:3, cols]
                for s in range(nseq):
                    rows = slice(s * seq, (s + 1) * seq)
                    if not carry:
                        u_s[6:8, cols] = prev_ref[s, :, cols]
                    u_s[8:8 + seq, cols] = u[rows]
                    conv = (w0 * u_s[6:6 + seq, cols] + w1 * u_s[7:7 + seq, cols]
                            + w2 * u_s[8:8 + seq, cols])
                    z_s[rows, cols] = gb_s[rows, cols] * conv
                    tail = u_s[seq + 6:seq + 8, cols]
                    tail_ref[s, :, cols] = tail
                    if carry:
                        u_s[6:8, cols] = tail

        epilogues.append(epilogue)

    n_steps = N_SECTIONS // IN_SPS
    for step in range(n_steps):
        stages = [(sub * SECTION + c * IN_CH, epilogues[step * IN_SPS + sub],
                   slice(c * IN_CH, (c + 1) * IN_CH))
                  for sub in range(IN_SPS) for c in range(SECTION // IN_CH)]

        @pl.when(j == step)
        def _(step=step, stages=stages):
            pipelined(stages)
            if step == n_steps - 1:
                z = z_s[...]
                inv = lax.rsqrt(jnp.mean(z * z, axis=-1, keepdims=True) + NORM_EPS)
                yc_ref[...] = (z * inv * cng_ref[...]).astype(BF16)


def _inproj(x2d, norm_g, w_in_bf, qg, kg, gmat, cos, sin, conv_prev, conv_w, conv_norm_g,
            *, tm, nseq, carry, qv_transposed):
    t = x2d.shape[0]
    ni = t // tm
    seq = tm // nseq
    row = lambda i, j: (i, 0)
    const = lambda i, j: (0, 0)
    if qv_transposed:
        qv_shape = jax.ShapeDtypeStruct((ni, QK_WIDTH, tm), BF16)
        qv_spec = pl.BlockSpec((1, QK_WIDTH, tm), lambda i, j: (i, 0, 0))
    else:
        qv_shape = jax.ShapeDtypeStruct((t, QK_WIDTH), BF16)
        qv_spec = pl.BlockSpec((tm, QK_WIDTH), row)
    if carry:
        prev_spec = pl.BlockSpec((1, CONV_K - 1, CONV_CH), lambda i, j: (0, 0, 0))
        tail_shape = jax.ShapeDtypeStruct((1, CONV_K - 1, CONV_CH), F32)
        tail_spec = pl.BlockSpec((1, CONV_K - 1, CONV_CH), lambda i, j: (0, 0, 0))
    else:
        prev_spec = pl.BlockSpec((nseq, CONV_K - 1, CONV_CH), lambda i, j: (i, 0, 0))
        tail_shape = jax.ShapeDtypeStruct((ni * nseq, CONV_K - 1, CONV_CH), F32)
        tail_spec = pl.BlockSpec((nseq, CONV_K - 1, CONV_CH), lambda i, j: (i, 0, 0))
    kern = functools.partial(_inproj_kernel, nseq=nseq, carry=carry, qv_transposed=qv_transposed)
    return pl.pallas_call(
        kern,
        grid=(ni, N_SECTIONS // IN_SPS),
        in_specs=[
            pl.BlockSpec((tm, D_MODEL), row),
            pl.BlockSpec((1, D_MODEL), const),
            pl.BlockSpec((D_MODEL, IN_SPS * SECTION), lambda i, j: (0, j)),
            pl.BlockSpec((1, SECTION), const),
            pl.BlockSpec((1, SECTION), const),
            pl.BlockSpec((IN_CH, IN_CH), const),
            pl.BlockSpec((tm, LANES), row),
            pl.BlockSpec((tm, LANES), row),
            prev_spec,
            pl.BlockSpec((CONV_K, CONV_CH), const),
            pl.BlockSpec((1, CONV_CH), const),
        ],
        out_specs=[
            qv_spec,
            pl.BlockSpec((tm, QK_WIDTH), row),
            pl.BlockSpec((tm, QK_WIDTH), row),
            pl.BlockSpec((tm, ATT_WIDTH), row),
            qv_spec,
            pl.BlockSpec((tm, CONV_CH), row),
            tail_spec,
        ],
        out_shape=[
            qv_shape,
            jax.ShapeDtypeStruct((t, QK_WIDTH), F32),
            jax.ShapeDtypeStruct((t, QK_WIDTH), BF16),
            jax.ShapeDtypeStruct((t, ATT_WIDTH), F32),
            qv_shape,
            jax.ShapeDtypeStruct((t, CONV_CH), BF16),
            tail_shape,
        ],
        scratch_shapes=[
            pltpu.VMEM((tm, D_MODEL), BF16),
            pltpu.VMEM((tm, CONV_CH), F32),
            pltpu.VMEM((tm, CONV_CH), F32),
            pltpu.VMEM((tm, CONV_CH), F32),
            pltpu.VMEM((seq + 8, CONV_CH), F32),
        ],
        compiler_params=_params(("arbitrary", "arbitrary"), 58),
        name="inproj_carry" if carry else "inproj_seqs",
    )(x2d, norm_g, w_in_bf, qg, kg, gmat, cos, sin, conv_prev, conv_w, conv_norm_g)


def _lambda_full(lamv_ref, lambda_init):
    lv = lamv_ref[...]
    a = jnp.sum(lv[0:1] * lv[1:2], axis=-1, keepdims=True)
    b = jnp.sum(lv[2:3] * lv[3:4], axis=-1, keepdims=True)
    return jnp.exp(a) - jnp.exp(b) + lambda_init


def _split_heads(q):
    lane = lax.broadcasted_iota(jnp.int32, q.shape, 1)
    zero = jnp.zeros_like(q)
    return jnp.where(lane < HEAD_DIM, q, zero), jnp.where(lane >= HEAD_DIM, q, zero)


def _diff_finish(acc, l, lam, g, lambda_init, rows):
    a = acc / l
    d = a[0:rows] - lam * a[rows:2 * rows]
    inv = lax.rsqrt(jnp.mean(d * d, axis=-1, keepdims=True) + NORM_EPS)
    return d * inv * g * (1.0 - lambda_init)


def _attn_prompt_kernel(qt_ref, k_ref, vt_ref, lamv_ref, g_ref, o_ref, qq_s, m_s, acc_s, s_s,
                        *, lambda_init):
    for sub in range(ATT_QPS):
        _attn_prompt_tile(ATT_QPS * pl.program_id(1) + sub, qt_ref.at[sub], k_ref, vt_ref, lamv_ref, g_ref,
                          o_ref.at[pl.ds(sub * ATT_T, ATT_T)], qq_s.at[sub], m_s.at[sub], acc_s.at[sub],
                          s_s.at[sub], lambda_init=lambda_init)


def _attn_prompt_tile(i, qt_ref, k_ref, vt_ref, lamv_ref, g_ref, o_ref, qq_s, m_s, acc_s, s_s,
                      *, lambda_init):
    t = ATT_T
    qt = qt_ref[...]
    feat = lax.broadcasted_iota(jnp.int32, qt.shape, 0)
    zero = jnp.zeros_like(qt)
    qq_s[:, 0:t] = jnp.where(feat < HEAD_DIM, qt, zero)
    qq_s[:, t:2 * t] = jnp.where(feat >= HEAD_DIM, qt, zero)
    m_s[...] = jnp.full(m_s.shape, MASKED, F32)
    acc_s[...] = jnp.zeros(acc_s.shape, F32)
    ones_rows = (lax.broadcasted_iota(jnp.int32, (ATT_ONES, t), 0) == 0).astype(BF16)

    nblk = 2 * t // ATT_CB

    def scores(kt, cb):
        k = k_ref[pl.ds(pl.multiple_of(kt * t, t), t), :]
        s = jnp.dot(k, qq_s[:, cb * ATT_CB:(cb + 1) * ATT_CB], preferred_element_type=F32)
        s_s[cb] = s.astype(BF16)

    def step(kt, diagonal):
        vt = jnp.concatenate([vt_ref[kt], ones_rows], axis=0)
        for cb in range(nblk):
            cs = slice(cb * ATT_CB, (cb + 1) * ATT_CB)
            s = s_s[cb]
            if diagonal:
                key = lax.broadcasted_iota(jnp.int32, s.shape, 0)
                qry = lax.broadcasted_iota(jnp.int32, s.shape, 1) + (cb * ATT_CB) % t
                s = s + jnp.where((key // CHUNK) <= (qry // CHUNK), 0.0, MASKED).astype(BF16)
            m_prev = m_s[:, cs]
            m_new = jnp.maximum(m_prev, jnp.max(s, axis=0, keepdims=True).astype(F32))
            alpha = jnp.exp2(m_prev - m_new)
            p = jnp.exp2(s - m_new.astype(BF16))
            m_s[:, cs] = m_new
            ahead = cb + ATT_AHEAD
            if ahead < nblk:
                scores(kt, ahead)
            elif not diagonal:
                scores(kt + 1, ahead - nblk)
            acc_s[:, cs] = alpha * acc_s[:, cs] + jnp.dot(vt, p, preferred_element_type=F32)

    def group_body(g, carry):
        for u in range(ATT_UNROLL):
            step(ATT_UNROLL * g + u, False)
        return carry

    def single_body(kt, carry):
        step(kt, False)
        return carry

    for cb in range(ATT_AHEAD):
        scores(0, cb)
    n_grouped = i // ATT_UNROLL * ATT_UNROLL
    lax.fori_loop(0, i // ATT_UNROLL, group_body, 0)
    lax.fori_loop(n_grouped, i, single_body, 0)
    step(i, True)
    lam = _lambda_full(lamv_ref, lambda_init)
    a = acc_s[0:V_DIM, :] / acc_s[V_DIM:V_DIM + 1, :]
    d = a[:, 0:t] - lam * a[:, t:2 * t]
    inv = lax.rsqrt(jnp.mean(d * d, axis=0, keepdims=True) + NORM_EPS)
    y = d * inv * g_ref[...] * (1.0 - lambda_init)
    o_ref[...] = y.T.astype(BF16)


def _attn_prompt(qt_bf, k_bf, vt_bf, lamv, subln_g_col, lambda_init):
    nq = qt_bf.shape[0]
    t = nq * ATT_T
    kern = functools.partial(_attn_prompt_kernel, lambda_init=lambda_init)
    return pl.pallas_call(
        kern,
        grid=(N_HEADS, nq // ATT_QPS),
        in_specs=[
            pl.BlockSpec((ATT_QPS, V_DIM, ATT_T), lambda h, i: (i, h, 0)),
            pl.BlockSpec((t, V_DIM), lambda h, i: (0, h)),
            pl.BlockSpec((nq, V_DIM, ATT_T), lambda h, i: (0, h, 0)),
            pl.BlockSpec((4, HEAD_DIM), lambda h, i: (0, 0)),
            pl.BlockSpec((V_DIM, 1), lambda h, i: (0, 0)),
        ],
        out_specs=pl.BlockSpec((ATT_QPS * ATT_T, V_DIM), lambda h, i: (i, h)),
        out_shape=jax.ShapeDtypeStruct((t, ATT_WIDTH), BF16),
        scratch_shapes=[
            pltpu.VMEM((ATT_QPS, V_DIM, 2 * ATT_T), BF16),
            pltpu.VMEM((ATT_QPS, 1, 2 * ATT_T), F32),
            pltpu.VMEM((ATT_QPS, V_DIM + ATT_ONES, 2 * ATT_T), F32),
            pltpu.VMEM((ATT_QPS, 2 * ATT_T // ATT_CB, ATT_T, ATT_CB), BF16),
        ],
        compiler_params=_params(("arbitrary", "arbitrary"), 48),
        name="attn_prompt",
    )(qt_bf, k_bf, vt_bf, lamv, subln_g_col)


def _attn_sample_kernel(q_ref, kc_ref, vc_ref, kn_ref, vn_ref, lamv_ref, g_ref, o_ref,
                        *, lambda_init, past):
    rows = q_ref.shape[0]
    lam = _lambda_full(lamv_ref, lambda_init)
    contract_last = (((1,), (1,)), ((), ()))
    for h in range(N_HEADS):
        hs = slice(h * V_DIM, (h + 1) * V_DIM)
        q1, q2 = _split_heads(q_ref[:, hs])
        qq = jnp.concatenate([q1, q2], axis=0)
        s_c = jnp.dot(qq, kc_ref[0, hs, :].astype(BF16), preferred_element_type=F32)
        s_n = lax.dot_general(qq, kn_ref[:, hs], contract_last, preferred_element_type=F32)
        r = lax.broadcasted_iota(jnp.int32, s_n.shape, 0)
        c = lax.broadcasted_iota(jnp.int32, s_n.shape, 1)
        s_n = jnp.where(((past + c) // CHUNK) <= ((past + r % rows) // CHUNK), s_n, MASKED)
        m = jnp.maximum(jnp.max(s_c, axis=-1, keepdims=True), jnp.max(s_n, axis=-1, keepdims=True))
        p_c = jnp.exp2(s_c - m)
        p_n = jnp.exp2(s_n - m)
        l = jnp.sum(p_c, axis=-1, keepdims=True) + jnp.sum(p_n, axis=-1, keepdims=True)
        v_c = vc_ref[0, pl.ds(h, past, stride=N_HEADS), :].astype(BF16)
        acc = (jnp.dot(p_c.astype(BF16), v_c, preferred_element_type=F32)
               + jnp.dot(p_n.astype(BF16), vn_ref[:, hs], preferred_element_type=F32))
        o_ref[:, hs] = _diff_finish(acc, l, lam, g_ref[...], lambda_init, rows).astype(BF16)


def _attn_sample(q_bf, kt_cache, v_cache, k_bf, v_bf, lamv, subln_g, lambda_init):
    nb, _, past = kt_cache.shape
    rows = q_bf.shape[0] // nb
    kern = functools.partial(_attn_sample_kernel, lambda_init=lambda_init, past=past)
    new_spec = pl.BlockSpec((rows, ATT_WIDTH), lambda b: (b, 0))
    return pl.pallas_call(
        kern,
        grid=(nb,),
        in_specs=[
            new_spec,
            pl.BlockSpec((1, QK_WIDTH, past), lambda b: (b, 0, 0)),
            pl.BlockSpec((1, past * N_HEADS, V_DIM), lambda b: (b, 0, 0)),
            new_spec, new_spec,
            pl.BlockSpec((4, HEAD_DIM), lambda b: (0, 0)),
            pl.BlockSpec((1, V_DIM), lambda b: (0, 0)),
        ],
        out_specs=new_spec,
        out_shape=jax.ShapeDtypeStruct(q_bf.shape, BF16),
        compiler_params=_params(("arbitrary",), 48),
        name="attn_sample",
    )(q_bf, kt_cache, v_cache, k_bf, v_bf, lamv, subln_g)


def _outproj_kernel(att_p_ref, yc_p_ref, x_p_ref, att_s_ref, yc_s_ref, x_s_ref,
                    wt_ref, wb_ref, ng_ref, rw_ref, rb_ref, x1_ref, xf_ref, route_ref, count_ref, count_s,
                    *, n_prompt_tiles):
    i = pl.program_id(0)
    shared = (wt_ref, wb_ref, ng_ref, rw_ref, rb_ref, x1_ref, xf_ref, route_ref, count_ref, count_s)

    @pl.when(i == 0)
    def _():
        count_s[...] = jnp.zeros(count_s.shape, F32)

    @pl.when(i < n_prompt_tiles)
    def _():
        _outproj_tile(att_p_ref, yc_p_ref, x_p_ref, *shared)

    @pl.when(i >= n_prompt_tiles)
    def _():
        _outproj_tile(att_s_ref, yc_s_ref, x_s_ref, *shared)


def _outproj_tile(att_ref, yc_ref, x_ref, wt_ref, wb_ref, ng_ref, rw_ref, rb_ref,
                  x1_ref, xf_ref, route_ref, count_ref, count_s):
    o = (jnp.dot(att_ref[...], wt_ref[...], preferred_element_type=F32)
         + jnp.dot(yc_ref[...], wb_ref[...], preferred_element_type=F32))
    x1 = x_ref[...] + o
    x1_ref[...] = x1
    xf = x1 * lax.rsqrt(jnp.mean(x1 * x1, axis=-1, keepdims=True) + NORM_EPS) * ng_ref[...]
    for c in range(SLAB):
        lo = slice(c * LANES, (c + 1) * LANES)
        hi = slice(PACK_COLS + c * LANES, PACK_COLS + (c + 1) * LANES)
        xf_ref[c] = _pack_bf16_pair(xf[:, lo], xf[:, hi])
    logits =jnp.dot(xf.astype(BF16), rw_ref[...], preferred_element_type=F32) + rb_ref[...]

    lane = lax.broadcasted_iota(jnp.int32, logits.shape, 1)
    neg = -jnp.inf
    is_group = lane < N_GROUPS
    gl = jnp.where(is_group, logits, neg)
    gmax = jnp.max(gl, axis=-1, keepdims=True)
    grp = jnp.min(jnp.where(gl == gmax, lane, ROUTE_LANES), axis=-1, keepdims=True)
    gsum = jnp.sum(jnp.where(is_group, jnp.exp(gl - gmax), 0.0), axis=-1, keepdims=True)
    g_w = 1.0 / gsum
    e_lane = lane - N_GROUPS
    in_grp = (e_lane >= 0) & (e_lane < N_EXPERTS) & ((e_lane // EXPERTS_PER_GROUP) == grp)
    el = jnp.where(in_grp, logits, neg)
    t1 = jnp.max(el, axis=-1, keepdims=True)
    i1 = jnp.min(jnp.where(el == t1, lane, ROUTE_LANES), axis=-1, keepdims=True)
    el2 = jnp.where(lane == i1, neg, el)
    t2 = jnp.max(el2, axis=-1, keepdims=True)
    i2 = jnp.min(jnp.where(el2 == t2, lane, ROUTE_LANES), axis=-1, keepdims=True)
    r21 = jnp.exp(t2 - t1)
    w0 = g_w / (1.0 + r21)
    w1 = g_w * r21 / (1.0 + r21)
    e0 = (i1 - N_GROUPS).astype(F32)
    e1 = (i2 - N_GROUPS).astype(F32)
    rows = logits.shape[0]
    picked0 = lane == i1
    picked1 = lane == i2
    member = (picked0 | picked1).astype(BF16)
    earlier = (lax.broadcasted_iota(jnp.int32, (rows, rows), 1)
               < lax.broadcasted_iota(jnp.int32, (rows, rows), 0)).astype(BF16)
    before = jnp.dot(earlier, member, preferred_element_type=F32) + count_s[...]
    r0 = jnp.sum(jnp.where(picked0, before, 0.0), axis=-1, keepdims=True)
    r1 = jnp.sum(jnp.where(picked1, before, 0.0), axis=-1, keepdims=True)
    count_s[...] += jnp.sum(member.astype(F32), axis=0, keepdims=True)
    count_ref[...] = count_s[...]
    route_ref[...] = jnp.where(lane == 0, e0, jnp.where(lane == 1, e1,
                               jnp.where(lane == 2, w0, jnp.where(lane == 3, w1,
                               jnp.where(lane == 4, r0, jnp.where(lane == 5, r1, 0.0))))))


def _outproj(att_p, yc_p, x_p, att_s, yc_s, x_s, w_top, w_bot, norm_g, rw, rb):
    tm = OUT_TM
    n_p = x_p.shape[0] // tm
    n_s = x_s.shape[0] // tm
    t_all = x_p.shape[0] + x_s.shape[0]
    prow = lambda i: (jnp.minimum(i, n_p - 1), 0)
    srow = lambda i: (jnp.maximum(i - n_p, 0), 0)
    row = lambda i: (i, 0)
    const = lambda i: (0, 0)
    kern = functools.partial(_outproj_kernel, n_prompt_tiles=n_p)
    return pl.pallas_call(
        kern,
        grid=(n_p + n_s,),
        in_specs=[
            pl.BlockSpec((tm, ATT_WIDTH), prow),
            pl.BlockSpec((tm, CONV_CH), prow),
            pl.BlockSpec((tm, D_MODEL), prow),
            pl.BlockSpec((tm, ATT_WIDTH), srow),
            pl.BlockSpec((tm, CONV_CH), srow),
            pl.BlockSpec((tm, D_MODEL), srow),
            pl.BlockSpec((ATT_WIDTH, D_MODEL), const),
            pl.BlockSpec((CONV_CH, D_MODEL), const),
            pl.BlockSpec((1, D_MODEL), const),
            pl.BlockSpec((D_MODEL, ROUTE_LANES), const),
            pl.BlockSpec((1, ROUTE_LANES), const),
        ],
        out_specs=[
            pl.BlockSpec((tm, D_MODEL), row),
            pl.BlockSpec((SLAB, tm, LANES), lambda i: (0, i, 0)),
            pl.BlockSpec((tm, ROUTE_LANES), row),
            pl.BlockSpec((1, ROUTE_LANES), const),
        ],
        out_shape=[
            jax.ShapeDtypeStruct((t_all, D_MODEL), F32),
            jax.ShapeDtypeStruct((SLAB, t_all, LANES), U32),
            jax.ShapeDtypeStruct((t_all, ROUTE_LANES), F32),
            jax.ShapeDtypeStruct((1, ROUTE_LANES), F32),
        ],
        scratch_shapes=[pltpu.VMEM((1, ROUTE_LANES), F32)],
        compiler_params=_params(("arbitrary",), 52),
        name="outproj_router",
    )(att_p, yc_p, x_p, att_s, yc_s, x_s, w_top, w_bot, norm_g, rw, rb)


def _dispatch_copy(xf_ref, xs_hbm, sem, src_row, dst_row):
    return pltpu.make_async_copy(xf_ref.at[:, src_row, :],
                                 xs_hbm.at[pl.ds(dst_row * SLAB, SLAB)], sem)


def _zero_block_copy(zero_s, xs_hbm, sem, blk):
    span = MOE_BM * SLAB
    return pltpu.make_async_copy(zero_s, xs_hbm.at[pl.ds(blk * span, span)], sem)


def _dispatch_kernel(dest_ref, zero_blk_ref, xf_ref, xs_hbm, zero_s, sem):
    tm = DISPATCH_TM

    @pl.when(pl.program_id(0) == 0)
    def _():
        zero_s[...] = jnp.zeros(zero_s.shape, U32)

        def fill(j, start):
            blk = zero_blk_ref[0, 0, j]

            @pl.when(blk >= 0)
            def _():
                copy = _zero_block_copy(zero_s, xs_hbm, sem, blk)
                if start:
                    copy.start()
                else:
                    copy.wait()

        lax.fori_loop(0, zero_blk_ref.shape[-1], lambda j, c: (fill(j, True), c)[1], 0)
        lax.fori_loop(0, zero_blk_ref.shape[-1], lambda j, c: (fill(j, False), c)[1], 0)

    def issue(t, carry):
        for k in range(TOP_K):
            _dispatch_copy(xf_ref, xs_hbm, sem, t, dest_ref[0, 0, TOP_K * t + k]).start()
        return carry

    def drain(t, carry):
        for k in range(TOP_K):
            _dispatch_copy(xf_ref, xs_hbm, sem, t, dest_ref[0, 0, TOP_K * t + k]).wait()
        return carry

    lax.fori_loop(0, tm, issue, 0, unroll=ROW_DMA_UNROLL)
    lax.fori_loop(0, tm, drain, 0, unroll=ROW_DMA_UNROLL)


def _dispatch(dest, zero_blk, xf, cap):
    t_all = xf.shape[1]
    tm = DISPATCH_TM
    dest3 = dest.reshape(t_all // tm, 1, TOP_K * tm)
    zero_blk3 = zero_blk.reshape(1, 1, -1)
    return pl.pallas_call(
        _dispatch_kernel,
        grid=(t_all // tm,),
        in_specs=[
            pl.BlockSpec((1, 1, TOP_K * tm), lambda i: (i, 0, 0), memory_space=pltpu.SMEM),
            pl.BlockSpec(zero_blk3.shape, lambda i: (0, 0, 0), memory_space=pltpu.SMEM),
            pl.BlockSpec((SLAB, tm, LANES), lambda i: (0, i, 0)),
        ],
        out_specs=pl.BlockSpec(memory_space=pl.ANY),
        out_shape=jax.ShapeDtypeStruct((cap * SLAB, LANES), U32),
        scratch_shapes=[pltpu.VMEM((MOE_BM * SLAB, LANES), U32), pltpu.SemaphoreType.DMA(())],
        compiler_params=_params(("arbitrary",), 24),
        name="moe_dispatch",
    )(dest3, zero_blk3, xf)


def _expert_weight_copies(hbm_refs, stage_refs, sems, e):
    return [pltpu.make_async_copy(hbm.at[e], stage, sems.at[n])
            for n, (hbm, stage) in enumerate(zip(hbm_refs, stage_refs))]


def _experts_kernel(blk_e_ref, first_ref, next_e_ref, nused_ref, x_ref, wg_hbm, wu_hbm, wd_hbm, o_ref,
                    wg_f, wu_f, wd_f, wgu_s, wd_s, sems):
    i = pl.program_id(0)
    hbm_refs = (wg_hbm, wu_hbm, wd_hbm)
    stage_refs = (wg_f, wu_f, wd_f)

    @pl.when(i == 0)
    def _():
        for copy in _expert_weight_copies(hbm_refs, stage_refs, sems, blk_e_ref[0]):
            copy.start()

    @pl.when(i < nused_ref[0])
    def _():
        @pl.when(first_ref[i] == 1)
        def _():
            gate_copy, up_copy, down_copy = _expert_weight_copies(hbm_refs, stage_refs, sems, blk_e_ref[i])
            gate_copy.wait()
            wgu_s[:, 0:D_EXPERT] = wg_f[...].astype(BF16)
            up_copy.wait()
            wgu_s[:, D_EXPERT:2 * D_EXPERT] = wu_f[...].astype(BF16)
            down_copy.wait()
            wd_s[...] = wd_f[...].astype(BF16)

            @pl.when(next_e_ref[i] >= 0)
            def _():
                for copy in _expert_weight_copies(hbm_refs, stage_refs, sems, next_e_ref[i]):
                    copy.start()

        x_lo, x_hi = _unpack_bf16_pair(x_ref[...].reshape(MOE_BM, SLAB, LANES).reshape(MOE_BM, PACK_COLS))
        x = jnp.concatenate([x_lo, x_hi], axis=1).astype(BF16)
        gu = jnp.dot(x, wgu_s[...], preferred_element_type=F32)
        g = gu[:, 0:D_EXPERT]
        u = gu[:, D_EXPERT:2 * D_EXPERT]
        h = g / (1.0 + jnp.exp(-g)) * u
        out = jnp.dot(h.astype(BF16), wd_s[...], preferred_element_type=F32)
        packed = _pack_bf16_pair(out[:, 0:PACK_COLS], out[:, PACK_COLS:D_MODEL])
        o_ref[...] = packed.reshape(MOE_BM, SLAB, LANES).reshape(MOE_BM * SLAB, LANES)

    @pl.when(i >= nused_ref[0])
    def _():
        o_ref[...] = jnp.zeros(o_ref.shape, U32)


def _experts(blk_e, first, next_e, nused, xs, w_gate, w_up, w_down):
    cap = xs.shape[0] // SLAB
    bm = MOE_BM
    rows = lambda i, be, fi, ne, nu: (jnp.minimum(i, nu[0] - 1), 0)
    grid_spec = pltpu.PrefetchScalarGridSpec(
        num_scalar_prefetch=4,
        grid=(cap // bm,),
        in_specs=[
            pl.BlockSpec((bm * SLAB, LANES), rows),
            pl.BlockSpec(memory_space=pl.ANY),
            pl.BlockSpec(memory_space=pl.ANY),
            pl.BlockSpec(memory_space=pl.ANY),
        ],
        out_specs=pl.BlockSpec((bm * SLAB, LANES), lambda i, be, fi, ne, nu: (i, 0)),
        scratch_shapes=[
            pltpu.VMEM((D_MODEL, D_EXPERT), F32),
            pltpu.VMEM((D_MODEL, D_EXPERT), F32),
            pltpu.VMEM((D_EXPERT, D_MODEL), F32),
            pltpu.VMEM((D_MODEL, 2 * D_EXPERT), BF16),
            pltpu.VMEM((D_EXPERT, D_MODEL), BF16),
            pltpu.SemaphoreType.DMA((3,)),
        ],
    )
    return pl.pallas_call(
        _experts_kernel,
        grid_spec=grid_spec,
        out_shape=jax.ShapeDtypeStruct((cap * SLAB, LANES), U32),
        compiler_params=_params(("arbitrary",), 48),
        name="moe_experts",
    )(blk_e, first, next_e, nused, xs, w_gate, w_up, w_down)


def _combine_copy(outs_hbm, g_s, sems, slot, src_row, k, t):
    return pltpu.make_async_copy(outs_hbm.at[pl.ds(src_row * SLAB, SLAB)],
                                 g_s.at[slot, k, :, t, :], sems.at[slot])


def _combine_kernel(dest_ref, dest_next_ref, x1_ref, route_ref, outs_hbm, y_ref, g_s, sems, *, n_tiles):
    i = pl.program_id(0)
    tm = x1_ref.shape[0]
    slot = i % 2

    def gather(table_ref, into, start):
        def body(t, carry):
            for k in range(TOP_K):
                copy = _combine_copy(outs_hbm, g_s, sems, into, table_ref[0, 0, TOP_K * t + k], k, t)
                if start:
                    copy.start()
                else:
                    copy.wait()
            return carry
        lax.fori_loop(0, tm, body, 0, unroll=ROW_DMA_UNROLL)

    @pl.when(i == 0)
    def _():
        gather(dest_ref, 0, True)

    @pl.when(i + 1 < n_tiles)
    def _():
        gather(dest_next_ref, 1 - slot, True)

    gather(dest_ref, slot, False)
    r = route_ref[...]
    w0 = jnp.broadcast_to(r[:, 2:3], (tm, LANES))
    w1 = jnp.broadcast_to(r[:, 3:4], (tm, LANES))
    for c in range(SLAB):
        lo = slice(c * LANES, (c + 1) * LANES)
        hi = slice(PACK_COLS + c * LANES, PACK_COLS + (c + 1) * LANES)
        lo0, hi0 = _unpack_bf16_pair(g_s[slot, 0, c])
        lo1, hi1 = _unpack_bf16_pair(g_s[slot, 1, c])
        y_ref[:, lo] = x1_ref[:, lo] + w0 * lo0 + w1 * lo1
        y_ref[:, hi] = x1_ref[:, hi] + w0 * hi0 + w1 * hi1


def _combine(dest, x1, route, outs, row_off, t):
    tm = COMBINE_TM
    t_all = x1.shape[0]
    off = row_off // tm
    dest3 = dest.reshape(t_all // tm, 1, TOP_K * tm)
    n = t // tm
    return pl.pallas_call(
        functools.partial(_combine_kernel, n_tiles=n),
        grid=(n,),
        in_specs=[
            pl.BlockSpec((1, 1, TOP_K * tm), lambda i: (i + off, 0, 0), memory_space=pltpu.SMEM),
            pl.BlockSpec((1, 1, TOP_K * tm), lambda i: (jnp.minimum(i + 1, n - 1) + off, 0, 0),
                         memory_space=pltpu.SMEM),
            pl.BlockSpec((tm, D_MODEL), lambda i: (i + off, 0)),
            pl.BlockSpec((tm, ROUTE_LANES), lambda i: (i + off, 0)),
            pl.BlockSpec(memory_space=pl.ANY),
        ],
        out_specs=pl.BlockSpec((tm, D_MODEL), lambda i: (i, 0)),
        out_shape=jax.ShapeDtypeStruct((t, D_MODEL), F32),
        scratch_shapes=[pltpu.VMEM((2, TOP_K, SLAB, tm, LANES), U32), pltpu.SemaphoreType.DMA((2,))],
        compiler_params=_params(("arbitrary",), 40),
        name="moe_combine",
    )(dest3, dest3, x1, route, outs)


def _routing_tables(route, lane_counts, n_blocks):
    bm = MOE_BM
    flat_e = route[:, 0:TOP_K].astype(jnp.int32).reshape(-1)
    rank = route[:, 2 * TOP_K:3 * TOP_K].astype(jnp.int32).reshape(-1)
    counts = lane_counts[0, N_GROUPS:N_GROUPS + N_EXPERTS].astype(jnp.int32)
    padded = (counts + bm - 1) // bm * bm
    pad_end = jnp.cumsum(padded)
    pad_start = pad_end - padded
    experts = jnp.arange(N_EXPERTS, dtype=jnp.int32)
    onehot = (flat_e[:, None] == experts[None, :]).astype(BF16)
    start_blk = jnp.dot(onehot, (pad_start // bm).astype(BF16), preferred_element_type=F32)
    dest = start_blk.astype(jnp.int32) * bm + rank
    nused = (pad_end[-1] // bm).astype(jnp.int32)
    blk = jnp.arange(n_blocks, dtype=jnp.int32)
    blk_row = jnp.minimum(blk, nused - 1) * bm
    blk_e = jnp.sum((pad_end[None, :] <= blk_row[:, None]).astype(jnp.int32), axis=1)
    blk_e = jnp.minimum(blk_e, N_EXPERTS - 1)
    first = jnp.concatenate([jnp.ones((1,), jnp.int32),
                             (blk_e[1:] != blk_e[:-1]).astype(jnp.int32)])
    later_owner = jnp.where((counts[None, :] > 0) & (experts[None, :] > experts[:, None]),
                            experts[None, :], N_EXPERTS)
    next_owner = jnp.min(later_owner, axis=1)
    next_owner = jnp.where(next_owner == N_EXPERTS, -1, next_owner).astype(jnp.int32)
    last_blk = jnp.where(counts > 0, pad_end // bm - 1, -1).astype(jnp.int32)
    tail = nused + jnp.arange(N_EXPERTS, dtype=jnp.int32)
    tail_blk = jnp.where(tail < n_blocks, tail, -1)
    zero_blk = jnp.concatenate([last_blk, tail_blk])
    return dest, zero_blk, blk_e, first, next_owner[blk_e], nused.reshape(1)


def _rope_tables(pos):
    half = HEAD_DIM // 2
    inv = ROPE_THETA ** (-jnp.arange(half, dtype=F32) / half)
    ang = pos.astype(F32)[:, None] * inv[None, :]
    cos = jnp.cos(ang)
    sin = jnp.sin(ang)
    cos_h = jnp.concatenate([cos, cos], axis=-1)
    sin_h = jnp.concatenate([-sin, sin], axis=-1)
    reps = LANES // HEAD_DIM
    return jnp.tile(cos_h, (1, reps)), jnp.tile(sin_h, (1, reps))


def kernel(x_prompt, x_sample, cache_k, cache_v, state_conv, norm_mix_g, w_in, q_norm_g, k_norm_g, lambda_q1, lambda_k1, lambda_q2, lambda_k2, subln_g, conv_w, conv_norm_g, w_out, norm_ffn_g, router_group_w, router_group_b, router_expert_w, router_expert_b, expert_w_gate, expert_w_up, expert_w_down):
    assert w_in.shape[0] == 1, "single-layer step"
    bp, sp, _ = x_prompt.shape
    bs, ss, _ = x_sample.shape
    past = cache_k.shape[2]
    assert bp == 1 and sp % ATT_T == 0
    tp = bp * sp
    ts = bs * ss
    t_all = tp + ts
    lambda_init = 0.8 - 0.6 * math.exp(-0.3 * 0)

    w_in_bf = w_in[0].astype(BF16)
    w_out_bf = w_out[0].astype(BF16)
    w_top, w_bot = w_out_bf[:ATT_WIDTH], w_out_bf[ATT_WIDTH:]
    ng = norm_mix_g[0].reshape(1, D_MODEL)
    qg = jnp.tile(q_norm_g[0], QK_WIDTH // HEAD_DIM).reshape(1, QK_WIDTH)
    kg = jnp.tile(k_norm_g[0], QK_WIDTH // HEAD_DIM).reshape(1, QK_WIDTH)
    head_of = jnp.arange(IN_CH, dtype=jnp.int32) // HEAD_DIM
    gmat = jnp.where(head_of[:, None] == head_of[None, :], 1.0 / HEAD_DIM, 0.0).astype(BF16)
    lamv = jnp.stack([lambda_q1[0], lambda_k1[0], lambda_q2[0], lambda_k2[0]]).astype(F32)
    sg = subln_g[0].reshape(1, V_DIM)
    cw = conv_w[0]
    cng = conv_norm_g[0].reshape(1, CONV_CH)
    cos_p, sin_p = _rope_tables(jnp.arange(sp, dtype=jnp.int32))
    cos_s, sin_s = _rope_tables(jnp.tile(past + jnp.arange(ss, dtype=jnp.int32), bs))

    zero_conv = jnp.zeros((1, CONV_K - 1, CONV_CH), F32)
    qt_p, kf_p, kb_p, vf_p, vt_p, yc_p, tail_p = _inproj(
        x_prompt.reshape(tp, D_MODEL), ng, w_in_bf, qg, kg, gmat, cos_p, sin_p, zero_conv, cw, cng,
        tm=ATT_T, nseq=1, carry=True, qv_transposed=True)
    att_p = _attn_prompt(qt_p, kb_p, vt_p, lamv, subln_g[0].reshape(V_DIM, 1), lambda_init)

    seqs_per_tile = ATT_T // ss
    q_s, kf_s, kb_s, vf_s, vb_s, yc_s, tail_s = _inproj(
        x_sample.reshape(ts, D_MODEL), ng, w_in_bf, qg, kg, gmat, cos_s, sin_s, state_conv[0], cw, cng,
        tm=ATT_T, nseq=seqs_per_tile, carry=False, qv_transposed=False)
    kt_cache = jnp.transpose(cache_k[0], (0, 2, 3, 4, 1)).reshape(bs, QK_WIDTH, past)
    att_s = _attn_sample(q_s, kt_cache, cache_v[0].reshape(bs, past * N_HEADS, V_DIM),
                         kb_s, vb_s, lamv, sg, lambda_init)

    rw = jnp.zeros((D_MODEL, ROUTE_LANES), F32)
    rw = rw.at[:, 0:N_GROUPS].set(router_group_w[0]).at[:, N_GROUPS:N_GROUPS + N_EXPERTS].set(router_expert_w[0])
    rb = jnp.zeros((1, ROUTE_LANES), F32)
    rb = rb.at[0, 0:N_GROUPS].set(router_group_b[0]).at[0, N_GROUPS:N_GROUPS + N_EXPERTS].set(router_expert_b[0])
    nf = norm_ffn_g[0].reshape(1, D_MODEL)
    rw_bf = rw.astype(BF16)
    x1, xf, route, lane_counts = _outproj(att_p, yc_p, x_prompt.reshape(tp, D_MODEL), att_s, yc_s,
                                          x_sample.reshape(ts, D_MODEL), w_top, w_bot, nf, rw_bf, rb)

    n = t_all * TOP_K
    n_blocks = n // MOE_BM + N_EXPERTS
    dest, zero_blk, blk_e, first, next_e, nused = _routing_tables(route, lane_counts, n_blocks)
    xs = _dispatch(dest, zero_blk, xf, n_blocks * MOE_BM)
    outs = _experts(blk_e, first, next_e, nused, xs, expert_w_gate[0], expert_w_up[0], expert_w_down[0])
    y_p = _combine(dest, x1, route, outs, 0, tp)
    y_s = _combine(dest, x1, route, outs, tp, ts)

    return (y_p.reshape(bp, sp, D_MODEL),
            y_s.reshape(bs, ss, D_MODEL),
            kf_p.reshape(1, bp, sp, N_HEADS, 2, HEAD_DIM),
            vf_p.reshape(1, bp, sp, N_HEADS, V_DIM),
            tail_p.reshape(1, bp, CONV_K - 1, CONV_CH),
            kf_s.reshape(1, bs, ss, N_HEADS, 2, HEAD_DIM),
            vf_s.reshape(1, bs, ss, N_HEADS, V_DIM),
            tail_s.reshape(1, bs, CONV_K - 1, CONV_CH))
```

```python
import functools
import math

import jax
import jax.numpy as jnp
from jax import lax
from jax.experimental import pallas as pl
from jax.experimental.pallas import tpu as pltpu

F32 = jnp.float32
BF16 = jnp.bfloat16

D_MODEL = 2048
CHUNK = 64
HEAD_DIM = 64
V_DIM = 2 * HEAD_DIM
N_HEADS = 8
QK_WIDTH = N_HEADS * 2 * HEAD_DIM
ATT_WIDTH = N_HEADS * V_DIM
CONV_CH = 1024
CONV_K = 3
ROPE_THETA = 10000.0
N_GROUPS = 8
EXPERTS_PER_GROUP = 8
N_EXPERTS = N_GROUPS * EXPERTS_PER_GROUP
TOP_K = 2
D_EXPERT = D_MODEL // 4
NORM_EPS = 1e-6
SECTION = 1024
N_SECTIONS = 6

LANES = 128
IN_SPS = 2
IN_CH = 256
ATT_AHEAD = 2
ATT_ONES = 16
ATT_QPS = 4
ATT_UNROLL = 4
ATT_CB = 512
ATT_T = 512
OUT_TM = 256
MOE_BM = 256
DISPATCH_TM = 512
ROW_DMA_UNROLL = 8
PACK_COLS = D_MODEL // 2
SLAB = PACK_COLS // LANES
U32 = jnp.uint32
ROUTE_LANES = LANES
MASKED = -1e30
Q_SCALE = HEAD_DIM ** -0.5 * math.log2(math.e)
MIB = 1024 * 1024


def _pack_bf16_pair(lo, hi):
    lo_bits = lax.bitcast_convert_type(lo.astype(BF16).astype(F32), U32)
    hi_bits = lax.bitcast_convert_type(hi.astype(BF16).astype(F32), U32)
    return hi_bits | (lo_bits >> 16)


def _unpack_bf16_pair(words):
    lo = lax.bitcast_convert_type(words << 16, F32)
    hi = lax.bitcast_convert_type(words & U32(0xFFFF0000), F32)
    return lo, hi


def _params(sem, vmem_mib):
    return pltpu.CompilerParams(dimension_semantics=sem, vmem_limit_bytes=vmem_mib * MIB)


def _inproj_kernel(x_ref, ng_ref, w_ref, qg_ref, kg_ref, gmat_ref, cos_ref, sin_ref,
                   prev_ref, cw_ref, cng_ref,
                   q_ref, kf_ref, kb_ref, vf_ref, vb_ref, yc_ref, tail_ref,
                   xn_s, gb_s, c_s, z_s, u_s, *, nseq, carry, qv_transposed):
    i = pl.program_id(0)
    j = pl.program_id(1)
    tm = x_ref.shape[0]
    seq = tm // nseq

    @pl.when(j == 0)
    def _():
        x = x_ref[...]
        inv = lax.rsqrt(jnp.mean(x * x, axis=-1, keepdims=True) + NORM_EPS)
        xn_s[...] = (x * inv * ng_ref[...]).astype(BF16)
        if carry:
            @pl.when(i == 0)
            def _():
                u_s[6:8, :] = prev_ref[0]

    def chunk_dot(col):
        return jnp.dot(xn_s[...], w_ref[:, col:col + IN_CH], preferred_element_type=F32)

    def pipelined(stages):
        acc = chunk_dot(stages[0][0])
        for n, (_, epilogue, cols) in enumerate(stages):
            nxt = chunk_dot(stages[n + 1][0]) if n + 1 < len(stages) else None
            epilogue(acc, cols)
            acc = nxt

    def head_norm_rope(a, g):
        ms = jnp.dot((a * a).astype(BF16), gmat_ref[...], preferred_element_type=F32)
        y = a * lax.rsqrt(ms + NORM_EPS) * g
        reps = IN_CH // LANES
        cos = jnp.concatenate([cos_ref[...]] * reps, axis=1)
        sin = jnp.concatenate([sin_ref[...]] * reps, axis=1)
        lane = lax.broadcasted_iota(jnp.int32, y.shape, 1)
        first = (lane & (HEAD_DIM - 1)) < HEAD_DIM // 2
        partner = jnp.where(first, pltpu.roll(y, IN_CH - HEAD_DIM // 2, 1),
                            pltpu.roll(y, HEAD_DIM // 2, 1))
        return y * cos + partner * sin

    epilogues = []
    for section in range(N_SECTIONS):

        def epilogue(acc, cols, section=section):
            if section == 0:
                q = head_norm_rope(acc, qg_ref[:, cols]) * Q_SCALE
                if qv_transposed:
                    q_ref[0, cols, :] = q.T.astype(BF16)
                else:
                    q_ref[:, cols] = q.astype(BF16)
            elif section == 1:
                k = head_norm_rope(acc, kg_ref[:, cols])
                kf_ref[:, cols] = k
                kb_ref[:, cols] = k.astype(BF16)
            elif section == 2:
                vf_ref[:, cols] = acc
                if qv_transposed:
                    vb_ref[0, cols, :] = acc.T.astype(BF16)
                else:
                    vb_ref[:, cols] = acc.astype(BF16)
            elif section == 3:
                gb_s[:, cols] = acc
            elif section == 4:
                c_s[:, cols] = acc
            else:
                u = c_s[:, cols] * acc
                w0 = cw_ref[0:1, cols]
                w1 = cw_ref[1:2, cols]
                w2 = cw_ref[2:3, cols]
                for s in range(nseq):
                    rows = slice(s * seq, (s + 1) * seq)
                    if not carry:
                        u_s[6:8, cols] = prev_ref[s, :, cols]
                    u_s[8:8 + seq, cols] = u[rows]
                    conv = (w0 * u_s[6:6 + seq, cols] + w1 * u_s[7:7 + seq, cols]
                            + w2 * u_s[8:8 + seq, cols])
                    z_s[rows, cols] = gb_s[rows, cols] * conv
                    tail = u_s[seq + 6:seq + 8, cols]
                    tail_ref[s, :, cols] = tail
                    if carry:
                        u_s[6:8, cols] = tail

        epilogues.append(epilogue)

    n_steps = N_SECTIONS // IN_SPS
    for step in range(n_steps):
        stages = [(sub * SECTION + c * IN_CH, epilogues[step * IN_SPS + sub],
                   slice(c * IN_CH, (c + 1) * IN_CH))
                  for sub in range(IN_SPS) for c in range(SECTION // IN_CH)]

        @pl.when(j == step)
        def _(step=step, stages=stages):
            pipelined(stages)
            if step == n_steps - 1:
                z = z_s[...]
                inv = lax.rsqrt(jnp.mean(z * z, axis=-1, keepdims=True) + NORM_EPS)
                yc_ref[...] = (z * inv * cng_ref[...]).astype(BF16)


def _inproj(x2d, norm_g, w_in_bf, qg, kg, gmat, cos, sin, conv_prev, conv_w, conv_norm_g,
            *, tm, nseq, carry, qv_transposed):
    t = x2d.shape[0]
    ni = t // tm
    seq = tm // nseq
    row = lambda i, j: (i, 0)
    const = lambda i, j: (0, 0)
    if qv_transposed:
        qv_shape = jax.ShapeDtypeStruct((ni, QK_WIDTH, tm), BF16)
        qv_spec = pl.BlockSpec((1, QK_WIDTH, tm), lambda i, j: (i, 0, 0))
    else:
        qv_shape = jax.ShapeDtypeStruct((t, QK_WIDTH), BF16)
        qv_spec = pl.BlockSpec((tm, QK_WIDTH), row)
    if carry:
        prev_spec = pl.BlockSpec((1, CONV_K - 1, CONV_CH), lambda i, j: (0, 0, 0))
        tail_shape = jax.ShapeDtypeStruct((1, CONV_K - 1, CONV_CH), F32)
        tail_spec = pl.BlockSpec((1, CONV_K - 1, CONV_CH), lambda i, j: (0, 0, 0))
    else:
        prev_spec = pl.BlockSpec((nseq, CONV_K - 1, CONV_CH), lambda i, j: (i, 0, 0))
        tail_shape = jax.ShapeDtypeStruct((ni * nseq, CONV_K - 1, CONV_CH), F32)
        tail_spec = pl.BlockSpec((nseq, CONV_K - 1, CONV_CH), lambda i, j: (i, 0, 0))
    kern = functools.partial(_inproj_kernel, nseq=nseq, carry=carry, qv_transposed=qv_transposed)
    return pl.pallas_call(
        kern,
        grid=(ni, N_SECTIONS // IN_SPS),
        in_specs=[
            pl.BlockSpec((tm, D_MODEL), row),
            pl.BlockSpec((1, D_MODEL), const),
            pl.BlockSpec((D_MODEL, IN_SPS * SECTION), lambda i, j: (0, j)),
            pl.BlockSpec((1, SECTION), const),
            pl.BlockSpec((1, SECTION), const),
            pl.BlockSpec((IN_CH, IN_CH), const),
            pl.BlockSpec((tm, LANES), row),
            pl.BlockSpec((tm, LANES), row),
            prev_spec,
            pl.BlockSpec((CONV_K, CONV_CH), const),
            pl.BlockSpec((1, CONV_CH), const),
        ],
        out_specs=[
            qv_spec,
            pl.BlockSpec((tm, QK_WIDTH), row),
            pl.BlockSpec((tm, QK_WIDTH), row),
            pl.BlockSpec((tm, ATT_WIDTH), row),
            qv_spec,
            pl.BlockSpec((tm, CONV_CH), row),
            tail_spec,
        ],
        out_shape=[
            qv_shape,
            jax.ShapeDtypeStruct((t, QK_WIDTH), F32),
            jax.ShapeDtypeStruct((t, QK_WIDTH), BF16),
            jax.ShapeDtypeStruct((t, ATT_WIDTH), F32),
            qv_shape,
            jax.ShapeDtypeStruct((t, CONV_CH), BF16),
            tail_shape,
        ],
        scratch_shapes=[
            pltpu.VMEM((tm, D_MODEL), BF16),
            pltpu.VMEM((tm, CONV_CH), F32),
            pltpu.VMEM((tm, CONV_CH), F32),
            pltpu.VMEM((tm, CONV_CH), F32),
            pltpu.VMEM((seq + 8, CONV_CH), F32),
        ],
        compiler_params=_params(("arbitrary", "arbitrary"), 58),
        name="inproj_carry" if carry else "inproj_seqs",
    )(x2d, norm_g, w_in_bf, qg, kg, gmat, cos, sin, conv_prev, conv_w, conv_norm_g)


def _lambda_full(lamv_ref, lambda_init):
    lv = lamv_ref[...]
    a = jnp.sum(lv[0:1] * lv[1:2], axis=-1, keepdims=True)
    b = jnp.sum(lv[2:3] * lv[3:4], axis=-1, keepdims=True)
    return jnp.exp(a) - jnp.exp(b) + lambda_init


def _split_heads(q):
    lane = lax.broadcasted_iota(jnp.int32, q.shape, 1)
    zero = jnp.zeros_like(q)
    return jnp.where(lane < HEAD_DIM, q, zero), jnp.where(lane >= HEAD_DIM, q, zero)


def _diff_finish(acc, l, lam, g, lambda_init, rows):
    a = acc / l
    d = a[0:rows] - lam * a[rows:2 * rows]
    inv = lax.rsqrt(jnp.mean(d * d, axis=-1, keepdims=True) + NORM_EPS)
    return d * inv * g * (1.0 - lambda_init)


def _attn_prompt_kernel(qt_ref, k_ref, vt_ref, lamv_ref, g_ref, o_ref, qq_s, m_s, acc_s, s_s,
                        *, lambda_init):
    for sub in range(ATT_QPS):
        _attn_prompt_tile(ATT_QPS * pl.program_id(1) + sub, qt_ref.at[sub], k_ref, vt_ref, lamv_ref, g_ref,
                          o_ref.at[pl.ds(sub * ATT_T, ATT_T)], qq_s.at[sub], m_s.at[sub], acc_s.at[sub],
                          s_s.at[sub], lambda_init=lambda_init)


def _attn_prompt_tile(i, qt_ref, k_ref, vt_ref, lamv_ref, g_ref, o_ref, qq_s, m_s, acc_s, s_s,
                      *, lambda_init):
    t = ATT_T
    qt = qt_ref[...]
    feat = lax.broadcasted_iota(jnp.int32, qt.shape, 0)
    zero = jnp.zeros_like(qt)
    qq_s[:, 0:t] = jnp.where(feat < HEAD_DIM, qt, zero)
    qq_s[:, t:2 * t] = jnp.where(feat >= HEAD_DIM, qt, zero)
    m_s[...] = jnp.full(m_s.shape, MASKED, F32)
    acc_s[...] = jnp.zeros(acc_s.shape, F32)
    ones_rows = (lax.broadcasted_iota(jnp.int32, (ATT_ONES, t), 0) == 0).astype(BF16)

    nblk = 2 * t // ATT_CB

    def scores(kt, cb):
        k = k_ref[pl.ds(pl.multiple_of(kt * t, t), t), :]
        s = jnp.dot(k, qq_s[:, cb * ATT_CB:(cb + 1) * ATT_CB], preferred_element_type=F32)
        s_s[cb] = s.astype(BF16)

    def step(kt, diagonal):
        vt = jnp.concatenate([vt_ref[kt], ones_rows], axis=0)
        for cb in range(nblk):
            cs = slice(cb * ATT_CB, (cb + 1) * ATT_CB)
            s = s_s[cb]
            if diagonal:
                key = lax.broadcasted_iota(jnp.int32, s.shape, 0)
                qry = lax.broadcasted_iota(jnp.int32, s.shape, 1) + (cb * ATT_CB) % t
                s = s + jnp.where((key // CHUNK) <= (qry // CHUNK), 0.0, MASKED).astype(BF16)
            m_prev = m_s[:, cs]
            m_new = jnp.maximum(m_prev, jnp.max(s, axis=0, keepdims=True).astype(F32))
            alpha = jnp.exp2(m_prev - m_new)
            p = jnp.exp2(s - m_new.astype(BF16))
            m_s[:, cs] = m_new
            ahead = cb + ATT_AHEAD
            if ahead < nblk:
                scores(kt, ahead)
            elif not diagonal:
                scores(kt + 1, ahead - nblk)
            acc_s[:, cs] = alpha * acc_s[:, cs] + jnp.dot(vt, p, preferred_element_type=F32)

    def group_body(g, carry):
        for u in range(ATT_UNROLL):
            step(ATT_UNROLL * g + u, False)
        return carry

    def single_body(kt, carry):
        step(kt, False)
        return carry

    for cb in range(ATT_AHEAD):
        scores(0, cb)
    n_grouped = i // ATT_UNROLL * ATT_UNROLL
    lax.fori_loop(0, i // ATT_UNROLL, group_body, 0)
    lax.fori_loop(n_grouped, i, single_body, 0)
    step(i, True)
    lam = _lambda_full(lamv_ref, lambda_init)
    a = acc_s[0:V_DIM, :] / acc_s[V_DIM:V_DIM + 1, :]
    d = a[:, 0:t] - lam * a[:, t:2 * t]
    inv = lax.rsqrt(jnp.mean(d * d, axis=0, keepdims=True) + NORM_EPS)
    y = d * inv * g_ref[...] * (1.0 - lambda_init)
    o_ref[...] = y.T.astype(BF16)


def _attn_prompt(qt_bf, k_bf, vt_bf, lamv, subln_g_col, lambda_init):
    nq = qt_bf.shape[0]
    t = nq * ATT_T
    kern = functools.partial(_attn_prompt_kernel, lambda_init=lambda_init)
    return pl.pallas_call(
        kern,
        grid=(N_HEADS, nq // ATT_QPS),
        in_specs=[
            pl.BlockSpec((ATT_QPS, V_DIM, ATT_T), lambda h, i: (i, h, 0)),
            pl.BlockSpec((t, V_DIM), lambda h, i: (0, h)),
            pl.BlockSpec((nq, V_DIM, ATT_T), lambda h, i: (0, h, 0)),
            pl.BlockSpec((4, HEAD_DIM), lambda h, i: (0, 0)),
            pl.BlockSpec((V_DIM, 1), lambda h, i: (0, 0)),
        ],
        out_specs=pl.BlockSpec((ATT_QPS * ATT_T, V_DIM), lambda h, i: (i, h)),
        out_shape=jax.ShapeDtypeStruct((t, ATT_WIDTH), BF16),
        scratch_shapes=[
            pltpu.VMEM((ATT_QPS, V_DIM, 2 * ATT_T), BF16),
            pltpu.VMEM((ATT_QPS, 1, 2 * ATT_T), F32),
            pltpu.VMEM((ATT_QPS, V_DIM + ATT_ONES, 2 * ATT_T), F32),
            pltpu.VMEM((ATT_QPS, 2 * ATT_T // ATT_CB, ATT_T, ATT_CB), BF16),
        ],
        compiler_params=_params(("arbitrary", "arbitrary"), 48),
        name="attn_prompt",
    )(qt_bf, k_bf, vt_bf, lamv, subln_g_col)


def _attn_sample_kernel(q_ref, kc_ref, vc_ref, kn_ref, vn_ref, lamv_ref, g_ref, o_ref,
                        *, lambda_init, past):
    rows = q_ref.shape[0]
    lam = _lambda_full(lamv_ref, lambda_init)
    contract_last = (((1,), (1,)), ((), ()))
    for h in range(N_HEADS):
        hs = slice(h * V_DIM, (h + 1) * V_DIM)
        q1, q2 = _split_heads(q_ref[:, hs])
        qq = jnp.concatenate([q1, q2], axis=0)
        s_c = jnp.dot(qq, kc_ref[0, hs, :].astype(BF16), preferred_element_type=F32)
        s_n = lax.dot_general(qq, kn_ref[:, hs], contract_last, preferred_element_type=F32)
        r = lax.broadcasted_iota(jnp.int32, s_n.shape, 0)
        c = lax.broadcasted_iota(jnp.int32, s_n.shape, 1)
        s_n = jnp.where(((past + c) // CHUNK) <= ((past + r % rows) // CHUNK), s_n, MASKED)
        m = jnp.maximum(jnp.max(s_c, axis=-1, keepdims=True), jnp.max(s_n, axis=-1, keepdims=True))
        p_c = jnp.exp2(s_c - m)
        p_n = jnp.exp2(s_n - m)
        l = jnp.sum(p_c, axis=-1, keepdims=True) + jnp.sum(p_n, axis=-1, keepdims=True)
        v_c = vc_ref[0, pl.ds(h, past, stride=N_HEADS), :].astype(BF16)
        acc = (jnp.dot(p_c.astype(BF16), v_c, preferred_element_type=F32)
               + jnp.dot(p_n.astype(BF16), vn_ref[:, hs], preferred_element_type=F32))
        o_ref[:, hs] = _diff_finish(acc, l, lam, g_ref[...], lambda_init, rows).astype(BF16)


def _attn_sample(q_bf, kt_cache, v_cache, k_bf, v_bf, lamv, subln_g, lambda_init):
    nb, _, past = kt_cache.shape
    rows = q_bf.shape[0] // nb
    kern = functools.partial(_attn_sample_kernel, lambda_init=lambda_init, past=past)
    new_spec = pl.BlockSpec((rows, ATT_WIDTH), lambda b: (b, 0))
    return pl.pallas_call(
        kern,
        grid=(nb,),
        in_specs=[
            new_spec,
            pl.BlockSpec((1, QK_WIDTH, past), lambda b: (b, 0, 0)),
            pl.BlockSpec((1, past * N_HEADS, V_DIM), lambda b: (b, 0, 0)),
            new_spec, new_spec,
            pl.BlockSpec((4, HEAD_DIM), lambda b: (0, 0)),
            pl.BlockSpec((1, V_DIM), lambda b: (0, 0)),
        ],
        out_specs=new_spec,
        out_shape=jax.ShapeDtypeStruct(q_bf.shape, BF16),
        compiler_params=_params(("arbitrary",), 48),
        name="attn_sample",
    )(q_bf, kt_cache, v_cache, k_bf, v_bf, lamv, subln_g)


def _outproj_kernel(att_p_ref, yc_p_ref, x_p_ref, att_s_ref, yc_s_ref, x_s_ref,
                    wt_ref, wb_ref, ng_ref, rw_ref, rb_ref, x1_ref, xf_ref, route_ref, count_ref, count_s,
                    *, n_prompt_tiles):
    i = pl.program_id(0)
    shared = (wt_ref, wb_ref, ng_ref, rw_ref, rb_ref, x1_ref, xf_ref, route_ref, count_ref, count_s)

    @pl.when(i == 0)
    def _():
        count_s[...] = jnp.zeros(count_s.shape, F32)

    @pl.when(i < n_prompt_tiles)
    def _():
        _outproj_tile(att_p_ref, yc_p_ref, x_p_ref, *shared)

    @pl.when(i >= n_prompt_tiles)
    def _():
        _outproj_tile(att_s_ref, yc_s_ref, x_s_ref, *shared)


def _outproj_tile(att_ref, yc_ref, x_ref, wt_ref, wb_ref, ng_ref, rw_ref, rb_ref,
                  x1_ref, xf_ref, route_ref, count_ref, count_s):
    o = (jnp.dot(att_ref[...], wt_ref[...], preferred_element_type=F32)
         + jnp.dot(yc_ref[...], wb_ref[...], preferred_element_type=F32))
    x1 = x_ref[...] + o
    x1_ref[...] = x1
    xf = x1 * lax.rsqrt(jnp.mean(x1 * x1, axis=-1, keepdims=True) + NORM_EPS) * ng_ref[...]
    for c in range(SLAB):
        lo = slice(c * LANES, (c + 1) * LANES)
        hi = slice(PACK_COLS + c * LANES, PACK_COLS + (c + 1) * LANES)
        xf_ref[c] = _pack_bf16_pair(xf[:, lo], xf[:, hi])
    logits =jnp.dot(xf.astype(BF16), rw_ref[...], preferred_element_type=F32) + rb_ref[...]

    lane = lax.broadcasted_iota(jnp.int32, logits.shape, 1)
    neg = -jnp.inf
    is_group = lane < N_GROUPS
    gl = jnp.where(is_group, logits, neg)
    gmax = jnp.max(gl, axis=-1, keepdims=True)
    grp = jnp.min(jnp.where(gl == gmax, lane, ROUTE_LANES), axis=-1, keepdims=True)
    gsum = jnp.sum(jnp.where(is_group, jnp.exp(gl - gmax), 0.0), axis=-1, keepdims=True)
    g_w = 1.0 / gsum
    e_lane = lane - N_GROUPS
    in_grp = (e_lane >= 0) & (e_lane < N_EXPERTS) & ((e_lane // EXPERTS_PER_GROUP) == grp)
    el = jnp.where(in_grp, logits, neg)
    t1 = jnp.max(el, axis=-1, keepdims=True)
    i1 = jnp.min(jnp.where(el == t1, lane, ROUTE_LANES), axis=-1, keepdims=True)
    el2 = jnp.where(lane == i1, neg, el)
    t2 = jnp.max(el2, axis=-1, keepdims=True)
    i2 = jnp.min(jnp.where(el2 == t2, lane, ROUTE_LANES), axis=-1, keepdims=True)
    r21 = jnp.exp(t2 - t1)
    w0 = g_w / (1.0 + r21)
    w1 = g_w * r21 / (1.0 + r21)
    e0 = (i1 - N_GROUPS).astype(F32)
    e1 = (i2 - N_GROUPS).astype(F32)
    rows = logits.shape[0]
    picked0 = lane == i1
    picked1 = lane == i2
    member = (picked0 | picked1).astype(BF16)
    earlier = (lax.broadcasted_iota(jnp.int32, (rows, rows), 1)
               < lax.broadcasted_iota(jnp.int32, (rows, rows), 0)).astype(BF16)
    before = jnp.dot(earlier, member, preferred_element_type=F32) + count_s[...]
    r0 = jnp.sum(jnp.where(picked0, before, 0.0), axis=-1, keepdims=True)
    r1 = jnp.sum(jnp.where(picked1, before, 0.0), axis=-1, keepdims=True)
    count_s[...] += jnp.sum(member.astype(F32), axis=0, keepdims=True)
    count_ref[...] = count_s[...]
    route_ref[...] = jnp.where(lane == 0, e0, jnp.where(lane == 1, e1,
                               jnp.where(lane == 2, w0, jnp.where(lane == 3, w1,
                               jnp.where(lane == 4, r0, jnp.where(lane == 5, r1, 0.0))))))


def _outproj(att_p, yc_p, x_p, att_s, yc_s, x_s, w_top, w_bot, norm_g, rw, rb):
    tm = OUT_TM
    n_p = x_p.shape[0] // tm
    n_s = x_s.shape[0] // tm
    t_all = x_p.shape[0] + x_s.shape[0]
    prow = lambda i: (jnp.minimum(i, n_p - 1), 0)
    srow = lambda i: (jnp.maximum(i - n_p, 0), 0)
    row = lambda i: (i, 0)
    const = lambda i: (0, 0)
    kern = functools.partial(_outproj_kernel, n_prompt_tiles=n_p)
    return pl.pallas_call(
        kern,
        grid=(n_p + n_s,),
        in_specs=[
            pl.BlockSpec((tm, ATT_WIDTH), prow),
            pl.BlockSpec((tm, CONV_CH), prow),
            pl.BlockSpec((tm, D_MODEL), prow),
            pl.BlockSpec((tm, ATT_WIDTH), srow),
            pl.BlockSpec((tm, CONV_CH), srow),
            pl.BlockSpec((tm, D_MODEL), srow),
            pl.BlockSpec((ATT_WIDTH, D_MODEL), const),
            pl.BlockSpec((CONV_CH, D_MODEL), const),
            pl.BlockSpec((1, D_MODEL), const),
            pl.BlockSpec((D_MODEL, ROUTE_LANES), const),
            pl.BlockSpec((1, ROUTE_LANES), const),
        ],
        out_specs=[
            pl.BlockSpec((tm, D_MODEL), row),
            pl.BlockSpec((SLAB, tm, LANES), lambda i: (0, i, 0)),
            pl.BlockSpec((tm, ROUTE_LANES), row),
            pl.BlockSpec((1, ROUTE_LANES), const),
        ],
        out_shape=[
            jax.ShapeDtypeStruct((t_all, D_MODEL), F32),
            jax.ShapeDtypeStruct((SLAB, t_all, LANES), U32),
            jax.ShapeDtypeStruct((t_all, ROUTE_LANES), F32),
            jax.ShapeDtypeStruct((1, ROUTE_LANES), F32),
        ],
        scratch_shapes=[pltpu.VMEM((1, ROUTE_LANES), F32)],
        compiler_params=_params(("arbitrary",), 52),
        name="outproj_router",
    )(att_p, yc_p, x_p, att_s, yc_s, x_s, w_top, w_bot, norm_g, rw, rb)


def _dispatch_copy(xf_ref, xs_hbm, sem, src_row, dst_row):
    return pltpu.make_async_copy(xf_ref.at[:, src_row, :],
                                 xs_hbm.at[pl.ds(dst_row * SLAB, SLAB)], sem)


def _zero_block_copy(zero_s, xs_hbm, sem, blk):
    span = MOE_BM * SLAB
    return pltpu.make_async_copy(zero_s, xs_hbm.at[pl.ds(blk * span, span)], sem)


def _dispatch_kernel(dest_ref, zero_blk_ref, xf_ref, xs_hbm, zero_s, sem):
    tm = DISPATCH_TM

    @pl.when(pl.program_id(0) == 0)
    def _():
        zero_s[...] = jnp.zeros(zero_s.shape, U32)

        def fill(j, start):
            blk = zero_blk_ref[0, 0, j]

            @pl.when(blk >= 0)
            def _():
                copy = _zero_block_copy(zero_s, xs_hbm, sem, blk)
                if start:
                    copy.start()
                else:
                    copy.wait()

        lax.fori_loop(0, zero_blk_ref.shape[-1], lambda j, c: (fill(j, True), c)[1], 0)
        lax.fori_loop(0, zero_blk_ref.shape[-1], lambda j, c: (fill(j, False), c)[1], 0)

    def issue(t, carry):
        for k in range(TOP_K):
            _dispatch_copy(xf_ref, xs_hbm, sem, t, dest_ref[0, 0, TOP_K * t + k]).start(priority=k)
        return carry

    def drain(t, carry):
        for k in range(TOP_K):
            _dispatch_copy(xf_ref, xs_hbm, sem, t, dest_ref[0, 0, TOP_K * t + k]).wait()
        return carry

    lax.fori_loop(0, tm, issue, 0, unroll=ROW_DMA_UNROLL)
    lax.fori_loop(0, tm, drain, 0, unroll=ROW_DMA_UNROLL)


def _dispatch(dest, zero_blk, xf, cap):
    t_all = xf.shape[1]
    tm = DISPATCH_TM
    dest3 = dest.reshape(t_all // tm, 1, TOP_K * tm)
    zero_blk3 = zero_blk.reshape(1, 1, -1)
    return pl.pallas_call(
        _dispatch_kernel,
        grid=(t_all // tm,),
        in_specs=[
            pl.BlockSpec((1, 1, TOP_K * tm), lambda i: (i, 0, 0), memory_space=pltpu.SMEM),
            pl.BlockSpec(zero_blk3.shape, lambda i: (0, 0, 0), memory_space=pltpu.SMEM),
            pl.BlockSpec((SLAB, tm, LANES), lambda i: (0, i, 0)),
        ],
        out_specs=pl.BlockSpec(memory_space=pl.ANY),
        out_shape=jax.ShapeDtypeStruct((cap * SLAB, LANES), U32),
        scratch_shapes=[pltpu.VMEM((MOE_BM * SLAB, LANES), U32), pltpu.SemaphoreType.DMA(())],
        compiler_params=_params(("arbitrary",), 24),
        name="moe_dispatch",
    )(dest3, zero_blk3, xf)


def _expert_weight_copies(hbm_refs, stage_refs, sems, e):
    return [pltpu.make_async_copy(hbm.at[e], stage, sems.at[n])
            for n, (hbm, stage) in enumerate(zip(hbm_refs, stage_refs))]


def _experts_kernel(blk_e_ref, first_ref, next_e_ref, nused_ref, x_ref, wg_hbm, wu_hbm, wd_hbm, o_ref,
                    wg_f, wu_f, wd_f, wgu_s, wd_s, sems):
    i = pl.program_id(0)
    hbm_refs = (wg_hbm, wu_hbm, wd_hbm)
    stage_refs = (wg_f, wu_f, wd_f)

    @pl.when(i == 0)
    def _():
        for copy in _expert_weight_copies(hbm_refs, stage_refs, sems, blk_e_ref[0]):
            copy.start()

    @pl.when(i < nused_ref[0])
    def _():
        @pl.when(first_ref[i] == 1)
        def _():
            gate_copy, up_copy, down_copy = _expert_weight_copies(hbm_refs, stage_refs, sems, blk_e_ref[i])
            gate_copy.wait()
            wgu_s[:, 0:D_EXPERT] = wg_f[...].astype(BF16)
            up_copy.wait()
            wgu_s[:, D_EXPERT:2 * D_EXPERT] = wu_f[...].astype(BF16)
            down_copy.wait()
            wd_s[...] = wd_f[...].astype(BF16)

            @pl.when(next_e_ref[i] >= 0)
            def _():
                for copy in _expert_weight_copies(hbm_refs, stage_refs, sems, next_e_ref[i]):
                    copy.start()

        x_lo, x_hi = _unpack_bf16_pair(x_ref[...].reshape(MOE_BM, SLAB, LANES).reshape(MOE_BM, PACK_COLS))
        x = jnp.concatenate([x_lo, x_hi], axis=1).astype(BF16)
        gu = jnp.dot(x, wgu_s[...], preferred_element_type=F32)
        g = gu[:, 0:D_EXPERT]
        u = gu[:, D_EXPERT:2 * D_EXPERT]
        h = g / (1.0 + jnp.exp(-g)) * u
        out = jnp.dot(h.astype(BF16), wd_s[...], preferred_element_type=F32)
        packed = _pack_bf16_pair(out[:, 0:PACK_COLS], out[:, PACK_COLS:D_MODEL])
        o_ref[...] = packed.reshape(MOE_BM, SLAB, LANES).reshape(MOE_BM * SLAB, LANES)

    @pl.when(i >= nused_ref[0])
    def _():
        o_ref[...] = jnp.zeros(o_ref.shape, U32)


def _experts(blk_e, first, next_e, nused, xs, w_gate, w_up, w_down):
    cap = xs.shape[0] // SLAB
    bm = MOE_BM
    rows = lambda i, be, fi, ne, nu: (jnp.minimum(i, nu[0] - 1), 0)
    grid_spec = pltpu.PrefetchScalarGridSpec(
        num_scalar_prefetch=4,
        grid=(cap // bm,),
        in_specs=[
            pl.BlockSpec((bm * SLAB, LANES), rows),
            pl.BlockSpec(memory_space=pl.ANY),
            pl.BlockSpec(memory_space=pl.ANY),
            pl.BlockSpec(memory_space=pl.ANY),
        ],
        out_specs=pl.BlockSpec((bm * SLAB, LANES), lambda i, be, fi, ne, nu: (i, 0)),
        scratch_shapes=[
            pltpu.VMEM((D_MODEL, D_EXPERT), F32),
            pltpu.VMEM((D_MODEL, D_EXPERT), F32),
            pltpu.VMEM((D_EXPERT, D_MODEL), F32),
            pltpu.VMEM((D_MODEL, 2 * D_EXPERT), BF16),
            pltpu.VMEM((D_EXPERT, D_MODEL), BF16),
            pltpu.SemaphoreType.DMA((3,)),
        ],
    )
    return pl.pallas_call(
        _experts_kernel,
        grid_spec=grid_spec,
        out_shape=jax.ShapeDtypeStruct((cap * SLAB, LANES), U32),
        compiler_params=_params(("arbitrary",), 48),
        name="moe_experts",
    )(blk_e, first, next_e, nused, xs, w_gate, w_up, w_down)


def _combine_copy(outs_hbm, g_s, sems, slot, src_row, k, t):
    return pltpu.make_async_copy(outs_hbm.at[pl.ds(src_row * SLAB, SLAB)],
                                 g_s.at[slot, k, :, t, :], sems.at[slot])


def _combine_kernel(dest_ref, dest_next_ref, x1_ref, route_ref, outs_hbm, y_ref, g_s, sems, *, n_tiles):
    i = pl.program_id(0)
    tm = x1_ref.shape[0]
    slot = i % 2

    def gather(table_ref, into, start):
        def body(t, carry):
            for k in range(TOP_K):
                copy = _combine_copy(outs_hbm, g_s, sems, into, table_ref[0, 0, TOP_K * t + k], k, t)
                if start:
                    copy.start(priority=k)
                else:
                    copy.wait()
            return carry
        lax.fori_loop(0, tm, body, 0, unroll=ROW_DMA_UNROLL)

    @pl.when(i == 0)
    def _():
        gather(dest_ref, 0, True)

    @pl.when(i + 1 < n_tiles)
    def _():
        gather(dest_next_ref, 1 - slot, True)

    gather(dest_ref, slot, False)
    r = route_ref[...]
    w0 = jnp.broadcast_to(r[:, 2:3], (tm, LANES))
    w1 = jnp.broadcast_to(r[:, 3:4], (tm, LANES))
    for c in range(SLAB):
        lo = slice(c * LANES, (c + 1) * LANES)
        hi = slice(PACK_COLS + c * LANES, PACK_COLS + (c + 1) * LANES)
        lo0, hi0 = _unpack_bf16_pair(g_s[slot, 0, c])
        lo1, hi1 = _unpack_bf16_pair(g_s[slot, 1, c])
        y_ref[:, lo] = x1_ref[:, lo] + w0 * lo0 + w1 * lo1
        y_ref[:, hi] = x1_ref[:, hi] + w0 * hi0 + w1 * hi1


def _combine(dest, x1, route, outs, row_off, t):
    tm = OUT_TM
    t_all = x1.shape[0]
    off = row_off // tm
    dest3 = dest.reshape(t_all // tm, 1, TOP_K * tm)
    n = t // tm
    return pl.pallas_call(
        functools.partial(_combine_kernel, n_tiles=n),
        grid=(n,),
        in_specs=[
            pl.BlockSpec((1, 1, TOP_K * tm), lambda i: (i + off, 0, 0), memory_space=pltpu.SMEM),
            pl.BlockSpec((1, 1, TOP_K * tm), lambda i: (jnp.minimum(i + 1, n - 1) + off, 0, 0),
                         memory_space=pltpu.SMEM),
            pl.BlockSpec((tm, D_MODEL), lambda i: (i + off, 0)),
            pl.BlockSpec((tm, ROUTE_LANES), lambda i: (i + off, 0)),
            pl.BlockSpec(memory_space=pl.ANY),
        ],
        out_specs=pl.BlockSpec((tm, D_MODEL), lambda i: (i, 0)),
        out_shape=jax.ShapeDtypeStruct((t, D_MODEL), F32),
        scratch_shapes=[pltpu.VMEM((2, TOP_K, SLAB, tm, LANES), U32), pltpu.SemaphoreType.DMA((2,))],
        compiler_params=_params(("arbitrary",), 32),
        name="moe_combine",
    )(dest3, dest3, x1, route, outs)


def _routing_tables(route, lane_counts, n_blocks):
    bm = MOE_BM
    flat_e = route[:, 0:TOP_K].astype(jnp.int32).reshape(-1)
    rank = route[:, 2 * TOP_K:3 * TOP_K].astype(jnp.int32).reshape(-1)
    counts = lane_counts[0, N_GROUPS:N_GROUPS + N_EXPERTS].astype(jnp.int32)
    padded = (counts + bm - 1) // bm * bm
    pad_end = jnp.cumsum(padded)
    pad_start = pad_end - padded
    experts = jnp.arange(N_EXPERTS, dtype=jnp.int32)
    onehot = (flat_e[:, None] == experts[None, :]).astype(BF16)
    start_blk = jnp.dot(onehot, (pad_start // bm).astype(BF16), preferred_element_type=F32)
    dest = start_blk.astype(jnp.int32) * bm + rank
    nused = (pad_end[-1] // bm).astype(jnp.int32)
    blk = jnp.arange(n_blocks, dtype=jnp.int32)
    blk_row = jnp.minimum(blk, nused - 1) * bm
    blk_e = jnp.sum((pad_end[None, :] <= blk_row[:, None]).astype(jnp.int32), axis=1)
    blk_e = jnp.minimum(blk_e, N_EXPERTS - 1)
    first = jnp.concatenate([jnp.ones((1,), jnp.int32),
                             (blk_e[1:] != blk_e[:-1]).astype(jnp.int32)])
    later_owner = jnp.where((counts[None, :] > 0) & (experts[None, :] > experts[:, None]),
                            experts[None, :], N_EXPERTS)
    next_owner = jnp.min(later_owner, axis=1)
    next_owner = jnp.where(next_owner == N_EXPERTS, -1, next_owner).astype(jnp.int32)
    last_blk = jnp.where(counts > 0, pad_end // bm - 1, -1).astype(jnp.int32)
    tail = nused + jnp.arange(N_EXPERTS, dtype=jnp.int32)
    tail_blk = jnp.where(tail < n_blocks, tail, -1)
    zero_blk = jnp.concatenate([last_blk, tail_blk])
    return dest, zero_blk, blk_e, first, next_owner[blk_e], nused.reshape(1)


def _rope_tables(pos):
    half = HEAD_DIM // 2
    inv = ROPE_THETA ** (-jnp.arange(half, dtype=F32) / half)
    ang = pos.astype(F32)[:, None] * inv[None, :]
    cos = jnp.cos(ang)
    sin = jnp.sin(ang)
    cos_h = jnp.concatenate([cos, cos], axis=-1)
    sin_h = jnp.concatenate([-sin, sin], axis=-1)
    reps = LANES // HEAD_DIM
    return jnp.tile(cos_h, (1, reps)), jnp.tile(sin_h, (1, reps))


def kernel(x_prompt, x_sample, cache_k, cache_v, state_conv, norm_mix_g, w_in, q_norm_g, k_norm_g, lambda_q1, lambda_k1, lambda_q2, lambda_k2, subln_g, conv_w, conv_norm_g, w_out, norm_ffn_g, router_group_w, router_group_b, router_expert_w, router_expert_b, expert_w_gate, expert_w_up, expert_w_down):
    assert w_in.shape[0] == 1, "single-layer step"
    bp, sp, _ = x_prompt.shape
    bs, ss, _ = x_sample.shape
    past = cache_k.shape[2]
    assert bp == 1 and sp % ATT_T == 0
    tp = bp * sp
    ts = bs * ss
    t_all = tp + ts
    lambda_init = 0.8 - 0.6 * math.exp(-0.3 * 0)

    w_in_bf = w_in[0].astype(BF16)
    w_out_bf = w_out[0].astype(BF16)
    w_top, w_bot = w_out_bf[:ATT_WIDTH], w_out_bf[ATT_WIDTH:]
    ng = norm_mix_g[0].reshape(1, D_MODEL)
    qg = jnp.tile(q_norm_g[0], QK_WIDTH // HEAD_DIM).reshape(1, QK_WIDTH)
    kg = jnp.tile(k_norm_g[0], QK_WIDTH // HEAD_DIM).reshape(1, QK_WIDTH)
    head_of = jnp.arange(IN_CH, dtype=jnp.int32) // HEAD_DIM
    gmat = jnp.where(head_of[:, None] == head_of[None, :], 1.0 / HEAD_DIM, 0.0).astype(BF16)
    lamv = jnp.stack([lambda_q1[0], lambda_k1[0], lambda_q2[0], lambda_k2[0]]).astype(F32)
    sg = subln_g[0].reshape(1, V_DIM)
    cw = conv_w[0]
    cng = conv_norm_g[0].reshape(1, CONV_CH)
    cos_p, sin_p = _rope_tables(jnp.arange(sp, dtype=jnp.int32))
    cos_s, sin_s = _rope_tables(jnp.tile(past + jnp.arange(ss, dtype=jnp.int32), bs))

    zero_conv = jnp.zeros((1, CONV_K - 1, CONV_CH), F32)
    qt_p, kf_p, kb_p, vf_p, vt_p, yc_p, tail_p = _inproj(
        x_prompt.reshape(tp, D_MODEL), ng, w_in_bf, qg, kg, gmat, cos_p, sin_p, zero_conv, cw, cng,
        tm=ATT_T, nseq=1, carry=True, qv_transposed=True)
    att_p = _attn_prompt(qt_p, kb_p, vt_p, lamv, subln_g[0].reshape(V_DIM, 1), lambda_init)

    seqs_per_tile = ATT_T // ss
    q_s, kf_s, kb_s, vf_s, vb_s, yc_s, tail_s = _inproj(
        x_sample.reshape(ts, D_MODEL), ng, w_in_bf, qg, kg, gmat, cos_s, sin_s, state_conv[0], cw, cng,
        tm=ATT_T, nseq=seqs_per_tile, carry=False, qv_transposed=False)
    kt_cache = jnp.transpose(cache_k[0], (0, 2, 3, 4, 1)).reshape(bs, QK_WIDTH, past)
    att_s = _attn_sample(q_s, kt_cache, cache_v[0].reshape(bs, past * N_HEADS, V_DIM),
                         kb_s, vb_s, lamv, sg, lambda_init)

    rw = jnp.zeros((D_MODEL, ROUTE_LANES), F32)
    rw = rw.at[:, 0:N_GROUPS].set(router_group_w[0]).at[:, N_GROUPS:N_GROUPS + N_EXPERTS].set(router_expert_w[0])
    rb = jnp.zeros((1, ROUTE_LANES), F32)
    rb = rb.at[0, 0:N_GROUPS].set(router_group_b[0]).at[0, N_GROUPS:N_GROUPS + N_EXPERTS].set(router_expert_b[0])
    nf = norm_ffn_g[0].reshape(1, D_MODEL)
    rw_bf = rw.astype(BF16)
    x1, xf, route, lane_counts = _outproj(att_p, yc_p, x_prompt.reshape(tp, D_MODEL), att_s, yc_s,
                                          x_sample.reshape(ts, D_MODEL), w_top, w_bot, nf, rw_bf, rb)

    n = t_all * TOP_K
    n_blocks = n // MOE_BM + N_EXPERTS
    dest, zero_blk, blk_e, first, next_e, nused = _routing_tables(route, lane_counts, n_blocks)
    xs = _dispatch(dest, zero_blk, xf, n_blocks * MOE_BM)
    outs = _experts(blk_e, first, next_e, nused, xs, expert_w_gate[0], expert_w_up[0], expert_w_down[0])
    y_p = _combine(dest, x1, route, outs, 0, tp)
    y_s = _combine(dest, x1, route, outs, tp, ts)

    return (y_p.reshape(bp, sp, D_MODEL),
            y_s.reshape(bs, ss, D_MODEL),
            kf_p.reshape(1, bp, sp, N_HEADS, 2, HEAD_DIM),
            vf_p.reshape(1, bp, sp, N_HEADS, V_DIM),
            tail_p.reshape(1, bp, CONV_K - 1, CONV_CH),
            kf_s.reshape(1, bs, ss, N_HEADS, 2, HEAD_DIM),
            vf_s.reshape(1, bs, ss, N_HEADS, V_DIM),
            tail_s.reshape(1, bs, CONV_K - 1, CONV_CH))
```
